```python
import jax, jax.numpy as jnp
from jax import lax
import numpy as np

D_MODEL = 1024
BATCH = 8
SEQ = 4096
DEPTH = 2

N_EVEN = (DEPTH + 1) // 2
N_ODD = DEPTH // 2

CHUNK = 128
SGU_WIDTH = 2 * D_MODEL
SGU_GROUPS = 8
SGU_GROUP_DIM = SGU_WIDTH // SGU_GROUPS
RET_HEADS = 4
RET_DK = D_MODEL // RET_HEADS
RET_DV = 2 * RET_DK
RET_QK = RET_HEADS * RET_DK
RET_V = RET_HEADS * RET_DV
RET_IN = 2 * RET_QK + 2 * RET_V
ROPE_BASE = 10000.0
FFN_DIM = ((8 * D_MODEL // 3 + 127) // 128) * 128
N_EXPERTS = 8
TOP_K = 2
EPS = 1e-6

kernel_name = "hybrid_sgu_retention_moe_adaln"


def rmsnorm(x, g):
    xf = x.astype(jnp.float32)
    xf = xf * lax.rsqrt(jnp.mean(xf * xf, axis=-1, keepdims=True) + EPS)
    return (xf * g.astype(jnp.float32)).astype(x.dtype)


def layernorm(x, g, b):
    xf = x.astype(jnp.float32)
    mu = jnp.mean(xf, axis=-1, keepdims=True)
    var = jnp.mean(jnp.square(xf - mu), axis=-1, keepdims=True)
    y = (xf - mu) * lax.rsqrt(var + EPS) * g.astype(jnp.float32) + b.astype(jnp.float32)
    return y.astype(x.dtype)


def swiglu(h, w_gate_up, w_down):
    gu = h @ w_gate_up
    g, u = jnp.split(gu, 2, axis=-1)
    return (jax.nn.silu(g) * u) @ w_down


def chunked_sgu(h, w_in, ln_g, ln_b, w_s, b_s, w_out):
    bn, s, _ = h.shape
    nc = s // CHUNK
    z = jax.nn.gelu(h @ w_in)
    u, v = jnp.split(z, 2, axis=-1)
    v = layernorm(v, ln_g, ln_b)
    v = v.reshape(bn, nc, CHUNK, SGU_GROUPS, SGU_GROUP_DIM)
    mask = jnp.tril(jnp.ones((CHUNK, CHUNK), dtype=w_s.dtype))
    fv = jnp.einsum('gts,bnsgd->bntgd', w_s * mask, v) + b_s.T[:, :, None]
    y = u * fv.reshape(bn, s, SGU_WIDTH)
    return y @ w_out


def rope(t, positions):
    d = t.shape[-1]
    inv_freq = 1.0 / (ROPE_BASE ** (jnp.arange(0, d, 2, dtype=jnp.float32) / d))
    ang = positions.astype(jnp.float32)[..., None] * inv_freq
    cos = jnp.cos(ang)[:, :, None, :]
    sin = jnp.sin(ang)[:, :, None, :]
    tf = t.astype(jnp.float32)
    t1, t2 = jnp.split(tf, 2, axis=-1)
    return jnp.concatenate([t1 * cos - t2 * sin, t2 * cos + t1 * sin], axis=-1)


def retention(h, w_in, w_out, positions):
    bn, s, _ = h.shape
    nc = s // CHUNK
    proj = h @ w_in
    q, k, v, g = jnp.split(proj, [RET_QK, 2 * RET_QK, 2 * RET_QK + RET_V], axis=-1)
    q = rope(q.reshape(bn, s, RET_HEADS, RET_DK), positions)
    k = rope(k.reshape(bn, s, RET_HEADS, RET_DK), positions) * (RET_DK ** -0.5)
    v = v.reshape(bn, s, RET_HEADS, RET_DV).astype(jnp.float32)

    def to_chunks(t):
        return t.reshape(bn, nc, CHUNK, RET_HEADS, -1).transpose(1, 0, 3, 2, 4)

    qc, kc, vc = to_chunks(q), to_chunks(k), to_chunks(v)
    log_gamma = jnp.log1p(-jnp.exp2(-5.0 - jnp.arange(RET_HEADS, dtype=jnp.float32)))
    idx = jnp.arange(CHUNK, dtype=jnp.float32)
    rel = idx[:, None] - idx[None, :]
    intra = jnp.where(rel >= 0, jnp.exp(log_gamma[:, None, None] * jnp.maximum(rel, 0.0)), 0.0)
    xi = jnp.exp(log_gamma[:, None] * (idx + 1.0))[None, :, :, None]
    zeta = jnp.exp(log_gamma[:, None] * (CHUNK - 1.0 - idx))[None, :, :, None]
    chunk_decay = jnp.exp(log_gamma * CHUNK)[None, :, None, None]

    def step(state, qkv):
        qi, ki, vi = qkv
        scores = jnp.einsum('bhtd,bhsd->bhts', qi, ki) * intra
        o = (jnp.einsum('bhts,bhsv->bhtv', scores, vi)
             + jnp.einsum('bhtd,bhdv->bhtv', qi, state) * xi)
        state = state * chunk_decay + jnp.einsum('bhsd,bhsv->bhdv', ki, vi * zeta)
        return state, o

    state0 = jnp.zeros((bn, RET_HEADS, RET_DK, RET_DV), jnp.float32)
    _, o = lax.scan(step, state0, (qc, kc, vc))
    o = o.transpose(1, 0, 3, 2, 4).reshape(bn, s, RET_HEADS, RET_DV)
    o = o * lax.rsqrt(jnp.mean(o * o, axis=-1, keepdims=True) + EPS)
    o = o.reshape(bn, s, RET_V).astype(h.dtype) * jax.nn.silu(g)
    return o @ w_out


def moe_swiglu(h, w_router, b_router, w_gate_up, w_down):
    bn, s, dm = h.shape
    t = h.reshape(-1, dm)
    logits = (t @ w_router).astype(jnp.float32) + b_router.astype(jnp.float32)
    top_v, top_i = lax.top_k(logits, TOP_K)
    wts = jax.nn.softmax(top_v, axis=-1)
    gates = jnp.sum(jax.nn.one_hot(top_i, N_EXPERTS, dtype=jnp.float32) * wts[..., None], axis=1)
    gates = gates.astype(h.dtype)
    y = jnp.zeros_like(t)
    for e in range(N_EXPERTS):
        y = y + gates[:, e:e + 1] * swiglu(t, w_gate_up[e], w_down[e])
    return y.reshape(bn, s, dm)


def setup_inputs(seed: int = 0) -> dict:
    key = jax.random.key(seed)
    ks = jax.random.split(key, 24)

    def nrm(k, shape, std):
        return jax.random.normal(k, shape, jnp.float32) * std

    D = D_MODEL
    x = nrm(ks[0], (BATCH, SEQ, D), 1.0)
    c = nrm(ks[1], (BATCH, D), 1.0)
    offs = jax.random.randint(ks[2], (BATCH, 1), 0, 1024, dtype=jnp.int32)
    positions = jnp.arange(SEQ, dtype=jnp.int32)[None, :] + offs
    return {
        "x": x,
        "c": c,
        "positions": positions,
        "ada_w": nrm(ks[3], (DEPTH, D, 6 * D), 0.5 * D ** -0.5),
        "ada_b": nrm(ks[4], (DEPTH, 6 * D), 0.02),
        "mix_norm_g": 1.0 + nrm(ks[5], (DEPTH, D), 0.02),
        "ffn_norm_g": 1.0 + nrm(ks[6], (DEPTH, D), 0.02),
        "sgu_w_in": nrm(ks[7], (N_EVEN, D, 2 * SGU_WIDTH), D ** -0.5),
        "sgu_ln_g": 1.0 + nrm(ks[8], (N_EVEN, SGU_WIDTH), 0.02),
        "sgu_ln_b": nrm(ks[9], (N_EVEN, SGU_WIDTH), 0.02),
        "sgu_w_s": nrm(ks[10], (N_EVEN, SGU_GROUPS, CHUNK, CHUNK), 0.5 * CHUNK ** -0.5),
        "sgu_b_s": 1.0 + nrm(ks[11], (N_EVEN, SGU_GROUPS, CHUNK), 0.1),
        "sgu_w_out": nrm(ks[12], (N_EVEN, SGU_WIDTH, D), SGU_WIDTH ** -0.5),
        "ffn_w_gate_up": nrm(ks[13], (N_EVEN, D, 2 * FFN_DIM), D ** -0.5),
        "ffn_w_down": nrm(ks[14], (N_EVEN, FFN_DIM, D), FFN_DIM ** -0.5),
        "ret_w_in": nrm(ks[15], (N_ODD, D, RET_IN), D ** -0.5),
        "ret_w_out": nrm(ks[16], (N_ODD, RET_V, D), RET_V ** -0.5),
        "moe_w_router": nrm(ks[17], (N_ODD, D, N_EXPERTS), D ** -0.5),
        "moe_b_router": nrm(ks[18], (N_ODD, N_EXPERTS), 0.01),
        "moe_w_gate_up": nrm(ks[19], (N_ODD, N_EXPERTS, D, 2 * FFN_DIM), D ** -0.5),
        "moe_w_down": nrm(ks[20], (N_ODD, N_EXPERTS, FFN_DIM, D), FFN_DIM ** -0.5),
        "final_norm_g": 1.0 + nrm(ks[21], (D,), 0.02),
    }


def reference(x, c, positions, ada_w, ada_b, mix_norm_g, ffn_norm_g,
              sgu_w_in, sgu_ln_g, sgu_ln_b, sgu_w_s, sgu_b_s, sgu_w_out,
              ffn_w_gate_up, ffn_w_down, ret_w_in, ret_w_out,
              moe_w_router, moe_b_router, moe_w_gate_up, moe_w_down,
              final_norm_g):
    sc = jax.nn.silu(c)
    for i in range(DEPTH):
        j = i // 2
        mod = sc @ ada_w[i] + ada_b[i]
        sh1, s1, g1, sh2, s2, g2 = [m[:, None, :] for m in jnp.split(mod, 6, axis=-1)]
        h = rmsnorm(x, mix_norm_g[i]) * (1.0 + s1) + sh1
        if i % 2 == 0:
            y = chunked_sgu(h, sgu_w_in[j], sgu_ln_g[j], sgu_ln_b[j],
                            sgu_w_s[j], sgu_b_s[j], sgu_w_out[j])
        else:
            y = retention(h, ret_w_in[j], ret_w_out[j], positions)
        x = x + g1 * y
        h = rmsnorm(x, ffn_norm_g[i]) * (1.0 + s2) + sh2
        if i % 2 == 0:
            y = swiglu(h, ffn_w_gate_up[j], ffn_w_down[j])
        else:
            y = moe_swiglu(h, moe_w_router[j], moe_b_router[j],
                           moe_w_gate_up[j], moe_w_down[j])
        x = x + g2 * y
    return rmsnorm(x, final_norm_g)
```

```python
import functools

import jax
import jax.numpy as jnp
import numpy as np
from jax import lax
from jax.experimental import pallas as pl
from jax.experimental.pallas import tpu as pltpu

F32 = jnp.float32
BF16 = jnp.bfloat16

CHUNK = 128
SGU_GROUPS = 8
RET_HEADS = 4
N_EXPERTS = 8
ROPE_BASE = 10000.0
EPS = 1e-6
LANES = 128
VMEM_LIMIT_BYTES = 56 * 1024 * 1024


def _const_spec(shape):
    nd = len(shape)
    return pl.BlockSpec(shape, lambda *_: (0,) * nd, pipeline_mode=pl.Buffered(1))


def _modnorm(x, g, scale, shift):
    ms = jnp.mean(x * x, axis=-1, keepdims=True)
    return (x * lax.rsqrt(ms + EPS) * g) * (1.0 + scale) + shift


def _adaln_kernel(c_ref, w_ref, b_ref, o_ref):
    sc = jax.nn.silu(c_ref[...])
    o_ref[0] = jnp.dot(sc, w_ref[0], precision=lax.Precision.HIGHEST,
                       preferred_element_type=F32) + b_ref[0]


def _adaln(c, ada_w, ada_b):
    depth, d, six_d = ada_w.shape
    b = c.shape[0]
    tn = 1024
    return pl.pallas_call(
        _adaln_kernel,
        out_shape=jax.ShapeDtypeStruct((depth, b, six_d), F32),
        grid=(depth, six_d // tn),
        in_specs=[
            pl.BlockSpec((b, d), lambda l, j: (0, 0)),
            pl.BlockSpec((1, d, tn), lambda l, j: (l, 0, j)),
            pl.BlockSpec((1, 1, tn), lambda l, j: (l, 0, j)),
        ],
        out_specs=pl.BlockSpec((1, b, tn), lambda l, j: (l, 0, j)),
        compiler_params=pltpu.CompilerParams(
            dimension_semantics=("parallel", "parallel"),
            vmem_limit_bytes=VMEM_LIMIT_BYTES),
        name="adaln",
    )(c, ada_w, ada_b.reshape(depth, 1, six_d))


def _sgu_kernel(x_ref, mod_ref, ng_ref, win_ref, lng_ref, lnb_ref, ws_ref, bs_ref,
                wout_ref, o_ref, u_ref, vn_ref, y_ref, *, tm, width, col_block):
    x = x_ref[0]
    mod = mod_ref[0]
    h = _modnorm(x, ng_ref[...], mod[1:2], mod[0:1]).astype(BF16)
    nb = width // col_block
    vs = []
    for j in range(2 * nb):
        z = jnp.dot(h, win_ref[:, j * col_block:(j + 1) * col_block],
                    preferred_element_type=F32)
        z = jax.nn.gelu(z)
        if j < nb:
            u_ref[:, j * col_block:(j + 1) * col_block] = z
        else:
            vs.append(z)
    s1 = sum(jnp.sum(v, axis=-1, keepdims=True) for v in vs)
    mu = s1 * (1.0 / width)
    s2 = sum(jnp.sum((v - mu) * (v - mu), axis=-1, keepdims=True) for v in vs)
    rstd = lax.rsqrt(s2 * (1.0 / width) + EPS)
    for j, v in enumerate(vs):
        sl = slice(j * col_block, (j + 1) * col_block)
        vn_ref[:, sl] = ((v - mu) * rstd * lng_ref[:, sl] + lnb_ref[:, sl]).astype(BF16)
    gd = width // SGU_GROUPS
    row = lax.broadcasted_iota(jnp.int32, (CHUNK, CHUNK), 0)
    col = lax.broadcasted_iota(jnp.int32, (CHUNK, CHUNK), 1)
    causal = row >= col
    for g in range(SGU_GROUPS):
        w = jnp.where(causal, ws_ref[g], jnp.zeros((), BF16))
        cs = slice(g * gd, (g + 1) * gd)
        for c in range(tm // CHUNK):
            rs = slice(c * CHUNK, (c + 1) * CHUNK)
            fv = jnp.dot(w, vn_ref[rs, cs], preferred_element_type=F32) + bs_ref[g]
            y_ref[rs, cs] = (u_ref[rs, cs] * fv).astype(BF16)
    out = jnp.dot(y_ref[...], wout_ref[...], preferred_element_type=F32)
    o_ref[0] = x + mod[2:3] * out


def _sgu_layer(x, mod, norm_g, w_in, ln_g, ln_b, w_s, b_s, w_out, *, tm=512):
    b, s, d = x.shape
    width = w_out.shape[0]
    gd = width // SGU_GROUPS
    bs_full = jnp.broadcast_to(b_s[:, :, None], (SGU_GROUPS, CHUNK, gd))
    kern = functools.partial(_sgu_kernel, tm=tm, width=width, col_block=512)
    return pl.pallas_call(
        kern,
        out_shape=jax.ShapeDtypeStruct((b, s, d), F32),
        grid=(b, s // tm),
        in_specs=[
            pl.BlockSpec((1, tm, d), lambda i, j: (i, j, 0)),
            pl.BlockSpec((1, 6, d), lambda i, j: (i, 0, 0)),
            _const_spec((1, d)),
            _const_spec((d, 2 * width)),
            _const_spec((1, width)),
            _const_spec((1, width)),
            _const_spec((SGU_GROUPS, CHUNK, CHUNK)),
            _const_spec((SGU_GROUPS, CHUNK, gd)),
            _const_spec((width, d)),
        ],
        out_specs=pl.BlockSpec((1, tm, d), lambda i, j: (i, j, 0)),
        scratch_shapes=[
            pltpu.VMEM((tm, width), F32),
            pltpu.VMEM((tm, width), BF16),
            pltpu.VMEM((tm, width), BF16),
        ],
        compiler_params=pltpu.CompilerParams(
            dimension_semantics=("parallel", "parallel"),
            vmem_limit_bytes=VMEM_LIMIT_BYTES),
        name="sgu",
    )(x, mod, norm_g.reshape(1, d), w_in.astype(BF16), ln_g.reshape(1, width),
      ln_b.reshape(1, width), w_s.astype(BF16), bs_full, w_out.astype(BF16))


def _ffn_kernel(x_ref, mod_ref, ng_ref, wgu_ref, wd_ref, o_ref, *, ffn, col_block):
    x = x_ref[0]
    mod = mod_ref[0]
    h = _modnorm(x, ng_ref[...], mod[4:5], mod[3:4]).astype(BF16)
    acc = jnp.zeros(x.shape, F32)
    for j in range(ffn // col_block):
        g = jnp.dot(h, wgu_ref[:, j * col_block:(j + 1) * col_block],
                    preferred_element_type=F32)
        u = jnp.dot(h, wgu_ref[:, ffn + j * col_block:ffn + (j + 1) * col_block],
                    preferred_element_type=F32)
        a = (jax.nn.silu(g) * u).astype(BF16)
        acc = acc + jnp.dot(a, wd_ref[j * col_block:(j + 1) * col_block, :],
                            preferred_element_type=F32)
    o_ref[0] = x + mod[5:6] * acc


def _ffn_layer(x, mod, norm_g, w_gate_up, w_down, *, tm=512):
    b, s, d = x.shape
    ffn = w_down.shape[0]
    kern = functools.partial(_ffn_kernel, ffn=ffn, col_block=256)
    return pl.pallas_call(
        kern,
        out_shape=jax.ShapeDtypeStruct((b, s, d), F32),
        grid=(b, s // tm),
        in_specs=[
            pl.BlockSpec((1, tm, d), lambda i, j: (i, j, 0)),
            pl.BlockSpec((1, 6, d), lambda i, j: (i, 0, 0)),
            _const_spec((1, d)),
            _const_spec((d, 2 * ffn)),
            _const_spec((ffn, d)),
        ],
        out_specs=pl.BlockSpec((1, tm, d), lambda i, j: (i, j, 0)),
        compiler_params=pltpu.CompilerParams(
            dimension_semantics=("parallel", "parallel"),
            vmem_limit_bytes=VMEM_LIMIT_BYTES),
        name="ffn",
    )(x, mod, norm_g.reshape(1, d), w_gate_up.astype(BF16), w_down.astype(BF16))


def _retention_kernel(x_ref, pos_ref, mod_ref, ng_ref, invf_ref, win_ref, wout_ref,
                      intra_ref, xi_ref, zeta_ref, o_ref,
                      state_ref, q_ref, qx_ref, k_ref, kz_ref, v_ref, sg_ref, ob_ref,
                      *, tm, dk, dv, decay):
    heads = RET_HEADS
    qk = heads * dk

    @pl.when(pl.program_id(1) == 0)
    def _():
        state_ref[...] = jnp.zeros(state_ref.shape, F32)

    x = x_ref[0]
    mod = mod_ref[0]
    h = _modnorm(x, ng_ref[...], mod[1:2], mod[0:1]).astype(BF16)
    ang = pos_ref[0].astype(F32) * invf_ref[...]
    cos = jnp.cos(ang)
    sin = jnp.sin(ang)
    half = dk // 2
    for hd in range(heads):
        cs = slice(hd * dk, (hd + 1) * dk)
        p = jnp.dot(h, win_ref[:, cs], preferred_element_type=F32)
        t1, t2 = p[:, :half], p[:, half:]
        q = jnp.concatenate([t1 * cos - t2 * sin, t2 * cos + t1 * sin], axis=-1)
        q_ref[:, cs] = q.astype(BF16)
        qx_ref[:, cs] = (q * xi_ref[:, cs]).astype(BF16)
        p = jnp.dot(h, win_ref[:, qk + hd * dk:qk + (hd + 1) * dk],
                    preferred_element_type=F32)
        t1, t2 = p[:, :half], p[:, half:]
        k = jnp.concatenate([t1 * cos - t2 * sin, t2 * cos + t1 * sin], axis=-1)
        k = k * (dk ** -0.5)
        k_ref[:, cs] = k.astype(BF16)
        kz_ref[:, cs] = (k * zeta_ref[:, cs]).astype(BF16)
    for hd in range(heads):
        cs = slice(hd * dv, (hd + 1) * dv)
        v_ref[:, cs] = jnp.dot(h, win_ref[:, 2 * qk + hd * dv:2 * qk + (hd + 1) * dv],
                               preferred_element_type=F32).astype(BF16)
        g = jnp.dot(h, win_ref[:, 2 * qk + heads * dv + hd * dv:
                               2 * qk + heads * dv + (hd + 1) * dv],
                    preferred_element_type=F32)
        sg_ref[:, cs] = jax.nn.silu(g)

    def chunk_body(c, carry):
        rs = pl.ds(pl.multiple_of(c * CHUNK, CHUNK), CHUNK)
        for hd in range(heads):
            ks = slice(hd * dk, (hd + 1) * dk)
            vs = slice(hd * dv, (hd + 1) * dv)
            qh = q_ref[rs, ks]
            kh = k_ref[rs, ks]
            vh = v_ref[rs, vs]
            scores = lax.dot_general(qh, kh, (((1,), (1,)), ((), ())),
                                     preferred_element_type=F32)
            scores = (scores * intra_ref[hd]).astype(BF16)
            st = state_ref[hd]
            o = (jnp.dot(scores, vh, preferred_element_type=F32)
                 + jnp.dot(qx_ref[rs, ks], st.astype(BF16), preferred_element_type=F32))
            state_ref[hd] = st * decay[hd] + lax.dot_general(
                kz_ref[rs, ks], vh, (((0,), (0,)), ((), ())), preferred_element_type=F32)
            on = o * lax.rsqrt(jnp.mean(o * o, axis=-1, keepdims=True) + EPS)
            ob_ref[rs, vs] = (on * sg_ref[rs, vs]).astype(BF16)
        return carry

    lax.fori_loop(0, tm // CHUNK, chunk_body, 0)
    out = jnp.dot(ob_ref[...], wout_ref[...], preferred_element_type=F32)
    o_ref[0] = x + mod[2:3] * out


def _retention_layer(x, positions, mod, norm_g, w_in, w_out, *, tm=512):
    b, s, d = x.shape
    heads = RET_HEADS
    dk = d // heads
    dv = w_out.shape[0] // heads
    qk = heads * dk
    vw = heads * dv
    log_gamma = jnp.log1p(-jnp.exp2(-5.0 - jnp.arange(heads, dtype=F32)))
    idx = jnp.arange(CHUNK, dtype=F32)
    rel = idx[:, None] - idx[None, :]
    intra = jnp.where(rel >= 0, jnp.exp(log_gamma[:, None, None] * jnp.maximum(rel, 0.0)), 0.0)
    xi = jnp.exp(log_gamma[:, None] * (idx + 1.0))
    zeta = jnp.exp(log_gamma[:, None] * (CHUNK - 1.0 - idx))
    xi_t = jnp.tile(jnp.repeat(xi.T, dk, axis=1), (tm // CHUNK, 1))
    zeta_t = jnp.tile(jnp.repeat(zeta.T, dk, axis=1), (tm // CHUNK, 1))
    decay = tuple(float(np.exp(np.log1p(-2.0 ** (-5.0 - hd)) * CHUNK)) for hd in range(heads))
    inv_freq = 1.0 / (ROPE_BASE ** (jnp.arange(0, dk, 2, dtype=F32) / dk))
    kern = functools.partial(_retention_kernel, tm=tm, dk=dk, dv=dv, decay=decay)
    return pl.pallas_call(
        kern,
        out_shape=jax.ShapeDtypeStruct((b, s, d), F32),
        grid=(b, s // tm),
        in_specs=[
            pl.BlockSpec((1, tm, d), lambda i, j: (i, j, 0)),
            pl.BlockSpec((1, tm, 1), lambda i, j: (i, j, 0)),
            pl.BlockSpec((1, 6, d), lambda i, j: (i, 0, 0)),
            _const_spec((1, d)),
            _const_spec((1, dk // 2)),
            _const_spec((d, 2 * qk + 2 * vw)),
            _const_spec((vw, d)),
            _const_spec((heads, CHUNK, CHUNK)),
            _const_spec((tm, qk)),
            _const_spec((tm, qk)),
        ],
        out_specs=pl.BlockSpec((1, tm, d), lambda i, j: (i, j, 0)),
        scratch_shapes=[
            pltpu.VMEM((heads, dk, dv), F32),
            pltpu.VMEM((tm, qk), BF16),
            pltpu.VMEM((tm, qk), BF16),
            pltpu.VMEM((tm, qk), BF16),
            pltpu.VMEM((tm, qk), BF16),
            pltpu.VMEM((tm, vw), BF16),
            pltpu.VMEM((tm, vw), F32),
            pltpu.VMEM((tm, vw), BF16),
        ],
        compiler_params=pltpu.CompilerParams(
            dimension_semantics=("parallel", "arbitrary"),
            vmem_limit_bytes=VMEM_LIMIT_BYTES),
        name="retention",
    )(x, positions.reshape(b, s, 1), mod, norm_g.reshape(1, d), inv_freq.reshape(1, dk // 2),
      w_in.astype(BF16), w_out.astype(BF16), intra, xi_t, zeta_t)


def _router_kernel(x_ref, mod_ref, ng_ref, wr_ref, br_ref, h_ref, gates_ref):
    x = x_ref[0]
    mod = mod_ref[0]
    h = _modnorm(x, ng_ref[...], mod[4:5], mod[3:4])
    h_ref[0] = h.astype(BF16)
    logits = jnp.dot(h, wr_ref[...], precision=lax.Precision.HIGHEST,
                     preferred_element_type=F32) + br_ref[...]
    lane = lax.broadcasted_iota(jnp.int32, logits.shape, 1)
    neg = jnp.float32(-jnp.inf)
    logits = jnp.where(lane < N_EXPERTS, logits, neg)
    m1 = jnp.max(logits, axis=-1, keepdims=True)
    i1 = jnp.min(jnp.where(logits == m1, lane, LANES), axis=-1, keepdims=True)
    rest = jnp.where(lane == i1, neg, logits)
    m2 = jnp.max(rest, axis=-1, keepdims=True)
    i2 = jnp.min(jnp.where(rest == m2, lane, LANES), axis=-1, keepdims=True)
    e2 = jnp.exp(m2 - m1)
    w1 = 1.0 / (1.0 + e2)
    w2 = e2 / (1.0 + e2)
    gates_ref[0] = jnp.where(lane == i1, w1, 0.0) + jnp.where(lane == i2, w2, 0.0)


def _router(x, mod, norm_g, w_router, b_router, *, tm=512):
    b, s, d = x.shape
    ne = w_router.shape[1]
    wr = jnp.zeros((d, LANES), F32).at[:, :ne].set(w_router)
    br = jnp.zeros((1, LANES), F32).at[0, :ne].set(b_router)
    return pl.pallas_call(
        _router_kernel,
        out_shape=(jax.ShapeDtypeStruct((b, s, d), BF16),
                   jax.ShapeDtypeStruct((b, s, LANES), F32)),
        grid=(b, s // tm),
        in_specs=[
            pl.BlockSpec((1, tm, d), lambda i, j: (i, j, 0)),
            pl.BlockSpec((1, 6, d), lambda i, j: (i, 0, 0)),
            _const_spec((1, d)),
            _const_spec((d, LANES)),
            _const_spec((1, LANES)),
        ],
        out_specs=(pl.BlockSpec((1, tm, d), lambda i, j: (i, j, 0)),
                   pl.BlockSpec((1, tm, LANES), lambda i, j: (i, j, 0))),
        compiler_params=pltpu.CompilerParams(
            dimension_semantics=("parallel", "parallel"),
            vmem_limit_bytes=VMEM_LIMIT_BYTES),
        name="router",
    )(x, mod, norm_g.reshape(1, d), wr, br)


def _moe_kernel(x_ref, h_ref, gates_ref, mod_ref, fg_ref, wgu_ref, wd_ref, o_ref, acc_ref,
                *, ffn, col_block):
    e = pl.program_id(2)

    @pl.when(e == 0)
    def _():
        acc_ref[...] = jnp.zeros(acc_ref.shape, F32)

    h = h_ref[0]
    gates = gates_ref[0]
    lane = lax.broadcasted_iota(jnp.int32, gates.shape, 1)
    gate = jnp.sum(jnp.where(lane == e, gates, 0.0), axis=-1, keepdims=True)
    y = jnp.zeros(acc_ref.shape, F32)
    for j in range(ffn // col_block):
        g = jnp.dot(h, wgu_ref[0, :, j * col_block:(j + 1) * col_block],
                    preferred_element_type=F32)
        u = jnp.dot(h, wgu_ref[0, :, ffn + j * col_block:ffn + (j + 1) * col_block],
                    preferred_element_type=F32)
        a = (jax.nn.silu(g) * u).astype(BF16)
        y = y + jnp.dot(a, wd_ref[0, j * col_block:(j + 1) * col_block, :],
                        preferred_element_type=F32)
    acc_ref[...] += gate * y

    @pl.when(e == pl.num_programs(2) - 1)
    def _():
        xo = x_ref[0] + mod_ref[0][5:6] * acc_ref[...]
        ms = jnp.mean(xo * xo, axis=-1, keepdims=True)
        o_ref[0] = xo * lax.rsqrt(ms + EPS) * fg_ref[...]


def _moe_layer(x, h, gates, mod, final_g, w_gate_up, w_down, *, tm=512):
    b, s, d = x.shape
    ne, ffn, _ = w_down.shape
    kern = functools.partial(_moe_kernel, ffn=ffn, col_block=256)
    return pl.pallas_call(
        kern,
        out_shape=jax.ShapeDtypeStruct((b, s, d), F32),
        grid=(b, s // tm, ne),
        in_specs=[
            pl.BlockSpec((1, tm, d), lambda i, j, e: (i, j, 0)),
            pl.BlockSpec((1, tm, d), lambda i, j, e: (i, j, 0)),
            pl.BlockSpec((1, tm, LANES), lambda i, j, e: (i, j, 0)),
            pl.BlockSpec((1, 6, d), lambda i, j, e: (i, 0, 0)),
            _const_spec((1, d)),
            pl.BlockSpec((1, d, 2 * ffn), lambda i, j, e: (e, 0, 0)),
            pl.BlockSpec((1, ffn, d), lambda i, j, e: (e, 0, 0)),
        ],
        out_specs=pl.BlockSpec((1, tm, d), lambda i, j, e: (i, j, 0)),
        scratch_shapes=[pltpu.VMEM((tm, d), F32)],
        compiler_params=pltpu.CompilerParams(
            dimension_semantics=("parallel", "parallel", "arbitrary"),
            vmem_limit_bytes=VMEM_LIMIT_BYTES),
        name="moe",
    )(x, h, gates, mod, final_g.reshape(1, d), w_gate_up.astype(BF16), w_down.astype(BF16))


def kernel(x, c, positions, ada_w, ada_b, mix_norm_g, ffn_norm_g, sgu_w_in, sgu_ln_g, sgu_ln_b,
           sgu_w_s, sgu_b_s, sgu_w_out, ffn_w_gate_up, ffn_w_down, ret_w_in, ret_w_out,
           moe_w_router, moe_b_router, moe_w_gate_up, moe_w_down, final_norm_g):
    b, s, d = x.shape
    depth = ada_w.shape[0]
    assert depth == 2, "layer 0 = SGU + SwiGLU, layer 1 = retention + MoE"
    mod = _adaln(c, ada_w, ada_b).reshape(depth, b, 6, d)
    x = _sgu_layer(x, mod[0], mix_norm_g[0], sgu_w_in[0], sgu_ln_g[0], sgu_ln_b[0],
                   sgu_w_s[0], sgu_b_s[0], sgu_w_out[0])
    x = _ffn_layer(x, mod[0], ffn_norm_g[0], ffn_w_gate_up[0], ffn_w_down[0])
    x = _retention_layer(x, positions, mod[1], mix_norm_g[1], ret_w_in[0], ret_w_out[0])
    h, gates = _router(x, mod[1], ffn_norm_g[1], moe_w_router[0], moe_b_router[0])
    return _moe_layer(x, h, gates, mod[1], final_norm_g, moe_w_gate_up[0], moe_w_down[0])
```

```python
import functools

import jax
import jax.numpy as jnp
import numpy as np
from jax import lax
from jax.experimental import pallas as pl
from jax.experimental.pallas import tpu as pltpu

F32 = jnp.float32
BF16 = jnp.bfloat16

CHUNK = 128
SGU_GROUPS = 8
RET_HEADS = 4
N_EXPERTS = 8
ROPE_BASE = 10000.0
EPS = 1e-6
LANES = 128
VMEM_LIMIT_BYTES = 56 * 1024 * 1024


def _const_spec(shape):
    nd = len(shape)
    return pl.BlockSpec(shape, lambda *_: (0,) * nd, pipeline_mode=pl.Buffered(1))


def _modnorm(x, g, scale, shift):
    ms = jnp.mean(x * x, axis=-1, keepdims=True)
    return (x * lax.rsqrt(ms + EPS) * g) * (1.0 + scale) + shift


def _adaln_kernel(c_ref, w_ref, b_ref, o_ref):
    sc = jax.nn.silu(c_ref[...])
    o_ref[0] = jnp.dot(sc, w_ref[0], precision=lax.Precision.HIGHEST,
                       preferred_element_type=F32) + b_ref[0]


def _adaln(c, ada_w, ada_b):
    depth, d, six_d = ada_w.shape
    b = c.shape[0]
    tn = 1024
    return pl.pallas_call(
        _adaln_kernel,
        out_shape=jax.ShapeDtypeStruct((depth, b, six_d), F32),
        grid=(depth, six_d // tn),
        in_specs=[
            pl.BlockSpec((b, d), lambda l, j: (0, 0)),
            pl.BlockSpec((1, d, tn), lambda l, j: (l, 0, j)),
            pl.BlockSpec((1, 1, tn), lambda l, j: (l, 0, j)),
        ],
        out_specs=pl.BlockSpec((1, b, tn), lambda l, j: (l, 0, j)),
        compiler_params=pltpu.CompilerParams(
            dimension_semantics=("parallel", "parallel"),
            vmem_limit_bytes=VMEM_LIMIT_BYTES),
        name="adaln",
    )(c, ada_w, ada_b.reshape(depth, 1, six_d))


def _sgu_kernel(x_ref, mod_ref, ng_ref, win_ref, lng_ref, lnb_ref, ws_ref, bs_ref,
                wout_ref, o_ref, u_ref, vn_ref, y_ref, *, tm, width, col_block):
    x = x_ref[0]
    mod = mod_ref[0]
    h = _modnorm(x, ng_ref[...], mod[1:2], mod[0:1]).astype(BF16)
    nb = width // col_block
    vs = []
    for j in range(2 * nb):
        z = jnp.dot(h, win_ref[:, j * col_block:(j + 1) * col_block],
                    preferred_element_type=F32)
        z = jax.nn.gelu(z)
        if j < nb:
            u_ref[:, j * col_block:(j + 1) * col_block] = z
        else:
            vs.append(z)
    s1 = sum(jnp.sum(v, axis=-1, keepdims=True) for v in vs)
    mu = s1 * (1.0 / width)
    s2 = sum(jnp.sum((v - mu) * (v - mu), axis=-1, keepdims=True) for v in vs)
    rstd = lax.rsqrt(s2 * (1.0 / width) + EPS)
    for j, v in enumerate(vs):
        sl = slice(j * col_block, (j + 1) * col_block)
        vn_ref[:, sl] = ((v - mu) * rstd * lng_ref[:, sl] + lnb_ref[:, sl]).astype(BF16)
    gd = width // SGU_GROUPS
    row = lax.broadcasted_iota(jnp.int32, (CHUNK, CHUNK), 0)
    col = lax.broadcasted_iota(jnp.int32, (CHUNK, CHUNK), 1)
    causal = row >= col
    for g in range(SGU_GROUPS):
        w = jnp.where(causal, ws_ref[g], jnp.zeros((), BF16))
        cs = slice(g * gd, (g + 1) * gd)
        for c in range(tm // CHUNK):
            rs = slice(c * CHUNK, (c + 1) * CHUNK)
            fv = jnp.dot(w, vn_ref[rs, cs], preferred_element_type=F32) + bs_ref[g]
            y_ref[rs, cs] = (u_ref[rs, cs] * fv).astype(BF16)
    out = jnp.dot(y_ref[...], wout_ref[...], preferred_element_type=F32)
    o_ref[0] = x + mod[2:3] * out


def _sgu_layer(x, mod, norm_g, w_in, ln_g, ln_b, w_s, b_s, w_out, *, tm=512):
    b, s, d = x.shape
    width = w_out.shape[0]
    gd = width // SGU_GROUPS
    bs_full = jnp.broadcast_to(b_s[:, :, None], (SGU_GROUPS, CHUNK, gd))
    kern = functools.partial(_sgu_kernel, tm=tm, width=width, col_block=512)
    return pl.pallas_call(
        kern,
        out_shape=jax.ShapeDtypeStruct((b, s, d), F32),
        grid=(b, s // tm),
        in_specs=[
            pl.BlockSpec((1, tm, d), lambda i, j: (i, j, 0)),
            pl.BlockSpec((1, 6, d), lambda i, j: (i, 0, 0)),
            _const_spec((1, d)),
            _const_spec((d, 2 * width)),
            _const_spec((1, width)),
            _const_spec((1, width)),
            _const_spec((SGU_GROUPS, CHUNK, CHUNK)),
            _const_spec((SGU_GROUPS, CHUNK, gd)),
            _const_spec((width, d)),
        ],
        out_specs=pl.BlockSpec((1, tm, d), lambda i, j: (i, j, 0)),
        scratch_shapes=[
            pltpu.VMEM((tm, width), F32),
            pltpu.VMEM((tm, width), BF16),
            pltpu.VMEM((tm, width), BF16),
        ],
        compiler_params=pltpu.CompilerParams(
            dimension_semantics=("parallel", "parallel"),
            vmem_limit_bytes=VMEM_LIMIT_BYTES),
        name="sgu",
    )(x, mod, norm_g.reshape(1, d), w_in.astype(BF16), ln_g.reshape(1, width),
      ln_b.reshape(1, width), w_s.astype(BF16), bs_full, w_out.astype(BF16))


def _ffn_kernel(x_ref, mod_ref, ng_ref, wgu_ref, wd_ref, o_ref, *, ffn, col_block):
    x = x_ref[0]
    mod = mod_ref[0]
    h = _modnorm(x, ng_ref[...], mod[4:5], mod[3:4]).astype(BF16)
    acc = jnp.zeros(x.shape, F32)
    for j in range(ffn // col_block):
        g = jnp.dot(h, wgu_ref[:, j * col_block:(j + 1) * col_block],
                    preferred_element_type=F32)
        u = jnp.dot(h, wgu_ref[:, ffn + j * col_block:ffn + (j + 1) * col_block],
                    preferred_element_type=F32)
        a = (jax.nn.silu(g) * u).astype(BF16)
        acc = acc + jnp.dot(a, wd_ref[j * col_block:(j + 1) * col_block, :],
                            preferred_element_type=F32)
    o_ref[0] = x + mod[5:6] * acc


def _ffn_layer(x, mod, norm_g, w_gate_up, w_down, *, tm=512):
    b, s, d = x.shape
    ffn = w_down.shape[0]
    kern = functools.partial(_ffn_kernel, ffn=ffn, col_block=256)
    return pl.pallas_call(
        kern,
        out_shape=jax.ShapeDtypeStruct((b, s, d), F32),
        grid=(b, s // tm),
        in_specs=[
            pl.BlockSpec((1, tm, d), lambda i, j: (i, j, 0)),
            pl.BlockSpec((1, 6, d), lambda i, j: (i, 0, 0)),
            _const_spec((1, d)),
            _const_spec((d, 2 * ffn)),
            _const_spec((ffn, d)),
        ],
        out_specs=pl.BlockSpec((1, tm, d), lambda i, j: (i, j, 0)),
        compiler_params=pltpu.CompilerParams(
            dimension_semantics=("parallel", "parallel"),
            vmem_limit_bytes=VMEM_LIMIT_BYTES),
        name="ffn",
    )(x, mod, norm_g.reshape(1, d), w_gate_up.astype(BF16), w_down.astype(BF16))


def _retention_kernel(x_ref, pos_ref, mod_ref, ng_ref, invf_ref, win_ref, wout_ref,
                      intra_ref, xi_ref, zeta_ref, o_ref,
                      state_ref, q_ref, qx_ref, k_ref, kz_ref, v_ref, sg_ref, ob_ref,
                      *, tm, dk, dv, decay):
    heads = RET_HEADS
    qk = heads * dk

    @pl.when(pl.program_id(1) == 0)
    def _():
        state_ref[...] = jnp.zeros(state_ref.shape, F32)

    x = x_ref[0]
    mod = mod_ref[0]
    h = _modnorm(x, ng_ref[...], mod[1:2], mod[0:1]).astype(BF16)
    ang = pos_ref[0].astype(F32) * invf_ref[...]
    cos = jnp.cos(ang)
    sin = jnp.sin(ang)
    half = dk // 2
    for hd in range(heads):
        cs = slice(hd * dk, (hd + 1) * dk)
        p = jnp.dot(h, win_ref[:, cs], preferred_element_type=F32)
        t1, t2 = p[:, :half], p[:, half:]
        q = jnp.concatenate([t1 * cos - t2 * sin, t2 * cos + t1 * sin], axis=-1)
        q_ref[:, cs] = q.astype(BF16)
        qx_ref[:, cs] = (q * xi_ref[:, cs]).astype(BF16)
        p = jnp.dot(h, win_ref[:, qk + hd * dk:qk + (hd + 1) * dk],
                    preferred_element_type=F32)
        t1, t2 = p[:, :half], p[:, half:]
        k = jnp.concatenate([t1 * cos - t2 * sin, t2 * cos + t1 * sin], axis=-1)
        k = k * (dk ** -0.5)
        k_ref[:, cs] = k.astype(BF16)
        kz_ref[:, cs] = (k * zeta_ref[:, cs]).astype(BF16)
    for hd in range(heads):
        cs = slice(hd * dv, (hd + 1) * dv)
        v_ref[:, cs] = jnp.dot(h, win_ref[:, 2 * qk + hd * dv:2 * qk + (hd + 1) * dv],
                               preferred_element_type=F32).astype(BF16)
        g = jnp.dot(h, win_ref[:, 2 * qk + heads * dv + hd * dv:
                               2 * qk + heads * dv + (hd + 1) * dv],
                    preferred_element_type=F32)
        sg_ref[:, cs] = jax.nn.silu(g)

    def chunk_body(c, carry):
        rs = pl.ds(pl.multiple_of(c * CHUNK, CHUNK), CHUNK)
        for hd in range(heads):
            ks = slice(hd * dk, (hd + 1) * dk)
            vs = slice(hd * dv, (hd + 1) * dv)
            qh = q_ref[rs, ks]
            kh = k_ref[rs, ks]
            vh = v_ref[rs, vs]
            scores = lax.dot_general(qh, kh, (((1,), (1,)), ((), ())),
                                     preferred_element_type=F32)
            scores = (scores * intra_ref[hd]).astype(BF16)
            st = state_ref[hd]
            o = (jnp.dot(scores, vh, preferred_element_type=F32)
                 + jnp.dot(qx_ref[rs, ks], st.astype(BF16), preferred_element_type=F32))
            state_ref[hd] = st * decay[hd] + lax.dot_general(
                kz_ref[rs, ks], vh, (((0,), (0,)), ((), ())), preferred_element_type=F32)
            on = o * lax.rsqrt(jnp.mean(o * o, axis=-1, keepdims=True) + EPS)
            ob_ref[rs, vs] = (on * sg_ref[rs, vs]).astype(BF16)
        return carry

    lax.fori_loop(0, tm // CHUNK, chunk_body, 0)
    out = jnp.dot(ob_ref[...], wout_ref[...], preferred_element_type=F32)
    o_ref[0] = x + mod[2:3] * out


def _retention_layer(x, positions, mod, norm_g, w_in, w_out, *, tm=512):
    b, s, d = x.shape
    heads = RET_HEADS
    dk = d // heads
    dv = w_out.shape[0] // heads
    qk = heads * dk
    vw = heads * dv
    log_gamma = jnp.log1p(-jnp.exp2(-5.0 - jnp.arange(heads, dtype=F32)))
    idx = jnp.arange(CHUNK, dtype=F32)
    rel = idx[:, None] - idx[None, :]
    intra = jnp.where(rel >= 0, jnp.exp(log_gamma[:, None, None] * jnp.maximum(rel, 0.0)), 0.0)
    xi = jnp.exp(log_gamma[:, None] * (idx + 1.0))
    zeta = jnp.exp(log_gamma[:, None] * (CHUNK - 1.0 - idx))
    xi_t = jnp.tile(jnp.repeat(xi.T, dk, axis=1), (tm // CHUNK, 1))
    zeta_t = jnp.tile(jnp.repeat(zeta.T, dk, axis=1), (tm // CHUNK, 1))
    decay = tuple(float(np.exp(np.log1p(-2.0 ** (-5.0 - hd)) * CHUNK)) for hd in range(heads))
    inv_freq = 1.0 / (ROPE_BASE ** (jnp.arange(0, dk, 2, dtype=F32) / dk))
    kern = functools.partial(_retention_kernel, tm=tm, dk=dk, dv=dv, decay=decay)
    return pl.pallas_call(
        kern,
        out_shape=jax.ShapeDtypeStruct((b, s, d), F32),
        grid=(b, s // tm),
        in_specs=[
            pl.BlockSpec((1, tm, d), lambda i, j: (i, j, 0)),
            pl.BlockSpec((1, tm, 1), lambda i, j: (i, j, 0)),
            pl.BlockSpec((1, 6, d), lambda i, j: (i, 0, 0)),
            _const_spec((1, d)),
            _const_spec((1, dk // 2)),
            _const_spec((d, 2 * qk + 2 * vw)),
            _const_spec((vw, d)),
            _const_spec((heads, CHUNK, CHUNK)),
            _const_spec((tm, qk)),
            _const_spec((tm, qk)),
        ],
        out_specs=pl.BlockSpec((1, tm, d), lambda i, j: (i, j, 0)),
        scratch_shapes=[
            pltpu.VMEM((heads, dk, dv), F32),
            pltpu.VMEM((tm, qk), BF16),
            pltpu.VMEM((tm, qk), BF16),
            pltpu.VMEM((tm, qk), BF16),
            pltpu.VMEM((tm, qk), BF16),
            pltpu.VMEM((tm, vw), BF16),
            pltpu.VMEM((tm, vw), F32),
            pltpu.VMEM((tm, vw), BF16),
        ],
        compiler_params=pltpu.CompilerParams(
            dimension_semantics=("parallel", "arbitrary"),
            vmem_limit_bytes=VMEM_LIMIT_BYTES),
        name="retention",
    )(x, positions.reshape(b, s, 1), mod, norm_g.reshape(1, d), inv_freq.reshape(1, dk // 2),
      w_in.astype(BF16), w_out.astype(BF16), intra, xi_t, zeta_t)


META_E1, META_E2, META_R1, META_R2, META_W1, META_W2 = range(6)


def _router_kernel(x_ref, mod_ref, ng_ref, wr_ref, br_ref, h_ref, meta_ref, cnt_ref,
                   carry_ref):
    @pl.when((pl.program_id(0) == 0) & (pl.program_id(1) == 0))
    def _():
        carry_ref[...] = jnp.zeros(carry_ref.shape, F32)

    x = x_ref[0]
    mod = mod_ref[0]
    h = _modnorm(x, ng_ref[...], mod[4:5], mod[3:4])
    h_ref[0] = h
    logits = jnp.dot(h, wr_ref[...], precision=lax.Precision.HIGHEST,
                     preferred_element_type=F32) + br_ref[...]
    tm = logits.shape[0]
    lane = lax.broadcasted_iota(jnp.int32, logits.shape, 1)
    neg = jnp.float32(-jnp.inf)
    logits = jnp.where(lane < N_EXPERTS, logits, neg)
    m1 = jnp.max(logits, axis=-1, keepdims=True)
    i1 = jnp.min(jnp.where(logits == m1, lane, LANES), axis=-1, keepdims=True)
    rest = jnp.where(lane == i1, neg, logits)
    m2 = jnp.max(rest, axis=-1, keepdims=True)
    i2 = jnp.min(jnp.where(rest == m2, lane, LANES), axis=-1, keepdims=True)
    e2 = jnp.exp(m2 - m1)
    w1 = 1.0 / (1.0 + e2)
    w2 = e2 / (1.0 + e2)
    sel = jnp.where((lane == i1) | (lane == i2), 1.0, 0.0)
    row = lax.broadcasted_iota(jnp.int32, (tm, tm), 0)
    col = lax.broadcasted_iota(jnp.int32, (tm, tm), 1)
    tri = jnp.where(row > col, 1.0, 0.0).astype(BF16)
    before = jnp.dot(tri, sel.astype(BF16), preferred_element_type=F32) + carry_ref[...]
    r1 = jnp.sum(jnp.where(lane == i1, before, 0.0), axis=-1, keepdims=True)
    r2 = jnp.sum(jnp.where(lane == i2, before, 0.0), axis=-1, keepdims=True)
    carry_ref[...] += jnp.sum(sel, axis=0, keepdims=True)
    cnt_ref[...] = carry_ref[...]
    meta = jnp.zeros(logits.shape, F32)
    for k, val in ((META_E1, i1.astype(F32)), (META_E2, i2.astype(F32)), (META_R1, r1),
                   (META_R2, r2), (META_W1, w1), (META_W2, w2)):
        meta = jnp.where(lane == k, val, meta)
    meta_ref[0] = meta


def _router(x, mod, norm_g, w_router, b_router, *, tm=512):
    b, s, d = x.shape
    ne = w_router.shape[1]
    wr = jnp.zeros((d, LANES), F32).at[:, :ne].set(w_router)
    br = jnp.zeros((1, LANES), F32).at[0, :ne].set(b_router)
    return pl.pallas_call(
        _router_kernel,
        out_shape=(jax.ShapeDtypeStruct((b, s, d), F32),
                   jax.ShapeDtypeStruct((b, s, LANES), F32),
                   jax.ShapeDtypeStruct((1, LANES), F32)),
        grid=(b, s // tm),
        in_specs=[
            pl.BlockSpec((1, tm, d), lambda i, j: (i, j, 0)),
            pl.BlockSpec((1, 6, d), lambda i, j: (i, 0, 0)),
            _const_spec((1, d)),
            _const_spec((d, LANES)),
            _const_spec((1, LANES)),
        ],
        out_specs=(pl.BlockSpec((1, tm, d), lambda i, j: (i, j, 0)),
                   pl.BlockSpec((1, tm, LANES), lambda i, j: (i, j, 0)),
                   pl.BlockSpec((1, LANES), lambda i, j: (0, 0))),
        scratch_shapes=[pltpu.VMEM((1, LANES), F32)],
        compiler_params=pltpu.CompilerParams(
            dimension_semantics=("arbitrary", "arbitrary"),
            vmem_limit_bytes=VMEM_LIMIT_BYTES),
        name="router",
    )(x, mod, norm_g.reshape(1, d), wr, br)


N_ZERO_SLOTS = 2 * N_EXPERTS


def _dispatch_kernel(info_ref, pos_ref, h_hbm, hs_hbm, zeros_ref, sem, zsem, *, td, tr):
    i = pl.program_id(0)

    def zero_copy(k):
        start = pl.multiple_of(info_ref[N_ZERO_SLOTS + k] * tr, tr)
        return pltpu.make_async_copy(zeros_ref, hs_hbm.at[pl.ds(start, tr)], zsem)

    @pl.when(i == 0)
    def _():
        zeros_ref[...] = jnp.zeros(zeros_ref.shape, F32)
        for k in range(N_ZERO_SLOTS):
            @pl.when(info_ref[k] == 1)
            def _():
                zero_copy(k).start()
        for k in range(N_ZERO_SLOTS):
            @pl.when(info_ref[k] == 1)
            def _():
                zero_copy(k).wait()

    base = i * td

    def body(t, carry):
        src = h_hbm.at[pl.ds(base + t, 1)]
        pltpu.make_async_copy(src, hs_hbm.at[pl.ds(pos_ref[0, 0, t], 1)], sem).start()
        pltpu.make_async_copy(src, hs_hbm.at[pl.ds(pos_ref[0, 0, td + t], 1)], sem).start()
        return carry

    lax.fori_loop(0, td, body, 0, unroll=8)
    pltpu.make_async_copy(h_hbm.at[pl.ds(0, 2 * td)], hs_hbm.at[pl.ds(0, 2 * td)], sem).wait()


def _dispatch(h, pos, zero_info, *, n_rows, td, tr):
    n, d = h.shape
    kern = functools.partial(_dispatch_kernel, td=td, tr=tr)
    return pl.pallas_call(
        kern,
        out_shape=jax.ShapeDtypeStruct((n_rows, d), F32),
        grid_spec=pltpu.PrefetchScalarGridSpec(
            num_scalar_prefetch=1,
            grid=(n // td,),
            in_specs=[
                pl.BlockSpec((1, 1, 2 * td), lambda i, info: (i, 0, 0),
                             memory_space=pltpu.SMEM),
                pl.BlockSpec(memory_space=pl.ANY),
            ],
            out_specs=pl.BlockSpec(memory_space=pl.ANY),
            scratch_shapes=[pltpu.VMEM((tr, d), F32), pltpu.SemaphoreType.DMA,
                            pltpu.SemaphoreType.DMA],
        ),
        compiler_params=pltpu.CompilerParams(
            dimension_semantics=("arbitrary",), vmem_limit_bytes=VMEM_LIMIT_BYTES),
        name="dispatch",
    )(zero_info, pos, h)


def _experts_kernel(te_ref, nu_ref, hs_ref, wgu_ref, wd_ref, y_ref, *, ffn, col_block):
    i = pl.program_id(0)

    @pl.when(i < nu_ref[0])
    def _():
        h = hs_ref[...].astype(BF16)
        y = jnp.zeros(y_ref.shape, F32)
        for j in range(ffn // col_block):
            g = jnp.dot(h, wgu_ref[0, :, j * col_block:(j + 1) * col_block],
                        preferred_element_type=F32)
            u = jnp.dot(h, wgu_ref[0, :, ffn + j * col_block:ffn + (j + 1) * col_block],
                        preferred_element_type=F32)
            a = (jax.nn.silu(g) * u).astype(BF16)
            y = y + jnp.dot(a, wd_ref[0, j * col_block:(j + 1) * col_block, :],
                            preferred_element_type=F32)
        y_ref[...] = y

    @pl.when(i >= nu_ref[0])
    def _():
        y_ref[...] = jnp.zeros(y_ref.shape, F32)


def _experts(hs, tile_expert, n_used, w_gate_up, w_down, *, tr):
    n_rows, d = hs.shape
    ne, ffn, _ = w_down.shape
    kern = functools.partial(_experts_kernel, ffn=ffn, col_block=256)
    return pl.pallas_call(
        kern,
        out_shape=jax.ShapeDtypeStruct((n_rows, d), F32),
        grid_spec=pltpu.PrefetchScalarGridSpec(
            num_scalar_prefetch=2,
            grid=(n_rows // tr,),
            in_specs=[
                pl.BlockSpec((tr, d), lambda i, te, nu: (jnp.minimum(i, nu[0] - 1), 0)),
                pl.BlockSpec((1, d, 2 * ffn), lambda i, te, nu: (te[i], 0, 0)),
                pl.BlockSpec((1, ffn, d), lambda i, te, nu: (te[i], 0, 0)),
            ],
            out_specs=pl.BlockSpec((tr, d), lambda i, te, nu: (i, 0)),
        ),
        compiler_params=pltpu.CompilerParams(
            dimension_semantics=("arbitrary",), vmem_limit_bytes=VMEM_LIMIT_BYTES),
        name="experts",
    )(tile_expert, n_used, hs, w_gate_up.astype(BF16), w_down.astype(BF16))


def _combine_kernel(posc_ref, posn_ref, x_ref, meta_ref, mod_ref, fg_ref, y_hbm, o_ref,
                    ybuf, sem, *, tc):
    i = pl.program_id(0)
    slot = lax.rem(i, 2)

    def issue(pos_ref, to_slot):
        def body(t, carry):
            pltpu.make_async_copy(y_hbm.at[pl.ds(pos_ref[0, 0, t], 1)],
                                  ybuf.at[to_slot, pl.ds(t, 1)], sem.at[to_slot]).start()
            return carry
        lax.fori_loop(0, 2 * tc, body, 0, unroll=8)

    @pl.when(i == 0)
    def _():
        issue(posc_ref, 0)

    @pl.when(i + 1 < pl.num_programs(0))
    def _():
        issue(posn_ref, 1 - slot)

    pltpu.make_async_copy(y_hbm.at[pl.ds(0, 2 * tc)], ybuf.at[slot], sem.at[slot]).wait()
    meta = meta_ref[...]
    lane = lax.broadcasted_iota(jnp.int32, meta.shape, 1)
    w1 = jnp.sum(jnp.where(lane == META_W1, meta, 0.0), axis=-1, keepdims=True)
    w2 = jnp.sum(jnp.where(lane == META_W2, meta, 0.0), axis=-1, keepdims=True)
    y = w1 * ybuf[slot, pl.ds(0, tc), :] + w2 * ybuf[slot, pl.ds(tc, tc), :]
    xo = x_ref[...] + mod_ref[0][5:6] * y
    ms = jnp.mean(xo * xo, axis=-1, keepdims=True)
    o_ref[...] = xo * lax.rsqrt(ms + EPS) * fg_ref[...]


def _combine(x, meta, mod, final_g, y_sorted, pos, *, seq, tc):
    n, d = x.shape
    nblk = n // tc
    kern = functools.partial(_combine_kernel, tc=tc)
    smem_pos = lambda f: pl.BlockSpec((1, 1, 2 * tc), f, memory_space=pltpu.SMEM)
    return pl.pallas_call(
        kern,
        out_shape=jax.ShapeDtypeStruct((n, d), F32),
        grid=(nblk,),
        in_specs=[
            smem_pos(lambda i: (i, 0, 0)),
            smem_pos(lambda i: (jnp.minimum(i + 1, nblk - 1), 0, 0)),
            pl.BlockSpec((tc, d), lambda i: (i, 0)),
            pl.BlockSpec((tc, LANES), lambda i: (i, 0)),
            pl.BlockSpec((1, 6, d), lambda i: ((i * tc) // seq, 0, 0)),
            _const_spec((1, d)),
            pl.BlockSpec(memory_space=pl.ANY),
        ],
        out_specs=pl.BlockSpec((tc, d), lambda i: (i, 0)),
        scratch_shapes=[pltpu.VMEM((2, 2 * tc, d), F32), pltpu.SemaphoreType.DMA((2,))],
        compiler_params=pltpu.CompilerParams(
            dimension_semantics=("arbitrary",), vmem_limit_bytes=VMEM_LIMIT_BYTES),
        name="combine",
    )(pos, pos, x, meta, mod, final_g.reshape(1, d), y_sorted)


def _moe_layer(x, mod, norm_g, final_g, w_router, b_router, w_gate_up, w_down,
               *, tr=512, td=512, tc=256):
    b, s, d = x.shape
    n = b * s
    ne = w_router.shape[1]
    h, meta, cnt = _router(x, mod, norm_g, w_router, b_router)
    meta = meta.reshape(n, LANES)
    e1 = meta[:, META_E1].astype(jnp.int32)
    e2 = meta[:, META_E2].astype(jnp.int32)
    r1 = meta[:, META_R1].astype(jnp.int32)
    r2 = meta[:, META_R2].astype(jnp.int32)
    counts = cnt[0, :ne].astype(jnp.int32)
    tiles_per = (counts + tr - 1) // tr
    tile_end = jnp.cumsum(tiles_per)
    n_used = tile_end[-1]
    group_start = (tile_end - tiles_per) * tr
    pos1 = group_start[e1] + r1
    pos2 = group_start[e2] + r2
    nt = 2 * n // tr + ne
    tile_ids = jnp.arange(nt, dtype=jnp.int32)
    te = jnp.sum((tile_ids[:, None] >= tile_end[None, :]).astype(jnp.int32), axis=1)
    te = jnp.minimum(te, ne - 1)
    te = jnp.where(tile_ids < n_used, te, te[n_used - 1]).astype(jnp.int32)
    zero_tiles = jnp.concatenate([tile_end - 1, n_used + jnp.arange(ne, dtype=jnp.int32)])
    zero_valid = jnp.concatenate([(counts % tr) != 0, n_used + jnp.arange(ne) < nt])
    zero_info = jnp.concatenate([zero_valid.astype(jnp.int32),
                                 jnp.clip(zero_tiles, 0, nt - 1).astype(jnp.int32)])

    def blocked(t):
        return jnp.concatenate([pos1.reshape(n // t, 1, t), pos2.reshape(n // t, 1, t)], axis=-1)

    hs = _dispatch(h.reshape(n, d), blocked(td), zero_info, n_rows=nt * tr, td=td, tr=tr)
    y_sorted = _experts(hs, te, n_used.reshape(1).astype(jnp.int32), w_gate_up, w_down, tr=tr)
    out = _combine(x.reshape(n, d), meta, mod, final_g, y_sorted, blocked(tc), seq=s, tc=tc)
    return out.reshape(b, s, d)


def kernel(x, c, positions, ada_w, ada_b, mix_norm_g, ffn_norm_g, sgu_w_in, sgu_ln_g, sgu_ln_b,
           sgu_w_s, sgu_b_s, sgu_w_out, ffn_w_gate_up, ffn_w_down, ret_w_in, ret_w_out,
           moe_w_router, moe_b_router, moe_w_gate_up, moe_w_down, final_norm_g):
    b, s, d = x.shape
    depth = ada_w.shape[0]
    assert depth == 2, "layer 0 = SGU + SwiGLU, layer 1 = retention + MoE"
    mod = _adaln(c, ada_w, ada_b).reshape(depth, b, 6, d)
    x = _sgu_layer(x, mod[0], mix_norm_g[0], sgu_w_in[0], sgu_ln_g[0], sgu_ln_b[0],
                   sgu_w_s[0], sgu_b_s[0], sgu_w_out[0])
    x = _ffn_layer(x, mod[0], ffn_norm_g[0], ffn_w_gate_up[0], ffn_w_down[0])
    x = _retention_layer(x, positions, mod[1], mix_norm_g[1], ret_w_in[0], ret_w_out[0])
    return _moe_layer(x, mod[1], ffn_norm_g[1], final_norm_g, moe_w_router[0], moe_b_router[0],
                      moe_w_gate_up[0], moe_w_down[0])
```

```python
import functools

import jax
import jax.numpy as jnp
import numpy as np
from jax import lax
from jax.experimental import pallas as pl
from jax.experimental.pallas import tpu as pltpu

F32 = jnp.float32
BF16 = jnp.bfloat16

CHUNK = 128
SGU_GROUPS = 8
RET_HEADS = 4
N_EXPERTS = 8
ROPE_BASE = 10000.0
EPS = 1e-6
LANES = 128
VMEM_LIMIT_BYTES = 56 * 1024 * 1024


def _const_spec(shape):
    nd = len(shape)
    return pl.BlockSpec(shape, lambda *_: (0,) * nd, pipeline_mode=pl.Buffered(1))


def _modnorm(x, g, scale, shift):
    ms = jnp.mean(x * x, axis=-1, keepdims=True)
    return (x * lax.rsqrt(ms + EPS) * g) * (1.0 + scale) + shift


def _adaln_kernel(c_ref, w_ref, b_ref, o_ref):
    sc = jax.nn.silu(c_ref[...])
    o_ref[0] = jnp.dot(sc, w_ref[0], precision=lax.Precision.HIGHEST,
                       preferred_element_type=F32) + b_ref[0]


def _adaln(c, ada_w, ada_b):
    depth, d, six_d = ada_w.shape
    b = c.shape[0]
    tn = 1024
    return pl.pallas_call(
        _adaln_kernel,
        out_shape=jax.ShapeDtypeStruct((depth, b, six_d), F32),
        grid=(depth, six_d // tn),
        in_specs=[
            pl.BlockSpec((b, d), lambda l, j: (0, 0)),
            pl.BlockSpec((1, d, tn), lambda l, j: (l, 0, j)),
            pl.BlockSpec((1, 1, tn), lambda l, j: (l, 0, j)),
        ],
        out_specs=pl.BlockSpec((1, b, tn), lambda l, j: (l, 0, j)),
        compiler_params=pltpu.CompilerParams(
            dimension_semantics=("parallel", "parallel"),
            vmem_limit_bytes=VMEM_LIMIT_BYTES),
        name="adaln",
    )(c, ada_w, ada_b.reshape(depth, 1, six_d))


def _sgu_kernel(x_ref, mod_ref, ng_ref, win_ref, lng_ref, lnb_ref, ws_ref, bs_ref,
                wout_ref, o_ref, u_ref, vn_ref, y_ref, *, tm, width, col_block):
    x = x_ref[0]
    mod = mod_ref[0]
    h = _modnorm(x, ng_ref[...], mod[1:2], mod[0:1]).astype(BF16)
    nb = width // col_block
    vs = []
    for j in range(2 * nb):
        z = jnp.dot(h, win_ref[:, j * col_block:(j + 1) * col_block],
                    preferred_element_type=F32)
        z = jax.nn.gelu(z)
        if j < nb:
            u_ref[:, j * col_block:(j + 1) * col_block] = z
        else:
            vs.append(z)
    s1 = sum(jnp.sum(v, axis=-1, keepdims=True) for v in vs)
    mu = s1 * (1.0 / width)
    s2 = sum(jnp.sum((v - mu) * (v - mu), axis=-1, keepdims=True) for v in vs)
    rstd = lax.rsqrt(s2 * (1.0 / width) + EPS)
    for j, v in enumerate(vs):
        sl = slice(j * col_block, (j + 1) * col_block)
        vn_ref[:, sl] = ((v - mu) * rstd * lng_ref[:, sl] + lnb_ref[:, sl]).astype(BF16)
    gd = width // SGU_GROUPS
    row = lax.broadcasted_iota(jnp.int32, (CHUNK, CHUNK), 0)
    col = lax.broadcasted_iota(jnp.int32, (CHUNK, CHUNK), 1)
    causal = row >= col
    for g in range(SGU_GROUPS):
        w = jnp.where(causal, ws_ref[g], jnp.zeros((), BF16))
        cs = slice(g * gd, (g + 1) * gd)
        for c in range(tm // CHUNK):
            rs = slice(c * CHUNK, (c + 1) * CHUNK)
            fv = jnp.dot(w, vn_ref[rs, cs], preferred_element_type=F32) + bs_ref[g]
            y_ref[rs, cs] = (u_ref[rs, cs] * fv).astype(BF16)
    out = jnp.dot(y_ref[...], wout_ref[...], preferred_element_type=F32)
    o_ref[0] = x + mod[2:3] * out


def _sgu_layer(x, mod, norm_g, w_in, ln_g, ln_b, w_s, b_s, w_out, *, tm=512):
    b, s, d = x.shape
    width = w_out.shape[0]
    gd = width // SGU_GROUPS
    bs_full = jnp.broadcast_to(b_s[:, :, None], (SGU_GROUPS, CHUNK, gd))
    kern = functools.partial(_sgu_kernel, tm=tm, width=width, col_block=512)
    return pl.pallas_call(
        kern,
        out_shape=jax.ShapeDtypeStruct((b, s, d), F32),
        grid=(b, s // tm),
        in_specs=[
            pl.BlockSpec((1, tm, d), lambda i, j: (i, j, 0)),
            pl.BlockSpec((1, 6, d), lambda i, j: (i, 0, 0)),
            _const_spec((1, d)),
            _const_spec((d, 2 * width)),
            _const_spec((1, width)),
            _const_spec((1, width)),
            _const_spec((SGU_GROUPS, CHUNK, CHUNK)),
            _const_spec((SGU_GROUPS, CHUNK, gd)),
            _const_spec((width, d)),
        ],
        out_specs=pl.BlockSpec((1, tm, d), lambda i, j: (i, j, 0)),
        scratch_shapes=[
            pltpu.VMEM((tm, width), F32),
            pltpu.VMEM((tm, width), BF16),
            pltpu.VMEM((tm, width), BF16),
        ],
        compiler_params=pltpu.CompilerParams(
            dimension_semantics=("parallel", "parallel"),
            vmem_limit_bytes=VMEM_LIMIT_BYTES),
        name="sgu",
    )(x, mod, norm_g.reshape(1, d), w_in.astype(BF16), ln_g.reshape(1, width),
      ln_b.reshape(1, width), w_s.astype(BF16), bs_full, w_out.astype(BF16))


def _ffn_kernel(x_ref, mod_ref, ng_ref, wgu_ref, wd_ref, o_ref, *, ffn, col_block):
    x = x_ref[0]
    mod = mod_ref[0]
    h = _modnorm(x, ng_ref[...], mod[4:5], mod[3:4]).astype(BF16)
    acc = jnp.zeros(x.shape, F32)
    for j in range(ffn // col_block):
        g = jnp.dot(h, wgu_ref[:, j * col_block:(j + 1) * col_block],
                    preferred_element_type=F32)
        u = jnp.dot(h, wgu_ref[:, ffn + j * col_block:ffn + (j + 1) * col_block],
                    preferred_element_type=F32)
        a = (jax.nn.silu(g) * u).astype(BF16)
        acc = acc + jnp.dot(a, wd_ref[j * col_block:(j + 1) * col_block, :],
                            preferred_element_type=F32)
    o_ref[0] = x + mod[5:6] * acc


def _ffn_layer(x, mod, norm_g, w_gate_up, w_down, *, tm=512):
    b, s, d = x.shape
    ffn = w_down.shape[0]
    kern = functools.partial(_ffn_kernel, ffn=ffn, col_block=256)
    return pl.pallas_call(
        kern,
        out_shape=jax.ShapeDtypeStruct((b, s, d), F32),
        grid=(b, s // tm),
        in_specs=[
            pl.BlockSpec((1, tm, d), lambda i, j: (i, j, 0)),
            pl.BlockSpec((1, 6, d), lambda i, j: (i, 0, 0)),
            _const_spec((1, d)),
            _const_spec((d, 2 * ffn)),
            _const_spec((ffn, d)),
        ],
        out_specs=pl.BlockSpec((1, tm, d), lambda i, j: (i, j, 0)),
        compiler_params=pltpu.CompilerParams(
            dimension_semantics=("parallel", "parallel"),
            vmem_limit_bytes=VMEM_LIMIT_BYTES),
        name="ffn",
    )(x, mod, norm_g.reshape(1, d), w_gate_up.astype(BF16), w_down.astype(BF16))


def _retention_kernel(x_ref, pos_ref, mod_ref, ng_ref, invf_ref, win_ref, wout_ref,
                      intra_ref, xi_ref, zeta_ref, o_ref,
                      state_ref, q_ref, qx_ref, k_ref, kz_ref, v_ref, sg_ref, ob_ref,
                      *, tm, dk, dv, decay):
    heads = RET_HEADS
    qk = heads * dk

    @pl.when(pl.program_id(1) == 0)
    def _():
        state_ref[...] = jnp.zeros(state_ref.shape, F32)

    x = x_ref[0]
    mod = mod_ref[0]
    h = _modnorm(x, ng_ref[...], mod[1:2], mod[0:1]).astype(BF16)
    ang = pos_ref[0].astype(F32) * invf_ref[...]
    cos = jnp.cos(ang)
    sin = jnp.sin(ang)
    half = dk // 2
    for hd in range(heads):
        cs = slice(hd * dk, (hd + 1) * dk)
        p = jnp.dot(h, win_ref[:, cs], preferred_element_type=F32)
        t1, t2 = p[:, :half], p[:, half:]
        q = jnp.concatenate([t1 * cos - t2 * sin, t2 * cos + t1 * sin], axis=-1)
        q_ref[:, cs] = q.astype(BF16)
        qx_ref[:, cs] = (q * xi_ref[:, cs]).astype(BF16)
        p = jnp.dot(h, win_ref[:, qk + hd * dk:qk + (hd + 1) * dk],
                    preferred_element_type=F32)
        t1, t2 = p[:, :half], p[:, half:]
        k = jnp.concatenate([t1 * cos - t2 * sin, t2 * cos + t1 * sin], axis=-1)
        k = k * (dk ** -0.5)
        k_ref[:, cs] = k.astype(BF16)
        kz_ref[:, cs] = (k * zeta_ref[:, cs]).astype(BF16)
    for hd in range(heads):
        cs = slice(hd * dv, (hd + 1) * dv)
        v_ref[:, cs] = jnp.dot(h, win_ref[:, 2 * qk + hd * dv:2 * qk + (hd + 1) * dv],
                               preferred_element_type=F32).astype(BF16)
        g = jnp.dot(h, win_ref[:, 2 * qk + heads * dv + hd * dv:
                               2 * qk + heads * dv + (hd + 1) * dv],
                    preferred_element_type=F32)
        sg_ref[:, cs] = jax.nn.silu(g)

    def chunk_body(c, carry):
        rs = pl.ds(pl.multiple_of(c * CHUNK, CHUNK), CHUNK)
        for hd in range(heads):
            ks = slice(hd * dk, (hd + 1) * dk)
            vs = slice(hd * dv, (hd + 1) * dv)
            qh = q_ref[rs, ks]
            kh = k_ref[rs, ks]
            vh = v_ref[rs, vs]
            scores = lax.dot_general(qh, kh, (((1,), (1,)), ((), ())),
                                     preferred_element_type=F32)
            scores = (scores * intra_ref[hd]).astype(BF16)
            st = state_ref[hd]
            o = (jnp.dot(scores, vh, preferred_element_type=F32)
                 + jnp.dot(qx_ref[rs, ks], st.astype(BF16), preferred_element_type=F32))
            state_ref[hd] = st * decay[hd] + lax.dot_general(
                kz_ref[rs, ks], vh, (((0,), (0,)), ((), ())), preferred_element_type=F32)
            on = o * lax.rsqrt(jnp.mean(o * o, axis=-1, keepdims=True) + EPS)
            ob_ref[rs, vs] = (on * sg_ref[rs, vs]).astype(BF16)
        return carry

    lax.fori_loop(0, tm // CHUNK, chunk_body, 0)
    out = jnp.dot(ob_ref[...], wout_ref[...], preferred_element_type=F32)
    o_ref[0] = x + mod[2:3] * out


def _retention_layer(x, positions, mod, norm_g, w_in, w_out, *, tm=512):
    b, s, d = x.shape
    heads = RET_HEADS
    dk = d // heads
    dv = w_out.shape[0] // heads
    qk = heads * dk
    vw = heads * dv
    log_gamma = jnp.log1p(-jnp.exp2(-5.0 - jnp.arange(heads, dtype=F32)))
    idx = jnp.arange(CHUNK, dtype=F32)
    rel = idx[:, None] - idx[None, :]
    intra = jnp.where(rel >= 0, jnp.exp(log_gamma[:, None, None] * jnp.maximum(rel, 0.0)), 0.0)
    xi = jnp.exp(log_gamma[:, None] * (idx + 1.0))
    zeta = jnp.exp(log_gamma[:, None] * (CHUNK - 1.0 - idx))
    xi_t = jnp.tile(jnp.repeat(xi.T, dk, axis=1), (tm // CHUNK, 1))
    zeta_t = jnp.tile(jnp.repeat(zeta.T, dk, axis=1), (tm // CHUNK, 1))
    decay = tuple(float(np.exp(np.log1p(-2.0 ** (-5.0 - hd)) * CHUNK)) for hd in range(heads))
    inv_freq = 1.0 / (ROPE_BASE ** (jnp.arange(0, dk, 2, dtype=F32) / dk))
    kern = functools.partial(_retention_kernel, tm=tm, dk=dk, dv=dv, decay=decay)
    return pl.pallas_call(
        kern,
        out_shape=jax.ShapeDtypeStruct((b, s, d), F32),
        grid=(b, s // tm),
        in_specs=[
            pl.BlockSpec((1, tm, d), lambda i, j: (i, j, 0)),
            pl.BlockSpec((1, tm, 1), lambda i, j: (i, j, 0)),
            pl.BlockSpec((1, 6, d), lambda i, j: (i, 0, 0)),
            _const_spec((1, d)),
            _const_spec((1, dk // 2)),
            _const_spec((d, 2 * qk + 2 * vw)),
            _const_spec((vw, d)),
            _const_spec((heads, CHUNK, CHUNK)),
            _const_spec((tm, qk)),
            _const_spec((tm, qk)),
        ],
        out_specs=pl.BlockSpec((1, tm, d), lambda i, j: (i, j, 0)),
        scratch_shapes=[
            pltpu.VMEM((heads, dk, dv), F32),
            pltpu.VMEM((tm, qk), BF16),
            pltpu.VMEM((tm, qk), BF16),
            pltpu.VMEM((tm, qk), BF16),
            pltpu.VMEM((tm, qk), BF16),
            pltpu.VMEM((tm, vw), BF16),
            pltpu.VMEM((tm, vw), F32),
            pltpu.VMEM((tm, vw), BF16),
        ],
        compiler_params=pltpu.CompilerParams(
            dimension_semantics=("parallel", "arbitrary"),
            vmem_limit_bytes=VMEM_LIMIT_BYTES),
        name="retention",
    )(x, positions.reshape(b, s, 1), mod, norm_g.reshape(1, d), inv_freq.reshape(1, dk // 2),
      w_in.astype(BF16), w_out.astype(BF16), intra, xi_t, zeta_t)


META_E1, META_E2, META_R1, META_R2, META_W1, META_W2 = range(6)


def _router_kernel(x_ref, mod_ref, ng_ref, wr_ref, br_ref, meta_ref, cnt_ref, carry_ref):
    @pl.when((pl.program_id(0) == 0) & (pl.program_id(1) == 0))
    def _():
        carry_ref[...] = jnp.zeros(carry_ref.shape, F32)

    x = x_ref[0]
    mod = mod_ref[0]
    h = _modnorm(x, ng_ref[...], mod[4:5], mod[3:4])
    logits = jnp.dot(h, wr_ref[...], precision=lax.Precision.HIGHEST,
                     preferred_element_type=F32) + br_ref[...]
    tm = logits.shape[0]
    lane = lax.broadcasted_iota(jnp.int32, logits.shape, 1)
    neg = jnp.float32(-jnp.inf)
    logits = jnp.where(lane < N_EXPERTS, logits, neg)
    m1 = jnp.max(logits, axis=-1, keepdims=True)
    i1 = jnp.min(jnp.where(logits == m1, lane, LANES), axis=-1, keepdims=True)
    rest = jnp.where(lane == i1, neg, logits)
    m2 = jnp.max(rest, axis=-1, keepdims=True)
    i2 = jnp.min(jnp.where(rest == m2, lane, LANES), axis=-1, keepdims=True)
    e2 = jnp.exp(m2 - m1)
    w1 = 1.0 / (1.0 + e2)
    w2 = e2 / (1.0 + e2)
    sel = jnp.where((lane == i1) | (lane == i2), 1.0, 0.0)
    row = lax.broadcasted_iota(jnp.int32, (tm, tm), 0)
    col = lax.broadcasted_iota(jnp.int32, (tm, tm), 1)
    tri = jnp.where(row > col, 1.0, 0.0).astype(BF16)
    before = jnp.dot(tri, sel.astype(BF16), preferred_element_type=F32) + carry_ref[...]
    r1 = jnp.sum(jnp.where(lane == i1, before, 0.0), axis=-1, keepdims=True)
    r2 = jnp.sum(jnp.where(lane == i2, before, 0.0), axis=-1, keepdims=True)
    carry_ref[...] += jnp.sum(sel, axis=0, keepdims=True)
    cnt_ref[...] = carry_ref[...]
    meta = jnp.zeros(logits.shape, F32)
    for k, val in ((META_E1, i1.astype(F32)), (META_E2, i2.astype(F32)), (META_R1, r1),
                   (META_R2, r2), (META_W1, w1), (META_W2, w2)):
        meta = jnp.where(lane == k, val, meta)
    meta_ref[0] = meta


def _router(x, mod, norm_g, w_router, b_router, *, tm=512):
    b, s, d = x.shape
    ne = w_router.shape[1]
    wr = jnp.zeros((d, LANES), F32).at[:, :ne].set(w_router)
    br = jnp.zeros((1, LANES), F32).at[0, :ne].set(b_router)
    return pl.pallas_call(
        _router_kernel,
        out_shape=(jax.ShapeDtypeStruct((b, s, LANES), F32),
                   jax.ShapeDtypeStruct((1, LANES), F32)),
        grid=(b, s // tm),
        in_specs=[
            pl.BlockSpec((1, tm, d), lambda i, j: (i, j, 0)),
            pl.BlockSpec((1, 6, d), lambda i, j: (i, 0, 0)),
            _const_spec((1, d)),
            _const_spec((d, LANES)),
            _const_spec((1, LANES)),
        ],
        out_specs=(pl.BlockSpec((1, tm, LANES), lambda i, j: (i, j, 0)),
                   pl.BlockSpec((1, LANES), lambda i, j: (0, 0))),
        scratch_shapes=[pltpu.VMEM((1, LANES), F32)],
        compiler_params=pltpu.CompilerParams(
            dimension_semantics=("arbitrary", "arbitrary"),
            vmem_limit_bytes=VMEM_LIMIT_BYTES),
        name="router",
    )(x, mod, norm_g.reshape(1, d), wr, br)


N_ZERO_SLOTS = 2 * N_EXPERTS


def _dispatch_kernel(info_ref, pos_ref, x_ref, mod_ref, ng_ref, hs_hbm, hbuf, zeros_ref,
                     sem, zsem, *, td, tr):
    i = pl.program_id(0)
    slot = lax.rem(i, 2)

    def zero_copy(k):
        start = pl.multiple_of(info_ref[N_ZERO_SLOTS + k] * tr, tr)
        return pltpu.make_async_copy(zeros_ref, hs_hbm.at[pl.ds(start, tr)], zsem)

    @pl.when(i == 0)
    def _():
        zeros_ref[...] = jnp.zeros(zeros_ref.shape, F32)
        for k in range(N_ZERO_SLOTS):
            @pl.when(info_ref[k] == 1)
            def _():
                zero_copy(k).start()
        for k in range(N_ZERO_SLOTS):
            @pl.when(info_ref[k] == 1)
            def _():
                zero_copy(k).wait()

    mod = mod_ref[0]
    hbuf[slot] = _modnorm(x_ref[...], ng_ref[...], mod[4:5], mod[3:4])

    def body(t, carry):
        src = hbuf.at[slot, pl.ds(t, 1)]
        pltpu.make_async_copy(src, hs_hbm.at[pl.ds(pos_ref[0, 0, t], 1)], sem.at[slot]).start()
        pltpu.make_async_copy(src, hs_hbm.at[pl.ds(pos_ref[0, 0, td + t], 1)],
                              sem.at[slot]).start()
        return carry

    lax.fori_loop(0, td, body, 0, unroll=8)

    def wait_rows(s):
        pltpu.make_async_copy(hs_hbm.at[pl.ds(0, 2 * td)], hs_hbm.at[pl.ds(0, 2 * td)],
                              sem.at[s]).wait()

    @pl.when(i > 0)
    def _():
        wait_rows(1 - slot)

    @pl.when(i == pl.num_programs(0) - 1)
    def _():
        wait_rows(slot)


def _dispatch(x, mod, norm_g, pos, zero_info, *, n_rows, seq, td, tr):
    n, d = x.shape
    kern = functools.partial(_dispatch_kernel, td=td, tr=tr)
    return pl.pallas_call(
        kern,
        out_shape=jax.ShapeDtypeStruct((n_rows, d), F32),
        grid_spec=pltpu.PrefetchScalarGridSpec(
            num_scalar_prefetch=1,
            grid=(n // td,),
            in_specs=[
                pl.BlockSpec((1, 1, 2 * td), lambda i, info: (i, 0, 0),
                             memory_space=pltpu.SMEM),
                pl.BlockSpec((td, d), lambda i, info: (i, 0)),
                pl.BlockSpec((1, 6, d), lambda i, info: ((i * td) // seq, 0, 0)),
                pl.BlockSpec((1, d), lambda i, info: (0, 0)),
            ],
            out_specs=pl.BlockSpec(memory_space=pl.ANY),
            scratch_shapes=[pltpu.VMEM((2, td, d), F32), pltpu.VMEM((tr, d), F32),
                            pltpu.SemaphoreType.DMA((2,)), pltpu.SemaphoreType.DMA],
        ),
        compiler_params=pltpu.CompilerParams(
            dimension_semantics=("arbitrary",), vmem_limit_bytes=VMEM_LIMIT_BYTES),
        name="dispatch",
    )(zero_info, pos, x, mod, norm_g.reshape(1, d))


def _experts_kernel(te_ref, nu_ref, hs_ref, wgu_ref, wd_ref, y_ref, *, ffn, col_block):
    i = pl.program_id(0)

    @pl.when(i < nu_ref[0])
    def _():
        h = hs_ref[...].astype(BF16)
        y = jnp.zeros(y_ref.shape, F32)
        for j in range(ffn // col_block):
            g = jnp.dot(h, wgu_ref[0, :, j * col_block:(j + 1) * col_block],
                        preferred_element_type=F32)
            u = jnp.dot(h, wgu_ref[0, :, ffn + j * col_block:ffn + (j + 1) * col_block],
                        preferred_element_type=F32)
            a = (jax.nn.silu(g) * u).astype(BF16)
            y = y + jnp.dot(a, wd_ref[0, j * col_block:(j + 1) * col_block, :],
                            preferred_element_type=F32)
        y_ref[...] = y

    @pl.when(i >= nu_ref[0])
    def _():
        y_ref[...] = jnp.zeros(y_ref.shape, F32)


def _experts(hs, tile_expert, n_used, w_gate_up, w_down, *, tr):
    n_rows, d = hs.shape
    ne, ffn, _ = w_down.shape
    kern = functools.partial(_experts_kernel, ffn=ffn, col_block=256)
    return pl.pallas_call(
        kern,
        out_shape=jax.ShapeDtypeStruct((n_rows, d), F32),
        grid_spec=pltpu.PrefetchScalarGridSpec(
            num_scalar_prefetch=2,
            grid=(n_rows // tr,),
            in_specs=[
                pl.BlockSpec((tr, d), lambda i, te, nu: (jnp.minimum(i, nu[0] - 1), 0)),
                pl.BlockSpec((1, d, 2 * ffn), lambda i, te, nu: (te[i], 0, 0)),
                pl.BlockSpec((1, ffn, d), lambda i, te, nu: (te[i], 0, 0)),
            ],
            out_specs=pl.BlockSpec((tr, d), lambda i, te, nu: (i, 0)),
        ),
        compiler_params=pltpu.CompilerParams(
            dimension_semantics=("arbitrary",), vmem_limit_bytes=VMEM_LIMIT_BYTES),
        name="experts",
    )(tile_expert, n_used, hs, w_gate_up.astype(BF16), w_down.astype(BF16))


def _combine_kernel(posc_ref, posn_ref, x_ref, meta_ref, mod_ref, fg_ref, y_hbm, o_ref,
                    ybuf, sem, *, tc):
    i = pl.program_id(0)
    slot = lax.rem(i, 2)

    def issue(pos_ref, to_slot):
        def body(t, carry):
            pltpu.make_async_copy(y_hbm.at[pl.ds(pos_ref[0, 0, t], 1)],
                                  ybuf.at[to_slot, pl.ds(t, 1)], sem.at[to_slot]).start()
            return carry
        lax.fori_loop(0, 2 * tc, body, 0, unroll=8)

    @pl.when(i == 0)
    def _():
        issue(posc_ref, 0)

    @pl.when(i + 1 < pl.num_programs(0))
    def _():
        issue(posn_ref, 1 - slot)

    pltpu.make_async_copy(y_hbm.at[pl.ds(0, 2 * tc)], ybuf.at[slot], sem.at[slot]).wait()
    meta = meta_ref[...]
    lane = lax.broadcasted_iota(jnp.int32, meta.shape, 1)
    w1 = jnp.sum(jnp.where(lane == META_W1, meta, 0.0), axis=-1, keepdims=True)
    w2 = jnp.sum(jnp.where(lane == META_W2, meta, 0.0), axis=-1, keepdims=True)
    y = w1 * ybuf[slot, pl.ds(0, tc), :] + w2 * ybuf[slot, pl.ds(tc, tc), :]
    xo = x_ref[...] + mod_ref[0][5:6] * y
    ms = jnp.mean(xo * xo, axis=-1, keepdims=True)
    o_ref[...] = xo * lax.rsqrt(ms + EPS) * fg_ref[...]


def _combine(x, meta, mod, final_g, y_sorted, pos, *, seq, tc):
    n, d = x.shape
    nblk = n // tc
    kern = functools.partial(_combine_kernel, tc=tc)
    smem_pos = lambda f: pl.BlockSpec((1, 1, 2 * tc), f, memory_space=pltpu.SMEM)
    return pl.pallas_call(
        kern,
        out_shape=jax.ShapeDtypeStruct((n, d), F32),
        grid=(nblk,),
        in_specs=[
            smem_pos(lambda i: (i, 0, 0)),
            smem_pos(lambda i: (jnp.minimum(i + 1, nblk - 1), 0, 0)),
            pl.BlockSpec((tc, d), lambda i: (i, 0)),
            pl.BlockSpec((tc, LANES), lambda i: (i, 0)),
            pl.BlockSpec((1, 6, d), lambda i: ((i * tc) // seq, 0, 0)),
            _const_spec((1, d)),
            pl.BlockSpec(memory_space=pl.ANY),
        ],
        out_specs=pl.BlockSpec((tc, d), lambda i: (i, 0)),
        scratch_shapes=[pltpu.VMEM((2, 2 * tc, d), F32), pltpu.SemaphoreType.DMA((2,))],
        compiler_params=pltpu.CompilerParams(
            dimension_semantics=("arbitrary",), vmem_limit_bytes=VMEM_LIMIT_BYTES),
        name="combine",
    )(pos, pos, x, meta, mod, final_g.reshape(1, d), y_sorted)


def _moe_layer(x, mod, norm_g, final_g, w_router, b_router, w_gate_up, w_down,
               *, tr=512, td=512, tc=256):
    b, s, d = x.shape
    n = b * s
    ne = w_router.shape[1]
    meta, cnt = _router(x, mod, norm_g, w_router, b_router)
    meta = meta.reshape(n, LANES)
    e1 = meta[:, META_E1].astype(jnp.int32)
    e2 = meta[:, META_E2].astype(jnp.int32)
    r1 = meta[:, META_R1].astype(jnp.int32)
    r2 = meta[:, META_R2].astype(jnp.int32)
    counts = cnt[0, :ne].astype(jnp.int32)
    tiles_per = (counts + tr - 1) // tr
    tile_end = jnp.cumsum(tiles_per)
    n_used = tile_end[-1]
    group_start = (tile_end - tiles_per) * tr
    pos1 = group_start[e1] + r1
    pos2 = group_start[e2] + r2
    nt = 2 * n // tr + ne
    tile_ids = jnp.arange(nt, dtype=jnp.int32)
    te = jnp.sum((tile_ids[:, None] >= tile_end[None, :]).astype(jnp.int32), axis=1)
    te = jnp.minimum(te, ne - 1)
    te = jnp.where(tile_ids < n_used, te, te[n_used - 1]).astype(jnp.int32)
    zero_tiles = jnp.concatenate([tile_end - 1, n_used + jnp.arange(ne, dtype=jnp.int32)])
    zero_valid = jnp.concatenate([(counts % tr) != 0, n_used + jnp.arange(ne) < nt])
    zero_info = jnp.concatenate([zero_valid.astype(jnp.int32),
                                 jnp.clip(zero_tiles, 0, nt - 1).astype(jnp.int32)])

    def blocked(t):
        return jnp.concatenate([pos1.reshape(n // t, 1, t), pos2.reshape(n // t, 1, t)], axis=-1)

    hs = _dispatch(x.reshape(n, d), mod, norm_g, blocked(td), zero_info, n_rows=nt * tr,
                   seq=s, td=td, tr=tr)
    y_sorted = _experts(hs, te, n_used.reshape(1).astype(jnp.int32), w_gate_up, w_down, tr=tr)
    out = _combine(x.reshape(n, d), meta, mod, final_g, y_sorted, blocked(tc), seq=s, tc=tc)
    return out.reshape(b, s, d)


def kernel(x, c, positions, ada_w, ada_b, mix_norm_g, ffn_norm_g, sgu_w_in, sgu_ln_g, sgu_ln_b,
           sgu_w_s, sgu_b_s, sgu_w_out, ffn_w_gate_up, ffn_w_down, ret_w_in, ret_w_out,
           moe_w_router, moe_b_router, moe_w_gate_up, moe_w_down, final_norm_g):
    b, s, d = x.shape
    depth = ada_w.shape[0]
    assert depth == 2, "layer 0 = SGU + SwiGLU, layer 1 = retention + MoE"
    mod = _adaln(c, ada_w, ada_b).reshape(depth, b, 6, d)
    x = _sgu_layer(x, mod[0], mix_norm_g[0], sgu_w_in[0], sgu_ln_g[0], sgu_ln_b[0],
                   sgu_w_s[0], sgu_b_s[0], sgu_w_out[0])
    x = _ffn_layer(x, mod[0], ffn_norm_g[0], ffn_w_gate_up[0], ffn_w_down[0])
    x = _retention_layer(x, positions, mod[1], mix_norm_g[1], ret_w_in[0], ret_w_out[0])
    return _moe_layer(x, mod[1], ffn_norm_g[1], final_norm_g, moe_w_router[0], moe_b_router[0],
                      moe_w_gate_up[0], moe_w_down[0])
```

```python
import functools

import jax
import jax.numpy as jnp
import numpy as np
from jax import lax
from jax.experimental import pallas as pl
from jax.experimental.pallas import tpu as pltpu

F32 = jnp.float32
BF16 = jnp.bfloat16

CHUNK = 128
SGU_GROUPS = 8
RET_HEADS = 4
N_EXPERTS = 8
ROPE_BASE = 10000.0
EPS = 1e-6
LANES = 128
VMEM_LIMIT_BYTES = 56 * 1024 * 1024


def _const_spec(shape):
    nd = len(shape)
    return pl.BlockSpec(shape, lambda *_: (0,) * nd, pipeline_mode=pl.Buffered(1))


def _modnorm(x, g, scale, shift):
    ms = jnp.mean(x * x, axis=-1, keepdims=True)
    return (x * lax.rsqrt(ms + EPS) * g) * (1.0 + scale) + shift


def _adaln_kernel(c_ref, w_ref, b_ref, o_ref):
    sc = jax.nn.silu(c_ref[...])
    o_ref[0] = jnp.dot(sc, w_ref[0], precision=lax.Precision.HIGHEST,
                       preferred_element_type=F32) + b_ref[0]


def _adaln(c, ada_w, ada_b):
    depth, d, six_d = ada_w.shape
    b = c.shape[0]
    tn = 1024
    return pl.pallas_call(
        _adaln_kernel,
        out_shape=jax.ShapeDtypeStruct((depth, b, six_d), F32),
        grid=(depth, six_d // tn),
        in_specs=[
            pl.BlockSpec((b, d), lambda l, j: (0, 0)),
            pl.BlockSpec((1, d, tn), lambda l, j: (l, 0, j)),
            pl.BlockSpec((1, 1, tn), lambda l, j: (l, 0, j)),
        ],
        out_specs=pl.BlockSpec((1, b, tn), lambda l, j: (l, 0, j)),
        compiler_params=pltpu.CompilerParams(
            dimension_semantics=("parallel", "parallel"),
            vmem_limit_bytes=VMEM_LIMIT_BYTES),
        name="adaln",
    )(c, ada_w, ada_b.reshape(depth, 1, six_d))


def _sgu_kernel(x_ref, mod_ref, ng_ref, win_ref, lng_ref, lnb_ref, ws_ref, bs_ref,
                wout_ref, o_ref, u_ref, vn_ref, y_ref, *, tm, width, col_block):
    x = x_ref[0]
    mod = mod_ref[0]
    h = _modnorm(x, ng_ref[...], mod[1:2], mod[0:1]).astype(BF16)
    nb = width // col_block
    vs = []
    for j in range(2 * nb):
        z = jnp.dot(h, win_ref[:, j * col_block:(j + 1) * col_block],
                    preferred_element_type=F32)
        z = jax.nn.gelu(z)
        if j < nb:
            u_ref[:, j * col_block:(j + 1) * col_block] = z
        else:
            vs.append(z)
    s1 = sum(jnp.sum(v, axis=-1, keepdims=True) for v in vs)
    mu = s1 * (1.0 / width)
    s2 = sum(jnp.sum((v - mu) * (v - mu), axis=-1, keepdims=True) for v in vs)
    rstd = lax.rsqrt(s2 * (1.0 / width) + EPS)
    for j, v in enumerate(vs):
        sl = slice(j * col_block, (j + 1) * col_block)
        vn_ref[:, sl] = ((v - mu) * rstd * lng_ref[:, sl] + lnb_ref[:, sl]).astype(BF16)
    gd = width // SGU_GROUPS
    row = lax.broadcasted_iota(jnp.int32, (CHUNK, CHUNK), 0)
    col = lax.broadcasted_iota(jnp.int32, (CHUNK, CHUNK), 1)
    causal = row >= col
    for g in range(SGU_GROUPS):
        w = jnp.where(causal, ws_ref[g], jnp.zeros((), BF16))
        cs = slice(g * gd, (g + 1) * gd)
        for c in range(tm // CHUNK):
            rs = slice(c * CHUNK, (c + 1) * CHUNK)
            fv = jnp.dot(w, vn_ref[rs, cs], preferred_element_type=F32) + bs_ref[g]
            y_ref[rs, cs] = (u_ref[rs, cs] * fv).astype(BF16)
    out = jnp.dot(y_ref[...], wout_ref[...], preferred_element_type=F32)
    o_ref[0] = x + mod[2:3] * out


def _sgu_layer(x, mod, norm_g, w_in, ln_g, ln_b, w_s, b_s, w_out, *, tm=512):
    b, s, d = x.shape
    width = w_out.shape[0]
    gd = width // SGU_GROUPS
    bs_full = jnp.broadcast_to(b_s[:, :, None], (SGU_GROUPS, CHUNK, gd))
    kern = functools.partial(_sgu_kernel, tm=tm, width=width, col_block=512)
    return pl.pallas_call(
        kern,
        out_shape=jax.ShapeDtypeStruct((b, s, d), F32),
        grid=(b, s // tm),
        in_specs=[
            pl.BlockSpec((1, tm, d), lambda i, j: (i, j, 0)),
            pl.BlockSpec((1, 6, d), lambda i, j: (i, 0, 0)),
            _const_spec((1, d)),
            _const_spec((d, 2 * width)),
            _const_spec((1, width)),
            _const_spec((1, width)),
            _const_spec((SGU_GROUPS, CHUNK, CHUNK)),
            _const_spec((SGU_GROUPS, CHUNK, gd)),
            _const_spec((width, d)),
        ],
        out_specs=pl.BlockSpec((1, tm, d), lambda i, j: (i, j, 0)),
        scratch_shapes=[
            pltpu.VMEM((tm, width), F32),
            pltpu.VMEM((tm, width), BF16),
            pltpu.VMEM((tm, width), BF16),
        ],
        compiler_params=pltpu.CompilerParams(
            dimension_semantics=("parallel", "parallel"),
            vmem_limit_bytes=VMEM_LIMIT_BYTES),
        name="sgu",
    )(x, mod, norm_g.reshape(1, d), w_in.astype(BF16), ln_g.reshape(1, width),
      ln_b.reshape(1, width), w_s.astype(BF16), bs_full, w_out.astype(BF16))


def _ffn_kernel(x_ref, mod_ref, ng_ref, wgu_ref, wd_ref, o_ref, *, ffn, col_block):
    x = x_ref[0]
    mod = mod_ref[0]
    h = _modnorm(x, ng_ref[...], mod[4:5], mod[3:4]).astype(BF16)
    acc = jnp.zeros(x.shape, F32)
    for j in range(ffn // col_block):
        g = jnp.dot(h, wgu_ref[:, j * col_block:(j + 1) * col_block],
                    preferred_element_type=F32)
        u = jnp.dot(h, wgu_ref[:, ffn + j * col_block:ffn + (j + 1) * col_block],
                    preferred_element_type=F32)
        a = (jax.nn.silu(g) * u).astype(BF16)
        acc = acc + jnp.dot(a, wd_ref[j * col_block:(j + 1) * col_block, :],
                            preferred_element_type=F32)
    o_ref[0] = x + mod[5:6] * acc


def _ffn_layer(x, mod, norm_g, w_gate_up, w_down, *, tm=512):
    b, s, d = x.shape
    ffn = w_down.shape[0]
    kern = functools.partial(_ffn_kernel, ffn=ffn, col_block=256)
    return pl.pallas_call(
        kern,
        out_shape=jax.ShapeDtypeStruct((b, s, d), F32),
        grid=(b, s // tm),
        in_specs=[
            pl.BlockSpec((1, tm, d), lambda i, j: (i, j, 0)),
            pl.BlockSpec((1, 6, d), lambda i, j: (i, 0, 0)),
            _const_spec((1, d)),
            _const_spec((d, 2 * ffn)),
            _const_spec((ffn, d)),
        ],
        out_specs=pl.BlockSpec((1, tm, d), lambda i, j: (i, j, 0)),
        compiler_params=pltpu.CompilerParams(
            dimension_semantics=("parallel", "parallel"),
            vmem_limit_bytes=VMEM_LIMIT_BYTES),
        name="ffn",
    )(x, mod, norm_g.reshape(1, d), w_gate_up.astype(BF16), w_down.astype(BF16))


def _retention_kernel(x_ref, pos_ref, mod_ref, ng_ref, invf_ref, win_ref, wout_ref,
                      intra_ref, xi_ref, zeta_ref, o_ref,
                      state_ref, q_ref, qx_ref, k_ref, kz_ref, v_ref, sg_ref, ob_ref,
                      *, tm, rc, dk, dv, decay):
    heads = RET_HEADS
    qk = heads * dk

    @pl.when(pl.program_id(1) == 0)
    def _():
        state_ref[...] = jnp.zeros(state_ref.shape, F32)

    mod = mod_ref[0]
    half = dk // 2

    def rope(p, cos, sin):
        t1, t2 = p[:, :half], p[:, half:]
        return jnp.concatenate([t1 * cos - t2 * sin, t2 * cos + t1 * sin], axis=-1)

    for c in range(tm // rc):
        rs = slice(c * rc, (c + 1) * rc)
        x = x_ref[0, rs, :]
        h = _modnorm(x, ng_ref[...], mod[1:2], mod[0:1]).astype(BF16)
        ang = pos_ref[0, rs, :].astype(F32) * invf_ref[...]
        cos = jnp.cos(ang)
        sin = jnp.sin(ang)
        for hd in range(heads):
            cs = slice(hd * dk, (hd + 1) * dk)
            q = rope(jnp.dot(h, win_ref[:, cs], preferred_element_type=F32), cos, sin)
            q_ref[rs, cs] = q.astype(BF16)
            qx_ref[rs, cs] = (q * xi_ref[:, cs]).astype(BF16)
            k = rope(jnp.dot(h, win_ref[:, qk + hd * dk:qk + (hd + 1) * dk],
                             preferred_element_type=F32), cos, sin) * (dk ** -0.5)
            k_ref[rs, cs] = k.astype(BF16)
            kz_ref[rs, cs] = (k * zeta_ref[:, cs]).astype(BF16)
        for hd in range(heads):
            cs = slice(hd * dv, (hd + 1) * dv)
            v_ref[rs, cs] = jnp.dot(h, win_ref[:, 2 * qk + hd * dv:2 * qk + (hd + 1) * dv],
                                    preferred_element_type=F32).astype(BF16)
            g = jnp.dot(h, win_ref[:, 2 * qk + heads * dv + hd * dv:
                                   2 * qk + heads * dv + (hd + 1) * dv],
                        preferred_element_type=F32)
            sg_ref[rs, cs] = jax.nn.silu(g)
        for hd in range(heads):
            ks = slice(hd * dk, (hd + 1) * dk)
            vs = slice(hd * dv, (hd + 1) * dv)
            vh = v_ref[rs, vs]
            scores = lax.dot_general(q_ref[rs, ks], k_ref[rs, ks], (((1,), (1,)), ((), ())),
                                     preferred_element_type=F32)
            scores = (scores * intra_ref[hd]).astype(BF16)
            st = state_ref[hd]
            o = (jnp.dot(scores, vh, preferred_element_type=F32)
                 + jnp.dot(qx_ref[rs, ks], st.astype(BF16), preferred_element_type=F32))
            state_ref[hd] = st * decay[hd] + lax.dot_general(
                kz_ref[rs, ks], vh, (((0,), (0,)), ((), ())), preferred_element_type=F32)
            on = o * lax.rsqrt(jnp.mean(o * o, axis=-1, keepdims=True) + EPS)
            ob_ref[rs, vs] = (on * sg_ref[rs, vs]).astype(BF16)
        out = jnp.dot(ob_ref[rs, :], wout_ref[...], preferred_element_type=F32)
        o_ref[0, rs, :] = x + mod[2:3] * out


def _retention_layer(x, positions, mod, norm_g, w_in, w_out, *, tm=512, rc=256):
    b, s, d = x.shape
    heads = RET_HEADS
    dk = d // heads
    dv = w_out.shape[0] // heads
    qk = heads * dk
    vw = heads * dv
    log_gamma = jnp.log1p(-jnp.exp2(-5.0 - jnp.arange(heads, dtype=F32)))
    idx = jnp.arange(rc, dtype=F32)
    rel = idx[:, None] - idx[None, :]
    intra = jnp.where(rel >= 0, jnp.exp(log_gamma[:, None, None] * jnp.maximum(rel, 0.0)), 0.0)
    xi = jnp.exp(log_gamma[:, None] * (idx + 1.0))
    zeta = jnp.exp(log_gamma[:, None] * (rc - 1.0 - idx))
    xi_t = jnp.repeat(xi.T, dk, axis=1)
    zeta_t = jnp.repeat(zeta.T, dk, axis=1)
    decay = tuple(float(np.exp(np.log1p(-2.0 ** (-5.0 - hd)) * rc)) for hd in range(heads))
    inv_freq = 1.0 / (ROPE_BASE ** (jnp.arange(0, dk, 2, dtype=F32) / dk))
    kern = functools.partial(_retention_kernel, tm=tm, rc=rc, dk=dk, dv=dv, decay=decay)
    return pl.pallas_call(
        kern,
        out_shape=jax.ShapeDtypeStruct((b, s, d), F32),
        grid=(b, s // tm),
        in_specs=[
            pl.BlockSpec((1, tm, d), lambda i, j: (i, j, 0)),
            pl.BlockSpec((1, tm, 1), lambda i, j: (i, j, 0)),
            pl.BlockSpec((1, 6, d), lambda i, j: (i, 0, 0)),
            _const_spec((1, d)),
            _const_spec((1, dk // 2)),
            _const_spec((d, 2 * qk + 2 * vw)),
            _const_spec((vw, d)),
            _const_spec((heads, rc, rc)),
            _const_spec((rc, qk)),
            _const_spec((rc, qk)),
        ],
        out_specs=pl.BlockSpec((1, tm, d), lambda i, j: (i, j, 0)),
        scratch_shapes=[
            pltpu.VMEM((heads, dk, dv), F32),
            pltpu.VMEM((tm, qk), BF16),
            pltpu.VMEM((tm, qk), BF16),
            pltpu.VMEM((tm, qk), BF16),
            pltpu.VMEM((tm, qk), BF16),
            pltpu.VMEM((tm, vw), BF16),
            pltpu.VMEM((tm, vw), F32),
            pltpu.VMEM((tm, vw), BF16),
        ],
        compiler_params=pltpu.CompilerParams(
            dimension_semantics=("parallel", "arbitrary"),
            vmem_limit_bytes=VMEM_LIMIT_BYTES),
        name="retention",
    )(x, positions.reshape(b, s, 1), mod, norm_g.reshape(1, d), inv_freq.reshape(1, dk // 2),
      w_in.astype(BF16), w_out.astype(BF16), intra, xi_t, zeta_t)


META_E1, META_E2, META_R1, META_R2, META_W1, META_W2 = range(6)


def _router_kernel(x_ref, mod_ref, ng_ref, wr_ref, br_ref, meta_ref, cnt_ref, carry_ref):
    @pl.when((pl.program_id(0) == 0) & (pl.program_id(1) == 0))
    def _():
        carry_ref[...] = jnp.zeros(carry_ref.shape, F32)

    x = x_ref[0]
    mod = mod_ref[0]
    h = _modnorm(x, ng_ref[...], mod[4:5], mod[3:4])
    tm = h.shape[0]
    lane = lax.broadcasted_iota(jnp.int32, (tm, LANES), 1)
    neg = jnp.float32(-jnp.inf)
    logits = jnp.full((tm, LANES), neg, F32)
    for e in range(N_EXPERTS):
        le = jnp.sum(h * wr_ref[e:e + 1, :], axis=-1, keepdims=True)
        logits = jnp.where(lane == e, le, logits)
    logits = logits + br_ref[...]
    m1 = jnp.max(logits, axis=-1, keepdims=True)
    i1 = jnp.min(jnp.where(logits == m1, lane, LANES), axis=-1, keepdims=True)
    rest = jnp.where(lane == i1, neg, logits)
    m2 = jnp.max(rest, axis=-1, keepdims=True)
    i2 = jnp.min(jnp.where(rest == m2, lane, LANES), axis=-1, keepdims=True)
    e2 = jnp.exp(m2 - m1)
    w1 = 1.0 / (1.0 + e2)
    w2 = e2 / (1.0 + e2)
    sel = jnp.where((lane == i1) | (lane == i2), 1.0, 0.0)
    row = lax.broadcasted_iota(jnp.int32, (tm, tm), 0)
    col = lax.broadcasted_iota(jnp.int32, (tm, tm), 1)
    tri = jnp.where(row > col, 1.0, 0.0).astype(BF16)
    before = jnp.dot(tri, sel.astype(BF16), preferred_element_type=F32) + carry_ref[...]
    r1 = jnp.sum(jnp.where(lane == i1, before, 0.0), axis=-1, keepdims=True)
    r2 = jnp.sum(jnp.where(lane == i2, before, 0.0), axis=-1, keepdims=True)
    carry_ref[...] += jnp.sum(sel, axis=0, keepdims=True)
    cnt_ref[...] = carry_ref[...]
    meta = jnp.zeros(logits.shape, F32)
    for k, val in ((META_E1, i1.astype(F32)), (META_E2, i2.astype(F32)), (META_R1, r1),
                   (META_R2, r2), (META_W1, w1), (META_W2, w2)):
        meta = jnp.where(lane == k, val, meta)
    meta_ref[0] = meta


def _router(x, mod, norm_g, w_router, b_router, *, tm=512):
    b, s, d = x.shape
    ne = w_router.shape[1]
    assert ne == N_EXPERTS
    wr = w_router.T
    br = jnp.zeros((1, LANES), F32).at[0, :ne].set(b_router)
    return pl.pallas_call(
        _router_kernel,
        out_shape=(jax.ShapeDtypeStruct((b, s, LANES), F32),
                   jax.ShapeDtypeStruct((1, LANES), F32)),
        grid=(b, s // tm),
        in_specs=[
            pl.BlockSpec((1, tm, d), lambda i, j: (i, j, 0)),
            pl.BlockSpec((1, 6, d), lambda i, j: (i, 0, 0)),
            _const_spec((1, d)),
            _const_spec((ne, d)),
            _const_spec((1, LANES)),
        ],
        out_specs=(pl.BlockSpec((1, tm, LANES), lambda i, j: (i, j, 0)),
                   pl.BlockSpec((1, LANES), lambda i, j: (0, 0))),
        scratch_shapes=[pltpu.VMEM((1, LANES), F32)],
        compiler_params=pltpu.CompilerParams(
            dimension_semantics=("arbitrary", "arbitrary"),
            vmem_limit_bytes=VMEM_LIMIT_BYTES),
        name="router",
    )(x, mod, norm_g.reshape(1, d), wr, br)


N_ZERO_SLOTS = 2 * N_EXPERTS


def _dispatch_kernel(info_ref, pos_ref, x_ref, mod_ref, ng_ref, hs_hbm, hbuf, zeros_ref,
                     sem, zsem, *, td, tr):
    i = pl.program_id(0)
    slot = lax.rem(i, 2)

    def zero_copy(k):
        start = pl.multiple_of(info_ref[N_ZERO_SLOTS + k] * tr, tr)
        return pltpu.make_async_copy(zeros_ref, hs_hbm.at[pl.ds(start, tr)], zsem)

    @pl.when(i == 0)
    def _():
        zeros_ref[...] = jnp.zeros(zeros_ref.shape, F32)
        for k in range(N_ZERO_SLOTS):
            @pl.when(info_ref[k] == 1)
            def _():
                zero_copy(k).start()
        for k in range(N_ZERO_SLOTS):
            @pl.when(info_ref[k] == 1)
            def _():
                zero_copy(k).wait()

    mod = mod_ref[0]
    hbuf[slot] = _modnorm(x_ref[...], ng_ref[...], mod[4:5], mod[3:4])

    for t in range(td):
        src = hbuf.at[slot, pl.ds(t, 1)]
        pltpu.make_async_copy(src, hs_hbm.at[pl.ds(pos_ref[0, 0, t], 1)], sem.at[slot]).start()
        pltpu.make_async_copy(src, hs_hbm.at[pl.ds(pos_ref[0, 0, td + t], 1)],
                              sem.at[slot]).start()

    def wait_rows(s):
        pltpu.make_async_copy(hs_hbm.at[pl.ds(0, 2 * td)], hs_hbm.at[pl.ds(0, 2 * td)],
                              sem.at[s]).wait()

    @pl.when(i > 0)
    def _():
        wait_rows(1 - slot)

    @pl.when(i == pl.num_programs(0) - 1)
    def _():
        wait_rows(slot)


def _dispatch(x, mod, norm_g, pos, zero_info, *, n_rows, seq, td, tr):
    n, d = x.shape
    kern = functools.partial(_dispatch_kernel, td=td, tr=tr)
    return pl.pallas_call(
        kern,
        out_shape=jax.ShapeDtypeStruct((n_rows, d), F32),
        grid_spec=pltpu.PrefetchScalarGridSpec(
            num_scalar_prefetch=1,
            grid=(n // td,),
            in_specs=[
                pl.BlockSpec((1, 1, 2 * td), lambda i, info: (i, 0, 0),
                             memory_space=pltpu.SMEM),
                pl.BlockSpec((td, d), lambda i, info: (i, 0)),
                pl.BlockSpec((1, 6, d), lambda i, info: ((i * td) // seq, 0, 0)),
                pl.BlockSpec((1, d), lambda i, info: (0, 0)),
            ],
            out_specs=pl.BlockSpec(memory_space=pl.ANY),
            scratch_shapes=[pltpu.VMEM((2, td, d), F32), pltpu.VMEM((tr, d), F32),
                            pltpu.SemaphoreType.DMA((2,)), pltpu.SemaphoreType.DMA],
        ),
        compiler_params=pltpu.CompilerParams(
            dimension_semantics=("arbitrary",), vmem_limit_bytes=VMEM_LIMIT_BYTES),
        name="dispatch",
    )(zero_info, pos, x, mod, norm_g.reshape(1, d))


def _experts_kernel(te_ref, nu_ref, hs_ref, wgu_ref, wd_ref, y_ref, *, ffn, col_block):
    i = pl.program_id(0)

    @pl.when(i < nu_ref[0])
    def _():
        h = hs_ref[...].astype(BF16)
        y = jnp.zeros(y_ref.shape, F32)
        for j in range(ffn // col_block):
            g = jnp.dot(h, wgu_ref[0, :, j * col_block:(j + 1) * col_block],
                        preferred_element_type=F32)
            u = jnp.dot(h, wgu_ref[0, :, ffn + j * col_block:ffn + (j + 1) * col_block],
                        preferred_element_type=F32)
            a = (jax.nn.silu(g) * u).astype(BF16)
            y = y + jnp.dot(a, wd_ref[0, j * col_block:(j + 1) * col_block, :],
                            preferred_element_type=F32)
        y_ref[...] = y

    @pl.when(i >= nu_ref[0])
    def _():
        y_ref[...] = jnp.zeros(y_ref.shape, F32)


def _experts(hs, tile_expert, n_used, w_gate_up, w_down, *, tr):
    n_rows, d = hs.shape
    ne, ffn, _ = w_down.shape
    kern = functools.partial(_experts_kernel, ffn=ffn, col_block=256)
    return pl.pallas_call(
        kern,
        out_shape=jax.ShapeDtypeStruct((n_rows, d), F32),
        grid_spec=pltpu.PrefetchScalarGridSpec(
            num_scalar_prefetch=2,
            grid=(n_rows // tr,),
            in_specs=[
                pl.BlockSpec((tr, d), lambda i, te, nu: (jnp.minimum(i, nu[0] - 1), 0)),
                pl.BlockSpec((1, d, 2 * ffn), lambda i, te, nu: (te[i], 0, 0)),
                pl.BlockSpec((1, ffn, d), lambda i, te, nu: (te[i], 0, 0)),
            ],
            out_specs=pl.BlockSpec((tr, d), lambda i, te, nu: (i, 0)),
        ),
        compiler_params=pltpu.CompilerParams(
            dimension_semantics=("arbitrary",), vmem_limit_bytes=VMEM_LIMIT_BYTES),
        name="experts",
    )(tile_expert, n_used, hs, w_gate_up.astype(BF16), w_down.astype(BF16))


def _combine_kernel(posc_ref, posn_ref, x_ref, meta_ref, mod_ref, fg_ref, y_hbm, o_ref,
                    ybuf, sem, *, tc):
    i = pl.program_id(0)
    slot = lax.rem(i, 2)

    def issue(pos_ref, to_slot):
        for t in range(2 * tc):
            pltpu.make_async_copy(y_hbm.at[pl.ds(pos_ref[0, 0, t], 1)],
                                  ybuf.at[to_slot, pl.ds(t, 1)], sem.at[to_slot]).start()

    @pl.when(i == 0)
    def _():
        issue(posc_ref, 0)

    @pl.when(i + 1 < pl.num_programs(0))
    def _():
        issue(posn_ref, 1 - slot)

    pltpu.make_async_copy(y_hbm.at[pl.ds(0, 2 * tc)], ybuf.at[slot], sem.at[slot]).wait()
    meta = meta_ref[...]
    lane = lax.broadcasted_iota(jnp.int32, meta.shape, 1)
    w1 = jnp.sum(jnp.where(lane == META_W1, meta, 0.0), axis=-1, keepdims=True)
    w2 = jnp.sum(jnp.where(lane == META_W2, meta, 0.0), axis=-1, keepdims=True)
    y = w1 * ybuf[slot, pl.ds(0, tc), :] + w2 * ybuf[slot, pl.ds(tc, tc), :]
    xo = x_ref[...] + mod_ref[0][5:6] * y
    ms = jnp.mean(xo * xo, axis=-1, keepdims=True)
    o_ref[...] = xo * lax.rsqrt(ms + EPS) * fg_ref[...]


def _combine(x, meta, mod, final_g, y_sorted, pos, *, seq, tc):
    n, d = x.shape
    nblk = n // tc
    kern = functools.partial(_combine_kernel, tc=tc)
    smem_pos = lambda f: pl.BlockSpec((1, 1, 2 * tc), f, memory_space=pltpu.SMEM)
    return pl.pallas_call(
        kern,
        out_shape=jax.ShapeDtypeStruct((n, d), F32),
        grid=(nblk,),
        in_specs=[
            smem_pos(lambda i: (i, 0, 0)),
            smem_pos(lambda i: (jnp.minimum(i + 1, nblk - 1), 0, 0)),
            pl.BlockSpec((tc, d), lambda i: (i, 0)),
            pl.BlockSpec((tc, LANES), lambda i: (i, 0)),
            pl.BlockSpec((1, 6, d), lambda i: ((i * tc) // seq, 0, 0)),
            _const_spec((1, d)),
            pl.BlockSpec(memory_space=pl.ANY),
        ],
        out_specs=pl.BlockSpec((tc, d), lambda i: (i, 0)),
        scratch_shapes=[pltpu.VMEM((2, 2 * tc, d), F32), pltpu.SemaphoreType.DMA((2,))],
        compiler_params=pltpu.CompilerParams(
            dimension_semantics=("arbitrary",), vmem_limit_bytes=VMEM_LIMIT_BYTES),
        name="combine",
    )(pos, pos, x, meta, mod, final_g.reshape(1, d), y_sorted)


def _moe_layer(x, mod, norm_g, final_g, w_router, b_router, w_gate_up, w_down,
               *, tr=512, td=512, tc=256):
    b, s, d = x.shape
    n = b * s
    ne = w_router.shape[1]
    meta, cnt = _router(x, mod, norm_g, w_router, b_router)
    meta = meta.reshape(n, LANES)
    e1 = meta[:, META_E1].astype(jnp.int32)
    e2 = meta[:, META_E2].astype(jnp.int32)
    r1 = meta[:, META_R1].astype(jnp.int32)
    r2 = meta[:, META_R2].astype(jnp.int32)
    counts = cnt[0, :ne].astype(jnp.int32)
    tiles_per = (counts + tr - 1) // tr
    tile_end = jnp.cumsum(tiles_per)
    n_used = tile_end[-1]
    group_start = (tile_end - tiles_per) * tr
    pos1 = group_start[e1] + r1
    pos2 = group_start[e2] + r2
    nt = 2 * n // tr + ne
    tile_ids = jnp.arange(nt, dtype=jnp.int32)
    te = jnp.sum((tile_ids[:, None] >= tile_end[None, :]).astype(jnp.int32), axis=1)
    te = jnp.minimum(te, ne - 1)
    te = jnp.where(tile_ids < n_used, te, te[n_used - 1]).astype(jnp.int32)
    zero_tiles = jnp.concatenate([tile_end - 1, n_used + jnp.arange(ne, dtype=jnp.int32)])
    zero_valid = jnp.concatenate([(counts % tr) != 0, n_used + jnp.arange(ne) < nt])
    zero_info = jnp.concatenate([zero_valid.astype(jnp.int32),
                                 jnp.clip(zero_tiles, 0, nt - 1).astype(jnp.int32)])

    def blocked(t):
        return jnp.concatenate([pos1.reshape(n // t, 1, t), pos2.reshape(n // t, 1, t)], axis=-1)

    hs = _dispatch(x.reshape(n, d), mod, norm_g, blocked(td), zero_info, n_rows=nt * tr,
                   seq=s, td=td, tr=tr)
    y_sorted = _experts(hs, te, n_used.reshape(1).astype(jnp.int32), w_gate_up, w_down, tr=tr)
    out = _combine(x.reshape(n, d), meta, mod, final_g, y_sorted, blocked(tc), seq=s, tc=tc)
    return out.reshape(b, s, d)


def kernel(x, c, positions, ada_w, ada_b, mix_norm_g, ffn_norm_g, sgu_w_in, sgu_ln_g, sgu_ln_b,
           sgu_w_s, sgu_b_s, sgu_w_out, ffn_w_gate_up, ffn_w_down, ret_w_in, ret_w_out,
           moe_w_router, moe_b_router, moe_w_gate_up, moe_w_down, final_norm_g):
    b, s, d = x.shape
    depth = ada_w.shape[0]
    assert depth == 2, "layer 0 = SGU + SwiGLU, layer 1 = retention + MoE"
    mod = _adaln(c, ada_w, ada_b).reshape(depth, b, 6, d)
    x = _sgu_layer(x, mod[0], mix_norm_g[0], sgu_w_in[0], sgu_ln_g[0], sgu_ln_b[0],
                   sgu_w_s[0], sgu_b_s[0], sgu_w_out[0])
    x = _ffn_layer(x, mod[0], ffn_norm_g[0], ffn_w_gate_up[0], ffn_w_down[0])
    x = _retention_layer(x, positions, mod[1], mix_norm_g[1], ret_w_in[0], ret_w_out[0])
    return _moe_layer(x, mod[1], ffn_norm_g[1], final_norm_g, moe_w_router[0], moe_b_router[0],
                      moe_w_gate_up[0], moe_w_down[0])
```

```python
import functools

import jax
import jax.numpy as jnp
import numpy as np
from jax import lax
from jax.experimental import pallas as pl
from jax.experimental.pallas import tpu as pltpu

F32 = jnp.float32
BF16 = jnp.bfloat16

CHUNK = 128
SGU_GROUPS = 8
RET_HEADS = 4
N_EXPERTS = 8
ROPE_BASE = 10000.0
EPS = 1e-6
LANES = 128
BF16_TILE_ROWS = 16
VMEM_LIMIT_BYTES = 56 * 1024 * 1024


def _const_spec(shape):
    nd = len(shape)
    return pl.BlockSpec(shape, lambda *_: (0,) * nd, pipeline_mode=pl.Buffered(1))


def _modnorm(x, g, scale, shift):
    ms = jnp.mean(x * x, axis=-1, keepdims=True)
    return (x * lax.rsqrt(ms + EPS) * g) * (1.0 + scale) + shift


def _adaln_kernel(c_ref, w_ref, b_ref, o_ref):
    sc = jax.nn.silu(c_ref[...])
    o_ref[0] = jnp.dot(sc, w_ref[0], precision=lax.Precision.HIGHEST,
                       preferred_element_type=F32) + b_ref[0]


def _adaln(c, ada_w, ada_b):
    depth, d, six_d = ada_w.shape
    b = c.shape[0]
    tn = 1024
    return pl.pallas_call(
        _adaln_kernel,
        out_shape=jax.ShapeDtypeStruct((depth, b, six_d), F32),
        grid=(depth, six_d // tn),
        in_specs=[
            pl.BlockSpec((b, d), lambda l, j: (0, 0)),
            pl.BlockSpec((1, d, tn), lambda l, j: (l, 0, j)),
            pl.BlockSpec((1, 1, tn), lambda l, j: (l, 0, j)),
        ],
        out_specs=pl.BlockSpec((1, b, tn), lambda l, j: (l, 0, j)),
        compiler_params=pltpu.CompilerParams(
            dimension_semantics=("parallel", "parallel"),
            vmem_limit_bytes=VMEM_LIMIT_BYTES),
        name="adaln",
    )(c, ada_w, ada_b.reshape(depth, 1, six_d))


def _sgu_kernel(x_ref, mod_ref, ng_ref, win_ref, lng_ref, lnb_ref, ws_ref, bs_ref,
                wout_ref, *rest, tm, width, col_block, n_cast):
    cast_in = rest[:n_cast]
    o_ref = rest[n_cast]
    cast_out = rest[n_cast + 1:2 * n_cast + 1]
    u_ref, vn_ref, y_ref = rest[2 * n_cast + 1:]
    for src, dst in zip(cast_in, cast_out):
        dst[...] = src[...].astype(BF16)
    x = x_ref[0]
    mod = mod_ref[0]
    h = _modnorm(x, ng_ref[...], mod[1:2], mod[0:1]).astype(BF16)
    nb = width // col_block
    vs = []
    for j in range(2 * nb):
        z = jnp.dot(h, win_ref[:, j * col_block:(j + 1) * col_block],
                    preferred_element_type=F32)
        z = jax.nn.gelu(z)
        if j < nb:
            u_ref[:, j * col_block:(j + 1) * col_block] = z
        else:
            vs.append(z)
    s1 = sum(jnp.sum(v, axis=-1, keepdims=True) for v in vs)
    mu = s1 * (1.0 / width)
    s2 = sum(jnp.sum((v - mu) * (v - mu), axis=-1, keepdims=True) for v in vs)
    rstd = lax.rsqrt(s2 * (1.0 / width) + EPS)
    for j, v in enumerate(vs):
        sl = slice(j * col_block, (j + 1) * col_block)
        vn_ref[:, sl] = ((v - mu) * rstd * lng_ref[:, sl] + lnb_ref[:, sl]).astype(BF16)
    gd = width // SGU_GROUPS
    row = lax.broadcasted_iota(jnp.int32, (CHUNK, CHUNK), 0)
    col = lax.broadcasted_iota(jnp.int32, (CHUNK, CHUNK), 1)
    causal = row >= col
    for g in range(SGU_GROUPS):
        w = jnp.where(causal, ws_ref[g], jnp.zeros((), BF16))
        cs = slice(g * gd, (g + 1) * gd)
        for c in range(tm // CHUNK):
            rs = slice(c * CHUNK, (c + 1) * CHUNK)
            fv = jnp.dot(w, vn_ref[rs, cs], preferred_element_type=F32) + bs_ref[g]
            y_ref[rs, cs] = (u_ref[rs, cs] * fv).astype(BF16)
    out = jnp.dot(y_ref[...], wout_ref[...], preferred_element_type=F32)
    o_ref[0] = x + mod[2:3] * out


def _cast_blocks(rows, steps):
    for br in range(BF16_TILE_ROWS, rows + 1, BF16_TILE_ROWS):
        if rows % br == 0 and rows // br <= steps:
            return br, rows // br
    raise ValueError(f"no bf16-tile row block casts {rows} rows in {steps} steps")


def _sgu_layer(x, mod, norm_g, w_in, ln_g, ln_b, w_s, b_s, w_out, later_weights, *, tm=512):
    b, s, d = x.shape
    width = w_out.shape[0]
    gd = width // SGU_GROUPS
    bs_full = jnp.broadcast_to(b_s[:, :, None], (SGU_GROUPS, CHUNK, gd))
    steps_j = s // tm
    plans = [_cast_blocks(w.shape[0], b * steps_j) for w in later_weights]

    def cast_spec(w, plan):
        br, nblk = plan
        return pl.BlockSpec((br, w.shape[1]),
                            lambda i, j: (jnp.minimum(i * steps_j + j, nblk - 1), 0))

    cast_specs = [cast_spec(w, p) for w, p in zip(later_weights, plans)]
    kern = functools.partial(_sgu_kernel, tm=tm, width=width, col_block=512,
                             n_cast=len(later_weights))
    outs = pl.pallas_call(
        kern,
        out_shape=[jax.ShapeDtypeStruct((b, s, d), F32)]
        + [jax.ShapeDtypeStruct(w.shape, BF16) for w in later_weights],
        grid=(b, steps_j),
        in_specs=[
            pl.BlockSpec((1, tm, d), lambda i, j: (i, j, 0)),
            pl.BlockSpec((1, 6, d), lambda i, j: (i, 0, 0)),
            _const_spec((1, d)),
            _const_spec((d, 2 * width)),
            _const_spec((1, width)),
            _const_spec((1, width)),
            _const_spec((SGU_GROUPS, CHUNK, CHUNK)),
            _const_spec((SGU_GROUPS, CHUNK, gd)),
            _const_spec((width, d)),
        ] + cast_specs,
        out_specs=[pl.BlockSpec((1, tm, d), lambda i, j: (i, j, 0))] + cast_specs,
        scratch_shapes=[
            pltpu.VMEM((tm, width), F32),
            pltpu.VMEM((tm, width), BF16),
            pltpu.VMEM((tm, width), BF16),
        ],
        compiler_params=pltpu.CompilerParams(
            dimension_semantics=("arbitrary", "arbitrary"),
            vmem_limit_bytes=VMEM_LIMIT_BYTES),
        name="sgu",
    )(x, mod, norm_g.reshape(1, d), w_in.astype(BF16), ln_g.reshape(1, width),
      ln_b.reshape(1, width), w_s.astype(BF16), bs_full, w_out.astype(BF16), *later_weights)
    return outs[0], outs[1:]


def _ffn_kernel(x_ref, mod_ref, ng_ref, wgu_ref, wd_ref, o_ref, *, ffn, col_block):
    x = x_ref[0]
    mod = mod_ref[0]
    h = _modnorm(x, ng_ref[...], mod[4:5], mod[3:4]).astype(BF16)
    acc = jnp.zeros(x.shape, F32)
    for j in range(ffn // col_block):
        g = jnp.dot(h, wgu_ref[:, j * col_block:(j + 1) * col_block],
                    preferred_element_type=F32)
        u = jnp.dot(h, wgu_ref[:, ffn + j * col_block:ffn + (j + 1) * col_block],
                    preferred_element_type=F32)
        a = (jax.nn.silu(g) * u).astype(BF16)
        acc = acc + jnp.dot(a, wd_ref[j * col_block:(j + 1) * col_block, :],
                            preferred_element_type=F32)
    o_ref[0] = x + mod[5:6] * acc


def _ffn_layer(x, mod, norm_g, w_gate_up, w_down, *, tm=512):
    b, s, d = x.shape
    ffn = w_down.shape[0]
    kern = functools.partial(_ffn_kernel, ffn=ffn, col_block=256)
    return pl.pallas_call(
        kern,
        out_shape=jax.ShapeDtypeStruct((b, s, d), F32),
        grid=(b, s // tm),
        in_specs=[
            pl.BlockSpec((1, tm, d), lambda i, j: (i, j, 0)),
            pl.BlockSpec((1, 6, d), lambda i, j: (i, 0, 0)),
            _const_spec((1, d)),
            _const_spec((d, 2 * ffn)),
            _const_spec((ffn, d)),
        ],
        out_specs=pl.BlockSpec((1, tm, d), lambda i, j: (i, j, 0)),
        compiler_params=pltpu.CompilerParams(
            dimension_semantics=("parallel", "parallel"),
            vmem_limit_bytes=VMEM_LIMIT_BYTES),
        name="ffn",
    )(x, mod, norm_g.reshape(1, d), w_gate_up.astype(BF16), w_down.astype(BF16))


def _retention_kernel(x_ref, pos_ref, mod_ref, ng_ref, invf_ref, win_ref, wout_ref,
                      intra_ref, xi_ref, zeta_ref, o_ref,
                      state_ref, q_ref, qx_ref, k_ref, kz_ref, v_ref, sg_ref, ob_ref,
                      *, tm, rc, dk, dv, decay):
    heads = RET_HEADS
    qk = heads * dk

    @pl.when(pl.program_id(1) == 0)
    def _():
        state_ref[...] = jnp.zeros(state_ref.shape, F32)

    mod = mod_ref[0]
    half = dk // 2

    def rope(p, cos, sin):
        t1, t2 = p[:, :half], p[:, half:]
        return jnp.concatenate([t1 * cos - t2 * sin, t2 * cos + t1 * sin], axis=-1)

    for c in range(tm // rc):
        rs = slice(c * rc, (c + 1) * rc)
        x = x_ref[0, rs, :]
        h = _modnorm(x, ng_ref[...], mod[1:2], mod[0:1]).astype(BF16)
        ang = pos_ref[0, rs, :].astype(F32) * invf_ref[...]
        cos = jnp.cos(ang)
        sin = jnp.sin(ang)
        for hd in range(heads):
            cs = slice(hd * dk, (hd + 1) * dk)
            q = rope(jnp.dot(h, win_ref[:, cs], preferred_element_type=F32), cos, sin)
            q_ref[rs, cs] = q.astype(BF16)
            qx_ref[rs, cs] = (q * xi_ref[:, cs]).astype(BF16)
            k = rope(jnp.dot(h, win_ref[:, qk + hd * dk:qk + (hd + 1) * dk],
                             preferred_element_type=F32), cos, sin) * (dk ** -0.5)
            k_ref[rs, cs] = k.astype(BF16)
            kz_ref[rs, cs] = (k * zeta_ref[:, cs]).astype(BF16)
        for hd in range(heads):
            cs = slice(hd * dv, (hd + 1) * dv)
            v_ref[rs, cs] = jnp.dot(h, win_ref[:, 2 * qk + hd * dv:2 * qk + (hd + 1) * dv],
                                    preferred_element_type=F32).astype(BF16)
            g = jnp.dot(h, win_ref[:, 2 * qk + heads * dv + hd * dv:
                                   2 * qk + heads * dv + (hd + 1) * dv],
                        preferred_element_type=F32)
            sg_ref[rs, cs] = jax.nn.silu(g)
        for hd in range(heads):
            ks = slice(hd * dk, (hd + 1) * dk)
            vs = slice(hd * dv, (hd + 1) * dv)
            vh = v_ref[rs, vs]
            scores = lax.dot_general(q_ref[rs, ks], k_ref[rs, ks], (((1,), (1,)), ((), ())),
                                     preferred_element_type=F32)
            scores = (scores * intra_ref[hd]).astype(BF16)
            st = state_ref[hd]
            o = (jnp.dot(scores, vh, preferred_element_type=F32)
                 + jnp.dot(qx_ref[rs, ks], st.astype(BF16), preferred_element_type=F32))
            state_ref[hd] = st * decay[hd] + lax.dot_general(
                kz_ref[rs, ks], vh, (((0,), (0,)), ((), ())), preferred_element_type=F32)
            on = o * lax.rsqrt(jnp.mean(o * o, axis=-1, keepdims=True) + EPS)
            ob_ref[rs, vs] = (on * sg_ref[rs, vs]).astype(BF16)
        out = jnp.dot(ob_ref[rs, :], wout_ref[...], preferred_element_type=F32)
        o_ref[0, rs, :] = x + mod[2:3] * out


def _retention_layer(x, positions, mod, norm_g, w_in, w_out, *, tm=512, rc=256):
    b, s, d = x.shape
    heads = RET_HEADS
    dk = d // heads
    dv = w_out.shape[0] // heads
    qk = heads * dk
    vw = heads * dv
    log_gamma = jnp.log1p(-jnp.exp2(-5.0 - jnp.arange(heads, dtype=F32)))
    idx = jnp.arange(rc, dtype=F32)
    rel = idx[:, None] - idx[None, :]
    intra = jnp.where(rel >= 0, jnp.exp(log_gamma[:, None, None] * jnp.maximum(rel, 0.0)), 0.0)
    xi = jnp.exp(log_gamma[:, None] * (idx + 1.0))
    zeta = jnp.exp(log_gamma[:, None] * (rc - 1.0 - idx))
    xi_t = jnp.repeat(xi.T, dk, axis=1)
    zeta_t = jnp.repeat(zeta.T, dk, axis=1)
    decay = tuple(float(np.exp(np.log1p(-2.0 ** (-5.0 - hd)) * rc)) for hd in range(heads))
    inv_freq = 1.0 / (ROPE_BASE ** (jnp.arange(0, dk, 2, dtype=F32) / dk))
    kern = functools.partial(_retention_kernel, tm=tm, rc=rc, dk=dk, dv=dv, decay=decay)
    return pl.pallas_call(
        kern,
        out_shape=jax.ShapeDtypeStruct((b, s, d), F32),
        grid=(b, s // tm),
        in_specs=[
            pl.BlockSpec((1, tm, d), lambda i, j: (i, j, 0)),
            pl.BlockSpec((1, tm, 1), lambda i, j: (i, j, 0)),
            pl.BlockSpec((1, 6, d), lambda i, j: (i, 0, 0)),
            _const_spec((1, d)),
            _const_spec((1, dk // 2)),
            _const_spec((d, 2 * qk + 2 * vw)),
            _const_spec((vw, d)),
            _const_spec((heads, rc, rc)),
            _const_spec((rc, qk)),
            _const_spec((rc, qk)),
        ],
        out_specs=pl.BlockSpec((1, tm, d), lambda i, j: (i, j, 0)),
        scratch_shapes=[
            pltpu.VMEM((heads, dk, dv), F32),
            pltpu.VMEM((tm, qk), BF16),
            pltpu.VMEM((tm, qk), BF16),
            pltpu.VMEM((tm, qk), BF16),
            pltpu.VMEM((tm, qk), BF16),
            pltpu.VMEM((tm, vw), BF16),
            pltpu.VMEM((tm, vw), F32),
            pltpu.VMEM((tm, vw), BF16),
        ],
        compiler_params=pltpu.CompilerParams(
            dimension_semantics=("parallel", "arbitrary"),
            vmem_limit_bytes=VMEM_LIMIT_BYTES),
        name="retention",
    )(x, positions.reshape(b, s, 1), mod, norm_g.reshape(1, d), inv_freq.reshape(1, dk // 2),
      w_in.astype(BF16), w_out.astype(BF16), intra, xi_t, zeta_t)


META_E1, META_E2, META_R1, META_R2, META_W1, META_W2 = range(6)
META_ROWS = 8


def _router_kernel(x_ref, mod_ref, ng_ref, wr_ref, br_ref, meta_ref, meta_t_ref, cnt_ref,
                   carry_ref):
    @pl.when((pl.program_id(0) == 0) & (pl.program_id(1) == 0))
    def _():
        carry_ref[...] = jnp.zeros(carry_ref.shape, F32)

    x = x_ref[0]
    mod = mod_ref[0]
    h = _modnorm(x, ng_ref[...], mod[4:5], mod[3:4])
    tm = h.shape[0]
    lane = lax.broadcasted_iota(jnp.int32, (tm, LANES), 1)
    neg = jnp.float32(-jnp.inf)
    logits = jnp.full((tm, LANES), neg, F32)
    for e in range(N_EXPERTS):
        le = jnp.sum(h * wr_ref[e:e + 1, :], axis=-1, keepdims=True)
        logits = jnp.where(lane == e, le, logits)
    logits = logits + br_ref[...]
    m1 = jnp.max(logits, axis=-1, keepdims=True)
    i1 = jnp.min(jnp.where(logits == m1, lane, LANES), axis=-1, keepdims=True)
    rest = jnp.where(lane == i1, neg, logits)
    m2 = jnp.max(rest, axis=-1, keepdims=True)
    i2 = jnp.min(jnp.where(rest == m2, lane, LANES), axis=-1, keepdims=True)
    e2 = jnp.exp(m2 - m1)
    w1 = 1.0 / (1.0 + e2)
    w2 = e2 / (1.0 + e2)
    sel = jnp.where((lane == i1) | (lane == i2), 1.0, 0.0)
    row = lax.broadcasted_iota(jnp.int32, (tm, tm), 0)
    col = lax.broadcasted_iota(jnp.int32, (tm, tm), 1)
    tri = jnp.where(row > col, 1.0, 0.0).astype(BF16)
    before = jnp.dot(tri, sel.astype(BF16), preferred_element_type=F32) + carry_ref[...]
    r1 = jnp.sum(jnp.where(lane == i1, before, 0.0), axis=-1, keepdims=True)
    r2 = jnp.sum(jnp.where(lane == i2, before, 0.0), axis=-1, keepdims=True)
    carry_ref[...] += jnp.sum(sel, axis=0, keepdims=True)
    cnt_ref[...] = carry_ref[...]
    meta = jnp.zeros(logits.shape, F32)
    for k, val in ((META_E1, i1.astype(F32)), (META_E2, i2.astype(F32)), (META_R1, r1),
                   (META_R2, r2), (META_W1, w1), (META_W2, w2)):
        meta = jnp.where(lane == k, val, meta)
    meta_ref[0] = meta
    meta_t_ref[...] = meta.T[:META_ROWS, :]


def _router(x, mod, norm_g, w_router, b_router, *, tm=512):
    b, s, d = x.shape
    ne = w_router.shape[1]
    assert ne == N_EXPERTS
    wr = w_router.T
    br = jnp.zeros((1, LANES), F32).at[0, :ne].set(b_router)
    return pl.pallas_call(
        _router_kernel,
        out_shape=(jax.ShapeDtypeStruct((b, s, LANES), F32),
                   jax.ShapeDtypeStruct((META_ROWS, b * s), F32),
                   jax.ShapeDtypeStruct((1, LANES), F32)),
        grid=(b, s // tm),
        in_specs=[
            pl.BlockSpec((1, tm, d), lambda i, j: (i, j, 0)),
            pl.BlockSpec((1, 6, d), lambda i, j: (i, 0, 0)),
            _const_spec((1, d)),
            _const_spec((ne, d)),
            _const_spec((1, LANES)),
        ],
        out_specs=(pl.BlockSpec((1, tm, LANES), lambda i, j: (i, j, 0)),
                   pl.BlockSpec((META_ROWS, tm), lambda i, j: (0, i * (s // tm) + j)),
                   pl.BlockSpec((1, LANES), lambda i, j: (0, 0))),
        scratch_shapes=[pltpu.VMEM((1, LANES), F32)],
        compiler_params=pltpu.CompilerParams(
            dimension_semantics=("arbitrary", "arbitrary"),
            vmem_limit_bytes=VMEM_LIMIT_BYTES),
        name="router",
    )(x, mod, norm_g.reshape(1, d), wr, br)


N_ZERO_SLOTS = 2 * N_EXPERTS


def _dispatch_kernel(info_ref, pos_ref, x_ref, mod_ref, ng_ref, hs_hbm, hbuf, zeros_ref,
                     sem, zsem, *, td, tr):
    i = pl.program_id(0)
    slot = lax.rem(i, 2)

    def zero_copy(k):
        start = pl.multiple_of(info_ref[N_ZERO_SLOTS + k] * tr, tr)
        return pltpu.make_async_copy(zeros_ref, hs_hbm.at[pl.ds(start, tr)], zsem)

    @pl.when(i == 0)
    def _():
        zeros_ref[...] = jnp.zeros(zeros_ref.shape, F32)
        for k in range(N_ZERO_SLOTS):
            @pl.when(info_ref[k] == 1)
            def _():
                zero_copy(k).start()
        for k in range(N_ZERO_SLOTS):
            @pl.when(info_ref[k] == 1)
            def _():
                zero_copy(k).wait()

    mod = mod_ref[0]
    hbuf[slot] = _modnorm(x_ref[...], ng_ref[...], mod[4:5], mod[3:4])

    for t in range(td):
        src = hbuf.at[slot, pl.ds(t, 1)]
        pltpu.make_async_copy(src, hs_hbm.at[pl.ds(pos_ref[0, 0, t], 1)], sem.at[slot]).start()
        pltpu.make_async_copy(src, hs_hbm.at[pl.ds(pos_ref[0, 0, td + t], 1)],
                              sem.at[slot]).start()

    def wait_rows(s):
        pltpu.make_async_copy(hs_hbm.at[pl.ds(0, 2 * td)], hs_hbm.at[pl.ds(0, 2 * td)],
                              sem.at[s]).wait()

    @pl.when(i > 0)
    def _():
        wait_rows(1 - slot)

    @pl.when(i == pl.num_programs(0) - 1)
    def _():
        wait_rows(slot)


def _dispatch(x, mod, norm_g, pos, zero_info, *, n_rows, seq, td, tr):
    n, d = x.shape
    kern = functools.partial(_dispatch_kernel, td=td, tr=tr)
    return pl.pallas_call(
        kern,
        out_shape=jax.ShapeDtypeStruct((n_rows, d), F32),
        grid_spec=pltpu.PrefetchScalarGridSpec(
            num_scalar_prefetch=1,
            grid=(n // td,),
            in_specs=[
                pl.BlockSpec((1, 1, 2 * td), lambda i, info: (i, 0, 0),
                             memory_space=pltpu.SMEM),
                pl.BlockSpec((td, d), lambda i, info: (i, 0)),
                pl.BlockSpec((1, 6, d), lambda i, info: ((i * td) // seq, 0, 0)),
                pl.BlockSpec((1, d), lambda i, info: (0, 0)),
            ],
            out_specs=pl.BlockSpec(memory_space=pl.ANY),
            scratch_shapes=[pltpu.VMEM((2, td, d), F32), pltpu.VMEM((tr, d), F32),
                            pltpu.SemaphoreType.DMA((2,)), pltpu.SemaphoreType.DMA],
        ),
        compiler_params=pltpu.CompilerParams(
            dimension_semantics=("arbitrary",), vmem_limit_bytes=VMEM_LIMIT_BYTES),
        name="dispatch",
    )(zero_info, pos, x, mod, norm_g.reshape(1, d))


def _experts_kernel(te_ref, nu_ref, hs_ref, wgu_ref, wd_ref, y_ref, *, ffn, col_block):
    i = pl.program_id(0)

    @pl.when(i < nu_ref[0])
    def _():
        h = hs_ref[...].astype(BF16)
        y = jnp.zeros(y_ref.shape, F32)
        for j in range(ffn // col_block):
            g = jnp.dot(h, wgu_ref[0, :, j * col_block:(j + 1) * col_block],
                        preferred_element_type=F32)
            u = jnp.dot(h, wgu_ref[0, :, ffn + j * col_block:ffn + (j + 1) * col_block],
                        preferred_element_type=F32)
            a = (jax.nn.silu(g) * u).astype(BF16)
            y = y + jnp.dot(a, wd_ref[0, j * col_block:(j + 1) * col_block, :],
                            preferred_element_type=F32)
        y_ref[...] = y

    @pl.when(i >= nu_ref[0])
    def _():
        y_ref[...] = jnp.zeros(y_ref.shape, F32)


def _experts(hs, tile_expert, n_used, w_gate_up, w_down, *, tr):
    n_rows, d = hs.shape
    ne, ffn, _ = w_down.shape
    kern = functools.partial(_experts_kernel, ffn=ffn, col_block=256)
    return pl.pallas_call(
        kern,
        out_shape=jax.ShapeDtypeStruct((n_rows, d), F32),
        grid_spec=pltpu.PrefetchScalarGridSpec(
            num_scalar_prefetch=2,
            grid=(n_rows // tr,),
            in_specs=[
                pl.BlockSpec((tr, d), lambda i, te, nu: (jnp.minimum(i, nu[0] - 1), 0)),
                pl.BlockSpec((1, d, 2 * ffn), lambda i, te, nu: (te[i], 0, 0)),
                pl.BlockSpec((1, ffn, d), lambda i, te, nu: (te[i], 0, 0)),
            ],
            out_specs=pl.BlockSpec((tr, d), lambda i, te, nu: (i, 0)),
        ),
        compiler_params=pltpu.CompilerParams(
            dimension_semantics=("arbitrary",), vmem_limit_bytes=VMEM_LIMIT_BYTES),
        name="experts",
    )(tile_expert, n_used, hs, w_gate_up.astype(BF16), w_down.astype(BF16))


def _combine_kernel(posc_ref, posn_ref, x_ref, meta_ref, mod_ref, fg_ref, y_hbm, o_ref,
                    ybuf, sem, *, tc):
    i = pl.program_id(0)
    slot = lax.rem(i, 2)

    def issue(pos_ref, to_slot):
        for t in range(2 * tc):
            pltpu.make_async_copy(y_hbm.at[pl.ds(pos_ref[0, 0, t], 1)],
                                  ybuf.at[to_slot, pl.ds(t, 1)], sem.at[to_slot]).start()

    @pl.when(i == 0)
    def _():
        issue(posc_ref, 0)

    @pl.when(i + 1 < pl.num_programs(0))
    def _():
        issue(posn_ref, 1 - slot)

    pltpu.make_async_copy(y_hbm.at[pl.ds(0, 2 * tc)], ybuf.at[slot], sem.at[slot]).wait()
    meta = meta_ref[...]
    lane = lax.broadcasted_iota(jnp.int32, meta.shape, 1)
    w1 = jnp.sum(jnp.where(lane == META_W1, meta, 0.0), axis=-1, keepdims=True)
    w2 = jnp.sum(jnp.where(lane == META_W2, meta, 0.0), axis=-1, keepdims=True)
    y = w1 * ybuf[slot, pl.ds(0, tc), :] + w2 * ybuf[slot, pl.ds(tc, tc), :]
    xo = x_ref[...] + mod_ref[0][5:6] * y
    ms = jnp.mean(xo * xo, axis=-1, keepdims=True)
    o_ref[...] = xo * lax.rsqrt(ms + EPS) * fg_ref[...]


def _combine(x, meta, mod, final_g, y_sorted, pos, *, seq, tc):
    n, d = x.shape
    nblk = n // tc
    kern = functools.partial(_combine_kernel, tc=tc)
    smem_pos = lambda f: pl.BlockSpec((1, 1, 2 * tc), f, memory_space=pltpu.SMEM)
    return pl.pallas_call(
        kern,
        out_shape=jax.ShapeDtypeStruct((n, d), F32),
        grid=(nblk,),
        in_specs=[
            smem_pos(lambda i: (i, 0, 0)),
            smem_pos(lambda i: (jnp.minimum(i + 1, nblk - 1), 0, 0)),
            pl.BlockSpec((tc, d), lambda i: (i, 0)),
            pl.BlockSpec((tc, LANES), lambda i: (i, 0)),
            pl.BlockSpec((1, 6, d), lambda i: ((i * tc) // seq, 0, 0)),
            _const_spec((1, d)),
            pl.BlockSpec(memory_space=pl.ANY),
        ],
        out_specs=pl.BlockSpec((tc, d), lambda i: (i, 0)),
        scratch_shapes=[pltpu.VMEM((2, 2 * tc, d), F32), pltpu.SemaphoreType.DMA((2,))],
        compiler_params=pltpu.CompilerParams(
            dimension_semantics=("arbitrary",), vmem_limit_bytes=VMEM_LIMIT_BYTES),
        name="combine",
    )(pos, pos, x, meta, mod, final_g.reshape(1, d), y_sorted)


def _moe_layer(x, mod, norm_g, final_g, w_router, b_router, w_gate_up, w_down,
               *, tr=512, td=512, tc=256):
    b, s, d = x.shape
    n = b * s
    ne = w_router.shape[1]
    meta, meta_t, cnt = _router(x, mod, norm_g, w_router, b_router)
    meta = meta.reshape(n, LANES)
    e1 = meta_t[META_E1].astype(jnp.int32)
    e2 = meta_t[META_E2].astype(jnp.int32)
    r1 = meta_t[META_R1].astype(jnp.int32)
    r2 = meta_t[META_R2].astype(jnp.int32)
    counts = cnt[0, :ne].astype(jnp.int32)
    tiles_per = (counts + tr - 1) // tr
    tile_end = jnp.cumsum(tiles_per)
    n_used = tile_end[-1]
    group_start = (tile_end - tiles_per) * tr
    pos1 = group_start[e1] + r1
    pos2 = group_start[e2] + r2
    nt = 2 * n // tr + ne
    tile_ids = jnp.arange(nt, dtype=jnp.int32)
    te = jnp.sum((tile_ids[:, None] >= tile_end[None, :]).astype(jnp.int32), axis=1)
    te = jnp.minimum(te, ne - 1)
    te = jnp.where(tile_ids < n_used, te, te[n_used - 1]).astype(jnp.int32)
    zero_tiles = jnp.concatenate([tile_end - 1, n_used + jnp.arange(ne, dtype=jnp.int32)])
    zero_valid = jnp.concatenate([(counts % tr) != 0, n_used + jnp.arange(ne) < nt])
    zero_info = jnp.concatenate([zero_valid.astype(jnp.int32),
                                 jnp.clip(zero_tiles, 0, nt - 1).astype(jnp.int32)])

    def blocked(t):
        return jnp.concatenate([pos1.reshape(n // t, 1, t), pos2.reshape(n // t, 1, t)], axis=-1)

    hs = _dispatch(x.reshape(n, d), mod, norm_g, blocked(td), zero_info, n_rows=nt * tr,
                   seq=s, td=td, tr=tr)
    y_sorted = _experts(hs, te, n_used.reshape(1).astype(jnp.int32), w_gate_up, w_down, tr=tr)
    out = _combine(x.reshape(n, d), meta, mod, final_g, y_sorted, blocked(tc), seq=s, tc=tc)
    return out.reshape(b, s, d)


def kernel(x, c, positions, ada_w, ada_b, mix_norm_g, ffn_norm_g, sgu_w_in, sgu_ln_g, sgu_ln_b,
           sgu_w_s, sgu_b_s, sgu_w_out, ffn_w_gate_up, ffn_w_down, ret_w_in, ret_w_out,
           moe_w_router, moe_b_router, moe_w_gate_up, moe_w_down, final_norm_g):
    b, s, d = x.shape
    depth = ada_w.shape[0]
    assert depth == 2, "layer 0 = SGU + SwiGLU, layer 1 = retention + MoE"
    mod = _adaln(c, ada_w, ada_b).reshape(depth, b, 6, d)
    ne, _, two_ffn = moe_w_gate_up[0].shape
    ffn = moe_w_down[0].shape[1]
    later = [ffn_w_gate_up[0], ffn_w_down[0], ret_w_in[0], ret_w_out[0],
             moe_w_gate_up[0].reshape(ne * d, two_ffn), moe_w_down[0].reshape(ne * ffn, d)]
    x, (ffn_gu, ffn_dn, ret_in, ret_out, moe_gu, moe_dn) = _sgu_layer(
        x, mod[0], mix_norm_g[0], sgu_w_in[0], sgu_ln_g[0], sgu_ln_b[0], sgu_w_s[0],
        sgu_b_s[0], sgu_w_out[0], later)
    x = _ffn_layer(x, mod[0], ffn_norm_g[0], ffn_gu, ffn_dn)
    x = _retention_layer(x, positions, mod[1], mix_norm_g[1], ret_in, ret_out)
    return _moe_layer(x, mod[1], ffn_norm_g[1], final_norm_g, moe_w_router[0], moe_b_router[0],
                      moe_gu.reshape(ne, d, two_ffn), moe_dn.reshape(ne, ffn, d))
```

```python
import functools

import jax
import jax.numpy as jnp
import numpy as np
from jax import lax
from jax.experimental import pallas as pl
from jax.experimental.pallas import tpu as pltpu

F32 = jnp.float32
BF16 = jnp.bfloat16

CHUNK = 128
SGU_GROUPS = 8
RET_HEADS = 4
N_EXPERTS = 8
ROPE_BASE = 10000.0
EPS = 1e-6
LANES = 128
SUBLANES = 8
BF16_TILE_ROWS = 16
VMEM_LIMIT_BYTES = 56 * 1024 * 1024


def _const_spec(shape):
    nd = len(shape)
    return pl.BlockSpec(shape, lambda *_: (0,) * nd, pipeline_mode=pl.Buffered(1))


def _store_slabs(ref, rows, first, count):
    for c in range(SUBLANES):
        ref[pl.ds(first * SUBLANES + c, count, stride=SUBLANES), :] = \
            rows[:, c * LANES:(c + 1) * LANES]


def _load_slabs(ref, first, count):
    return jnp.concatenate(
        [ref[pl.ds(first * SUBLANES + c, count, stride=SUBLANES), :] for c in range(SUBLANES)],
        axis=-1)


def _modnorm(x, g, scale, shift):
    ms = jnp.mean(x * x, axis=-1, keepdims=True)
    return (x * lax.rsqrt(ms + EPS) * g) * (1.0 + scale) + shift


def _adaln_kernel(c_ref, w_ref, b_ref, o_ref):
    sc = jax.nn.silu(c_ref[...])
    o_ref[0] = jnp.dot(sc, w_ref[0], precision=lax.Precision.HIGHEST,
                       preferred_element_type=F32) + b_ref[0]


def _adaln(c, ada_w, ada_b):
    depth, d, six_d = ada_w.shape
    b = c.shape[0]
    tn = 1024
    return pl.pallas_call(
        _adaln_kernel,
        out_shape=jax.ShapeDtypeStruct((depth, b, six_d), F32),
        grid=(depth, six_d // tn),
        in_specs=[
            pl.BlockSpec((b, d), lambda l, j: (0, 0)),
            pl.BlockSpec((1, d, tn), lambda l, j: (l, 0, j)),
            pl.BlockSpec((1, 1, tn), lambda l, j: (l, 0, j)),
        ],
        out_specs=pl.BlockSpec((1, b, tn), lambda l, j: (l, 0, j)),
        compiler_params=pltpu.CompilerParams(
            dimension_semantics=("parallel", "parallel"),
            vmem_limit_bytes=VMEM_LIMIT_BYTES),
        name="adaln",
    )(c, ada_w, ada_b.reshape(depth, 1, six_d))


def _sgu_kernel(x_ref, mod_ref, ng_ref, win_ref, lng_ref, lnb_ref, ws_ref, bs_ref,
                wout_ref, *rest, tm, width, col_block, n_cast):
    cast_in = rest[:n_cast]
    o_ref = rest[n_cast]
    cast_out = rest[n_cast + 1:2 * n_cast + 1]
    u_ref, vn_ref, y_ref = rest[2 * n_cast + 1:]
    for src, dst in zip(cast_in, cast_out):
        dst[...] = src[...].astype(BF16)
    x = x_ref[0]
    mod = mod_ref[0]
    h = _modnorm(x, ng_ref[...], mod[1:2], mod[0:1]).astype(BF16)
    nb = width // col_block
    vs = []
    for j in range(2 * nb):
        z = jnp.dot(h, win_ref[:, j * col_block:(j + 1) * col_block],
                    preferred_element_type=F32)
        z = jax.nn.gelu(z)
        if j < nb:
            u_ref[:, j * col_block:(j + 1) * col_block] = z
        else:
            vs.append(z)
    s1 = sum(jnp.sum(v, axis=-1, keepdims=True) for v in vs)
    mu = s1 * (1.0 / width)
    s2 = sum(jnp.sum((v - mu) * (v - mu), axis=-1, keepdims=True) for v in vs)
    rstd = lax.rsqrt(s2 * (1.0 / width) + EPS)
    for j, v in enumerate(vs):
        sl = slice(j * col_block, (j + 1) * col_block)
        vn_ref[:, sl] = ((v - mu) * rstd * lng_ref[:, sl] + lnb_ref[:, sl]).astype(BF16)
    gd = width // SGU_GROUPS
    row = lax.broadcasted_iota(jnp.int32, (CHUNK, CHUNK), 0)
    col = lax.broadcasted_iota(jnp.int32, (CHUNK, CHUNK), 1)
    causal = row >= col
    for g in range(SGU_GROUPS):
        w = jnp.where(causal, ws_ref[g], jnp.zeros((), BF16))
        cs = slice(g * gd, (g + 1) * gd)
        for c in range(tm // CHUNK):
            rs = slice(c * CHUNK, (c + 1) * CHUNK)
            fv = jnp.dot(w, vn_ref[rs, cs], preferred_element_type=F32) + bs_ref[g]
            y_ref[rs, cs] = (u_ref[rs, cs] * fv).astype(BF16)
    out = jnp.dot(y_ref[...], wout_ref[...], preferred_element_type=F32)
    o_ref[0] = x + mod[2:3] * out


def _cast_blocks(rows, steps):
    for br in range(BF16_TILE_ROWS, rows + 1, BF16_TILE_ROWS):
        if rows % br == 0 and rows // br <= steps:
            return br, rows // br
    raise ValueError(f"no bf16-tile row block casts {rows} rows in {steps} steps")


def _sgu_layer(x, mod, norm_g, w_in, ln_g, ln_b, w_s, b_s, w_out, later_weights, *, tm=512):
    b, s, d = x.shape
    width = w_out.shape[0]
    gd = width // SGU_GROUPS
    bs_full = jnp.broadcast_to(b_s[:, :, None], (SGU_GROUPS, CHUNK, gd))
    steps_j = s // tm
    plans = [_cast_blocks(w.shape[0], b * steps_j) for w in later_weights]

    def cast_spec(w, plan):
        br, nblk = plan
        return pl.BlockSpec((br, w.shape[1]),
                            lambda i, j: (jnp.minimum(i * steps_j + j, nblk - 1), 0))

    cast_specs = [cast_spec(w, p) for w, p in zip(later_weights, plans)]
    kern = functools.partial(_sgu_kernel, tm=tm, width=width, col_block=512,
                             n_cast=len(later_weights))
    outs = pl.pallas_call(
        kern,
        out_shape=[jax.ShapeDtypeStruct((b, s, d), F32)]
        + [jax.ShapeDtypeStruct(w.shape, BF16) for w in later_weights],
        grid=(b, steps_j),
        in_specs=[
            pl.BlockSpec((1, tm, d), lambda i, j: (i, j, 0)),
            pl.BlockSpec((1, 6, d), lambda i, j: (i, 0, 0)),
            _const_spec((1, d)),
            _const_spec((d, 2 * width)),
            _const_spec((1, width)),
            _const_spec((1, width)),
            _const_spec((SGU_GROUPS, CHUNK, CHUNK)),
            _const_spec((SGU_GROUPS, CHUNK, gd)),
            _const_spec((width, d)),
        ] + cast_specs,
        out_specs=[pl.BlockSpec((1, tm, d), lambda i, j: (i, j, 0))] + cast_specs,
        scratch_shapes=[
            pltpu.VMEM((tm, width), F32),
            pltpu.VMEM((tm, width), BF16),
            pltpu.VMEM((tm, width), BF16),
        ],
        compiler_params=pltpu.CompilerParams(
            dimension_semantics=("arbitrary", "arbitrary"),
            vmem_limit_bytes=VMEM_LIMIT_BYTES),
        name="sgu",
    )(x, mod, norm_g.reshape(1, d), w_in.astype(BF16), ln_g.reshape(1, width),
      ln_b.reshape(1, width), w_s.astype(BF16), bs_full, w_out.astype(BF16), *later_weights)
    return outs[0], outs[1:]


def _ffn_kernel(x_ref, mod_ref, ng_ref, wgu_ref, wd_ref, o_ref, *, ffn, col_block):
    x = x_ref[0]
    mod = mod_ref[0]
    h = _modnorm(x, ng_ref[...], mod[4:5], mod[3:4]).astype(BF16)
    acc = jnp.zeros(x.shape, F32)
    for j in range(ffn // col_block):
        g = jnp.dot(h, wgu_ref[:, j * col_block:(j + 1) * col_block],
                    preferred_element_type=F32)
        u = jnp.dot(h, wgu_ref[:, ffn + j * col_block:ffn + (j + 1) * col_block],
                    preferred_element_type=F32)
        a = (jax.nn.silu(g) * u).astype(BF16)
        acc = acc + jnp.dot(a, wd_ref[j * col_block:(j + 1) * col_block, :],
                            preferred_element_type=F32)
    o_ref[0] = x + mod[5:6] * acc


def _ffn_layer(x, mod, norm_g, w_gate_up, w_down, *, tm=512):
    b, s, d = x.shape
    ffn = w_down.shape[0]
    kern = functools.partial(_ffn_kernel, ffn=ffn, col_block=256)
    return pl.pallas_call(
        kern,
        out_shape=jax.ShapeDtypeStruct((b, s, d), F32),
        grid=(b, s // tm),
        in_specs=[
            pl.BlockSpec((1, tm, d), lambda i, j: (i, j, 0)),
            pl.BlockSpec((1, 6, d), lambda i, j: (i, 0, 0)),
            _const_spec((1, d)),
            _const_spec((d, 2 * ffn)),
            _const_spec((ffn, d)),
        ],
        out_specs=pl.BlockSpec((1, tm, d), lambda i, j: (i, j, 0)),
        compiler_params=pltpu.CompilerParams(
            dimension_semantics=("parallel", "parallel"),
            vmem_limit_bytes=VMEM_LIMIT_BYTES),
        name="ffn",
    )(x, mod, norm_g.reshape(1, d), w_gate_up.astype(BF16), w_down.astype(BF16))


def _retention_kernel(x_ref, pos_ref, mod_ref, ng_ref, invf_ref, win_ref, wout_ref,
                      intra_ref, xi_ref, zeta_ref, o_ref,
                      state_ref, q_ref, qx_ref, k_ref, kz_ref, v_ref, sg_ref, ob_ref,
                      *, tm, rc, dk, dv, decay):
    heads = RET_HEADS
    qk = heads * dk

    @pl.when(pl.program_id(1) == 0)
    def _():
        state_ref[...] = jnp.zeros(state_ref.shape, F32)

    mod = mod_ref[0]
    half = dk // 2

    def rope(p, cos, sin):
        t1, t2 = p[:, :half], p[:, half:]
        return jnp.concatenate([t1 * cos - t2 * sin, t2 * cos + t1 * sin], axis=-1)

    for c in range(tm // rc):
        rs = slice(c * rc, (c + 1) * rc)
        x = x_ref[0, rs, :]
        h = _modnorm(x, ng_ref[...], mod[1:2], mod[0:1]).astype(BF16)
        ang = pos_ref[0, rs, :].astype(F32) * invf_ref[...]
        cos = jnp.cos(ang)
        sin = jnp.sin(ang)
        for hd in range(heads):
            cs = slice(hd * dk, (hd + 1) * dk)
            q = rope(jnp.dot(h, win_ref[:, cs], preferred_element_type=F32), cos, sin)
            q_ref[rs, cs] = q.astype(BF16)
            qx_ref[rs, cs] = (q * xi_ref[:, cs]).astype(BF16)
            k = rope(jnp.dot(h, win_ref[:, qk + hd * dk:qk + (hd + 1) * dk],
                             preferred_element_type=F32), cos, sin) * (dk ** -0.5)
            k_ref[rs, cs] = k.astype(BF16)
            kz_ref[rs, cs] = (k * zeta_ref[:, cs]).astype(BF16)
        for hd in range(heads):
            cs = slice(hd * dv, (hd + 1) * dv)
            v_ref[rs, cs] = jnp.dot(h, win_ref[:, 2 * qk + hd * dv:2 * qk + (hd + 1) * dv],
                                    preferred_element_type=F32).astype(BF16)
            g = jnp.dot(h, win_ref[:, 2 * qk + heads * dv + hd * dv:
                                   2 * qk + heads * dv + (hd + 1) * dv],
                        preferred_element_type=F32)
            sg_ref[rs, cs] = jax.nn.silu(g)
        for hd in range(heads):
            ks = slice(hd * dk, (hd + 1) * dk)
            vs = slice(hd * dv, (hd + 1) * dv)
            vh = v_ref[rs, vs]
            scores = lax.dot_general(q_ref[rs, ks], k_ref[rs, ks], (((1,), (1,)), ((), ())),
                                     preferred_element_type=F32)
            scores = (scores * intra_ref[hd]).astype(BF16)
            st = state_ref[hd]
            o = (jnp.dot(scores, vh, preferred_element_type=F32)
                 + jnp.dot(qx_ref[rs, ks], st.astype(BF16), preferred_element_type=F32))
            state_ref[hd] = st * decay[hd] + lax.dot_general(
                kz_ref[rs, ks], vh, (((0,), (0,)), ((), ())), preferred_element_type=F32)
            on = o * lax.rsqrt(jnp.mean(o * o, axis=-1, keepdims=True) + EPS)
            ob_ref[rs, vs] = (on * sg_ref[rs, vs]).astype(BF16)
        out = jnp.dot(ob_ref[rs, :], wout_ref[...], preferred_element_type=F32)
        o_ref[0, rs, :] = x + mod[2:3] * out


def _retention_layer(x, positions, mod, norm_g, w_in, w_out, *, tm=512, rc=256):
    b, s, d = x.shape
    heads = RET_HEADS
    dk = d // heads
    dv = w_out.shape[0] // heads
    qk = heads * dk
    vw = heads * dv
    log_gamma = jnp.log1p(-jnp.exp2(-5.0 - jnp.arange(heads, dtype=F32)))
    idx = jnp.arange(rc, dtype=F32)
    rel = idx[:, None] - idx[None, :]
    intra = jnp.where(rel >= 0, jnp.exp(log_gamma[:, None, None] * jnp.maximum(rel, 0.0)), 0.0)
    xi = jnp.exp(log_gamma[:, None] * (idx + 1.0))
    zeta = jnp.exp(log_gamma[:, None] * (rc - 1.0 - idx))
    xi_t = jnp.repeat(xi.T, dk, axis=1)
    zeta_t = jnp.repeat(zeta.T, dk, axis=1)
    decay = tuple(float(np.exp(np.log1p(-2.0 ** (-5.0 - hd)) * rc)) for hd in range(heads))
    inv_freq = 1.0 / (ROPE_BASE ** (jnp.arange(0, dk, 2, dtype=F32) / dk))
    kern = functools.partial(_retention_kernel, tm=tm, rc=rc, dk=dk, dv=dv, decay=decay)
    return pl.pallas_call(
        kern,
        out_shape=jax.ShapeDtypeStruct((b, s, d), F32),
        grid=(b, s // tm),
        in_specs=[
            pl.BlockSpec((1, tm, d), lambda i, j: (i, j, 0)),
            pl.BlockSpec((1, tm, 1), lambda i, j: (i, j, 0)),
            pl.BlockSpec((1, 6, d), lambda i, j: (i, 0, 0)),
            _const_spec((1, d)),
            _const_spec((1, dk // 2)),
            _const_spec((d, 2 * qk + 2 * vw)),
            _const_spec((vw, d)),
            _const_spec((heads, rc, rc)),
            _const_spec((rc, qk)),
            _const_spec((rc, qk)),
        ],
        out_specs=pl.BlockSpec((1, tm, d), lambda i, j: (i, j, 0)),
        scratch_shapes=[
            pltpu.VMEM((heads, dk, dv), F32),
            pltpu.VMEM((tm, qk), BF16),
            pltpu.VMEM((tm, qk), BF16),
            pltpu.VMEM((tm, qk), BF16),
            pltpu.VMEM((tm, qk), BF16),
            pltpu.VMEM((tm, vw), BF16),
            pltpu.VMEM((tm, vw), F32),
            pltpu.VMEM((tm, vw), BF16),
        ],
        compiler_params=pltpu.CompilerParams(
            dimension_semantics=("parallel", "arbitrary"),
            vmem_limit_bytes=VMEM_LIMIT_BYTES),
        name="retention",
    )(x, positions.reshape(b, s, 1), mod, norm_g.reshape(1, d), inv_freq.reshape(1, dk // 2),
      w_in.astype(BF16), w_out.astype(BF16), intra, xi_t, zeta_t)


META_E1, META_E2, META_R1, META_R2, META_W1, META_W2 = range(6)
META_ROWS = 8


def _router_kernel(x_ref, mod_ref, ng_ref, wr_ref, br_ref, meta_ref, meta_t_ref, cnt_ref,
                   carry_ref):
    @pl.when((pl.program_id(0) == 0) & (pl.program_id(1) == 0))
    def _():
        carry_ref[...] = jnp.zeros(carry_ref.shape, F32)

    x = x_ref[0]
    mod = mod_ref[0]
    h = _modnorm(x, ng_ref[...], mod[4:5], mod[3:4])
    tm = h.shape[0]
    lane = lax.broadcasted_iota(jnp.int32, (tm, LANES), 1)
    neg = jnp.float32(-jnp.inf)
    logits = jnp.full((tm, LANES), neg, F32)
    for e in range(N_EXPERTS):
        le = jnp.sum(h * wr_ref[e:e + 1, :], axis=-1, keepdims=True)
        logits = jnp.where(lane == e, le, logits)
    logits = logits + br_ref[...]
    m1 = jnp.max(logits, axis=-1, keepdims=True)
    i1 = jnp.min(jnp.where(logits == m1, lane, LANES), axis=-1, keepdims=True)
    rest = jnp.where(lane == i1, neg, logits)
    m2 = jnp.max(rest, axis=-1, keepdims=True)
    i2 = jnp.min(jnp.where(rest == m2, lane, LANES), axis=-1, keepdims=True)
    e2 = jnp.exp(m2 - m1)
    w1 = 1.0 / (1.0 + e2)
    w2 = e2 / (1.0 + e2)
    sel = jnp.where((lane == i1) | (lane == i2), 1.0, 0.0)
    row = lax.broadcasted_iota(jnp.int32, (tm, tm), 0)
    col = lax.broadcasted_iota(jnp.int32, (tm, tm), 1)
    tri = jnp.where(row > col, 1.0, 0.0).astype(BF16)
    before = jnp.dot(tri, sel.astype(BF16), preferred_element_type=F32) + carry_ref[...]
    r1 = jnp.sum(jnp.where(lane == i1, before, 0.0), axis=-1, keepdims=True)
    r2 = jnp.sum(jnp.where(lane == i2, before, 0.0), axis=-1, keepdims=True)
    carry_ref[...] += jnp.sum(sel, axis=0, keepdims=True)
    cnt_ref[...] = carry_ref[...]
    meta = jnp.zeros(logits.shape, F32)
    for k, val in ((META_E1, i1.astype(F32)), (META_E2, i2.astype(F32)), (META_R1, r1),
                   (META_R2, r2), (META_W1, w1), (META_W2, w2)):
        meta = jnp.where(lane == k, val, meta)
    meta_ref[0] = meta
    meta_t_ref[...] = meta.T[:META_ROWS, :]


def _router(x, mod, norm_g, w_router, b_router, *, tm=512):
    b, s, d = x.shape
    ne = w_router.shape[1]
    assert ne == N_EXPERTS
    wr = w_router.T
    br = jnp.zeros((1, LANES), F32).at[0, :ne].set(b_router)
    return pl.pallas_call(
        _router_kernel,
        out_shape=(jax.ShapeDtypeStruct((b, s, LANES), F32),
                   jax.ShapeDtypeStruct((META_ROWS, b * s), F32),
                   jax.ShapeDtypeStruct((1, LANES), F32)),
        grid=(b, s // tm),
        in_specs=[
            pl.BlockSpec((1, tm, d), lambda i, j: (i, j, 0)),
            pl.BlockSpec((1, 6, d), lambda i, j: (i, 0, 0)),
            _const_spec((1, d)),
            _const_spec((ne, d)),
            _const_spec((1, LANES)),
        ],
        out_specs=(pl.BlockSpec((1, tm, LANES), lambda i, j: (i, j, 0)),
                   pl.BlockSpec((META_ROWS, tm), lambda i, j: (0, i * (s // tm) + j)),
                   pl.BlockSpec((1, LANES), lambda i, j: (0, 0))),
        scratch_shapes=[pltpu.VMEM((1, LANES), F32)],
        compiler_params=pltpu.CompilerParams(
            dimension_semantics=("arbitrary", "arbitrary"),
            vmem_limit_bytes=VMEM_LIMIT_BYTES),
        name="router",
    )(x, mod, norm_g.reshape(1, d), wr, br)


N_ZERO_SLOTS = 2 * N_EXPERTS


def _dispatch_kernel(info_ref, pos_ref, x_ref, mod_ref, ng_ref, hs_hbm, hbuf, zeros_ref,
                     sem, zsem, *, td, tr):
    i = pl.program_id(0)
    slot = lax.rem(i, 2)

    def zero_copy(k):
        start = pl.multiple_of(info_ref[N_ZERO_SLOTS + k] * (tr * SUBLANES), tr * SUBLANES)
        return pltpu.make_async_copy(zeros_ref, hs_hbm.at[pl.ds(start, tr * SUBLANES)], zsem)

    @pl.when(i == 0)
    def _():
        zeros_ref[...] = jnp.zeros(zeros_ref.shape, F32)
        for k in range(N_ZERO_SLOTS):
            @pl.when(info_ref[k] == 1)
            def _():
                zero_copy(k).start()
        for k in range(N_ZERO_SLOTS):
            @pl.when(info_ref[k] == 1)
            def _():
                zero_copy(k).wait()

    mod = mod_ref[0]
    h = _modnorm(x_ref[...], ng_ref[...], mod[4:5], mod[3:4])
    _store_slabs(hbuf.at[slot], h, 0, td)

    def hs_slab(p8):
        return hs_hbm.at[pl.ds(pl.multiple_of(p8, SUBLANES), SUBLANES)]

    for t in range(td):
        src = hbuf.at[slot, pl.ds(t * SUBLANES, SUBLANES)]
        pltpu.make_async_copy(src, hs_slab(pos_ref[0, 0, t]), sem.at[slot]).start()
        pltpu.make_async_copy(src, hs_slab(pos_ref[0, 0, td + t]), sem.at[slot]).start()

    def wait_rows(s):
        n_sub = 2 * td * SUBLANES
        pltpu.make_async_copy(hs_hbm.at[pl.ds(0, n_sub)], hs_hbm.at[pl.ds(0, n_sub)],
                              sem.at[s]).wait()

    @pl.when(i > 0)
    def _():
        wait_rows(1 - slot)

    @pl.when(i == pl.num_programs(0) - 1)
    def _():
        wait_rows(slot)


def _dispatch(x, mod, norm_g, pos, zero_info, *, n_rows, seq, td, tr):
    n, d = x.shape
    assert d == SUBLANES * LANES, "a row must fill exactly one (8, 128) slab"
    kern = functools.partial(_dispatch_kernel, td=td, tr=tr)
    return pl.pallas_call(
        kern,
        out_shape=jax.ShapeDtypeStruct((n_rows * SUBLANES, LANES), F32),
        grid_spec=pltpu.PrefetchScalarGridSpec(
            num_scalar_prefetch=1,
            grid=(n // td,),
            in_specs=[
                pl.BlockSpec((1, 1, 2 * td), lambda i, info: (i, 0, 0),
                             memory_space=pltpu.SMEM),
                pl.BlockSpec((td, d), lambda i, info: (i, 0)),
                pl.BlockSpec((1, 6, d), lambda i, info: ((i * td) // seq, 0, 0)),
                pl.BlockSpec((1, d), lambda i, info: (0, 0)),
            ],
            out_specs=pl.BlockSpec(memory_space=pl.ANY),
            scratch_shapes=[pltpu.VMEM((2, td * SUBLANES, LANES), F32),
                            pltpu.VMEM((tr * SUBLANES, LANES), F32),
                            pltpu.SemaphoreType.DMA((2,)), pltpu.SemaphoreType.DMA],
        ),
        compiler_params=pltpu.CompilerParams(
            dimension_semantics=("arbitrary",), vmem_limit_bytes=VMEM_LIMIT_BYTES),
        name="dispatch",
    )(zero_info, pos, x, mod, norm_g.reshape(1, d))


def _experts_kernel(te_ref, nu_ref, hs_ref, wgu_ref, wd_ref, y_ref, *, ffn, col_block, tr):
    i = pl.program_id(0)

    @pl.when(i < nu_ref[0])
    def _():
        h = _load_slabs(hs_ref, 0, tr).astype(BF16)
        y = jnp.zeros((tr, SUBLANES * LANES), F32)
        for j in range(ffn // col_block):
            g = jnp.dot(h, wgu_ref[0, :, j * col_block:(j + 1) * col_block],
                        preferred_element_type=F32)
            u = jnp.dot(h, wgu_ref[0, :, ffn + j * col_block:ffn + (j + 1) * col_block],
                        preferred_element_type=F32)
            a = (jax.nn.silu(g) * u).astype(BF16)
            y = y + jnp.dot(a, wd_ref[0, j * col_block:(j + 1) * col_block, :],
                            preferred_element_type=F32)
        _store_slabs(y_ref, y, 0, tr)

    @pl.when(i >= nu_ref[0])
    def _():
        y_ref[...] = jnp.zeros(y_ref.shape, F32)


def _experts(hs, tile_expert, n_used, w_gate_up, w_down, *, tr):
    ne, ffn, d = w_down.shape
    n_tiles = hs.shape[0] // (tr * SUBLANES)
    kern = functools.partial(_experts_kernel, ffn=ffn, col_block=256, tr=tr)
    return pl.pallas_call(
        kern,
        out_shape=jax.ShapeDtypeStruct(hs.shape, F32),
        grid_spec=pltpu.PrefetchScalarGridSpec(
            num_scalar_prefetch=2,
            grid=(n_tiles,),
            in_specs=[
                pl.BlockSpec((tr * SUBLANES, LANES),
                             lambda i, te, nu: (jnp.minimum(i, nu[0] - 1), 0)),
                pl.BlockSpec((1, d, 2 * ffn), lambda i, te, nu: (te[i], 0, 0)),
                pl.BlockSpec((1, ffn, d), lambda i, te, nu: (te[i], 0, 0)),
            ],
            out_specs=pl.BlockSpec((tr * SUBLANES, LANES), lambda i, te, nu: (i, 0)),
        ),
        compiler_params=pltpu.CompilerParams(
            dimension_semantics=("arbitrary",), vmem_limit_bytes=VMEM_LIMIT_BYTES),
        name="experts",
    )(tile_expert, n_used, hs, w_gate_up.astype(BF16), w_down.astype(BF16))


def _combine_kernel(posc_ref, posn_ref, x_ref, meta_ref, mod_ref, fg_ref, y_hbm, o_ref,
                    ybuf, sem, *, tc):
    i = pl.program_id(0)
    slot = lax.rem(i, 2)

    def issue(pos_ref, to_slot):
        for t in range(2 * tc):
            src = y_hbm.at[pl.ds(pl.multiple_of(pos_ref[0, 0, t], SUBLANES), SUBLANES)]
            pltpu.make_async_copy(src, ybuf.at[to_slot, pl.ds(t * SUBLANES, SUBLANES)],
                                  sem.at[to_slot]).start()

    @pl.when(i == 0)
    def _():
        issue(posc_ref, 0)

    @pl.when(i + 1 < pl.num_programs(0))
    def _():
        issue(posn_ref, 1 - slot)

    pltpu.make_async_copy(y_hbm.at[pl.ds(0, 2 * tc * SUBLANES)], ybuf.at[slot],
                          sem.at[slot]).wait()
    meta = meta_ref[...]
    lane = lax.broadcasted_iota(jnp.int32, meta.shape, 1)
    w1 = jnp.sum(jnp.where(lane == META_W1, meta, 0.0), axis=-1, keepdims=True)
    w2 = jnp.sum(jnp.where(lane == META_W2, meta, 0.0), axis=-1, keepdims=True)
    y = w1 * _load_slabs(ybuf.at[slot], 0, tc) + w2 * _load_slabs(ybuf.at[slot], tc, tc)
    xo = x_ref[...] + mod_ref[0][5:6] * y
    ms = jnp.mean(xo * xo, axis=-1, keepdims=True)
    o_ref[...] = xo * lax.rsqrt(ms + EPS) * fg_ref[...]


def _combine(x, meta, mod, final_g, y_sorted, pos, *, seq, tc):
    n, d = x.shape
    nblk = n // tc
    kern = functools.partial(_combine_kernel, tc=tc)
    smem_pos = lambda f: pl.BlockSpec((1, 1, 2 * tc), f, memory_space=pltpu.SMEM)
    return pl.pallas_call(
        kern,
        out_shape=jax.ShapeDtypeStruct((n, d), F32),
        grid=(nblk,),
        in_specs=[
            smem_pos(lambda i: (i, 0, 0)),
            smem_pos(lambda i: (jnp.minimum(i + 1, nblk - 1), 0, 0)),
            pl.BlockSpec((tc, d), lambda i: (i, 0)),
            pl.BlockSpec((tc, LANES), lambda i: (i, 0)),
            pl.BlockSpec((1, 6, d), lambda i: ((i * tc) // seq, 0, 0)),
            _const_spec((1, d)),
            pl.BlockSpec(memory_space=pl.ANY),
        ],
        out_specs=pl.BlockSpec((tc, d), lambda i: (i, 0)),
        scratch_shapes=[pltpu.VMEM((2, 2 * tc * SUBLANES, LANES), F32),
                        pltpu.SemaphoreType.DMA((2,))],
        compiler_params=pltpu.CompilerParams(
            dimension_semantics=("arbitrary",), vmem_limit_bytes=VMEM_LIMIT_BYTES),
        name="combine",
    )(pos, pos, x, meta, mod, final_g.reshape(1, d), y_sorted)


def _moe_layer(x, mod, norm_g, final_g, w_router, b_router, w_gate_up, w_down,
               *, tr=512, td=512, tc=256):
    b, s, d = x.shape
    n = b * s
    ne = w_router.shape[1]
    meta, meta_t, cnt = _router(x, mod, norm_g, w_router, b_router)
    meta = meta.reshape(n, LANES)
    e1 = meta_t[META_E1].astype(jnp.int32)
    e2 = meta_t[META_E2].astype(jnp.int32)
    r1 = meta_t[META_R1].astype(jnp.int32)
    r2 = meta_t[META_R2].astype(jnp.int32)
    counts = cnt[0, :ne].astype(jnp.int32)
    tiles_per = (counts + tr - 1) // tr
    tile_end = jnp.cumsum(tiles_per)
    n_used = tile_end[-1]
    group_start = (tile_end - tiles_per) * tr
    pos1 = group_start[e1] + r1
    pos2 = group_start[e2] + r2
    nt = 2 * n // tr + ne
    tile_ids = jnp.arange(nt, dtype=jnp.int32)
    te = jnp.sum((tile_ids[:, None] >= tile_end[None, :]).astype(jnp.int32), axis=1)
    te = jnp.minimum(te, ne - 1)
    te = jnp.where(tile_ids < n_used, te, te[n_used - 1]).astype(jnp.int32)
    zero_tiles = jnp.concatenate([tile_end - 1, n_used + jnp.arange(ne, dtype=jnp.int32)])
    zero_valid = jnp.concatenate([(counts % tr) != 0, n_used + jnp.arange(ne) < nt])
    zero_info = jnp.concatenate([zero_valid.astype(jnp.int32),
                                 jnp.clip(zero_tiles, 0, nt - 1).astype(jnp.int32)])

    def blocked(t):
        return SUBLANES * jnp.concatenate(
            [pos1.reshape(n // t, 1, t), pos2.reshape(n // t, 1, t)], axis=-1)

    hs = _dispatch(x.reshape(n, d), mod, norm_g, blocked(td), zero_info, n_rows=nt * tr,
                   seq=s, td=td, tr=tr)
    y_sorted = _experts(hs, te, n_used.reshape(1).astype(jnp.int32), w_gate_up, w_down, tr=tr)
    out = _combine(x.reshape(n, d), meta, mod, final_g, y_sorted, blocked(tc), seq=s, tc=tc)
    return out.reshape(b, s, d)


def kernel(x, c, positions, ada_w, ada_b, mix_norm_g, ffn_norm_g, sgu_w_in, sgu_ln_g, sgu_ln_b,
           sgu_w_s, sgu_b_s, sgu_w_out, ffn_w_gate_up, ffn_w_down, ret_w_in, ret_w_out,
           moe_w_router, moe_b_router, moe_w_gate_up, moe_w_down, final_norm_g):
    b, s, d = x.shape
    depth = ada_w.shape[0]
    assert depth == 2, "layer 0 = SGU + SwiGLU, layer 1 = retention + MoE"
    mod = _adaln(c, ada_w, ada_b).reshape(depth, b, 6, d)
    ne, _, two_ffn = moe_w_gate_up[0].shape
    ffn = moe_w_down[0].shape[1]
    later = [ffn_w_gate_up[0], ffn_w_down[0], ret_w_in[0], ret_w_out[0],
             moe_w_gate_up[0].reshape(ne * d, two_ffn), moe_w_down[0].reshape(ne * ffn, d)]
    x, (ffn_gu, ffn_dn, ret_in, ret_out, moe_gu, moe_dn) = _sgu_layer(
        x, mod[0], mix_norm_g[0], sgu_w_in[0], sgu_ln_g[0], sgu_ln_b[0], sgu_w_s[0],
        sgu_b_s[0], sgu_w_out[0], later)
    x = _ffn_layer(x, mod[0], ffn_norm_g[0], ffn_gu, ffn_dn)
    x = _retention_layer(x, positions, mod[1], mix_norm_g[1], ret_in, ret_out)
    return _moe_layer(x, mod[1], ffn_norm_g[1], final_norm_g, moe_w_router[0], moe_b_router[0],
                      moe_gu.reshape(ne, d, two_ffn), moe_dn.reshape(ne, ffn, d))
```

```python
import functools

import jax
import jax.numpy as jnp
import numpy as np
from jax import lax
from jax.experimental import pallas as pl
from jax.experimental.pallas import tpu as pltpu

F32 = jnp.float32
BF16 = jnp.bfloat16

CHUNK = 128
SGU_GROUPS = 8
RET_HEADS = 4
N_EXPERTS = 8
ROPE_BASE = 10000.0
EPS = 1e-6
LANES = 128
SUBLANES = 8
BF16_TILE_ROWS = 16
VMEM_LIMIT_BYTES = 56 * 1024 * 1024


def _const_spec(shape):
    nd = len(shape)
    return pl.BlockSpec(shape, lambda *_: (0,) * nd, pipeline_mode=pl.Buffered(1))


def _store_slabs(ref, rows, first, count):
    for c in range(SUBLANES):
        ref[pl.ds(first * SUBLANES + c, count, stride=SUBLANES), :] = \
            rows[:, c * LANES:(c + 1) * LANES]


def _load_slabs(ref, first, count):
    return jnp.concatenate(
        [ref[pl.ds(first * SUBLANES + c, count, stride=SUBLANES), :] for c in range(SUBLANES)],
        axis=-1)


def _modnorm(x, g, scale, shift):
    ms = jnp.mean(x * x, axis=-1, keepdims=True)
    return (x * lax.rsqrt(ms + EPS) * g) * (1.0 + scale) + shift


def _adaln_kernel(c_ref, w_ref, b_ref, o_ref):
    sc = jax.nn.silu(c_ref[...])
    o_ref[0] = jnp.dot(sc, w_ref[0], precision=lax.Precision.HIGHEST,
                       preferred_element_type=F32) + b_ref[0]


def _adaln(c, ada_w, ada_b):
    depth, d, six_d = ada_w.shape
    b = c.shape[0]
    tn = 1024
    return pl.pallas_call(
        _adaln_kernel,
        out_shape=jax.ShapeDtypeStruct((depth, b, six_d), F32),
        grid=(depth, six_d // tn),
        in_specs=[
            pl.BlockSpec((b, d), lambda l, j: (0, 0)),
            pl.BlockSpec((1, d, tn), lambda l, j: (l, 0, j)),
            pl.BlockSpec((1, 1, tn), lambda l, j: (l, 0, j)),
        ],
        out_specs=pl.BlockSpec((1, b, tn), lambda l, j: (l, 0, j)),
        compiler_params=pltpu.CompilerParams(
            dimension_semantics=("parallel", "parallel"),
            vmem_limit_bytes=VMEM_LIMIT_BYTES),
        name="adaln",
    )(c, ada_w, ada_b.reshape(depth, 1, six_d))


def _sgu_kernel(x_ref, mod_ref, ng_ref, win_ref, lng_ref, lnb_ref, ws_ref, bs_ref,
                wout_ref, *rest, tm, width, col_block, n_cast):
    cast_in = rest[:n_cast]
    o_ref = rest[n_cast]
    cast_out = rest[n_cast + 1:2 * n_cast + 1]
    u_ref, vn_ref, y_ref = rest[2 * n_cast + 1:]
    for src, dst in zip(cast_in, cast_out):
        dst[...] = src[...].astype(BF16)
    x = x_ref[0]
    mod = mod_ref[0]
    h = _modnorm(x, ng_ref[...], mod[1:2], mod[0:1]).astype(BF16)
    nb = width // col_block
    vs = []
    for j in range(2 * nb):
        z = jnp.dot(h, win_ref[:, j * col_block:(j + 1) * col_block],
                    preferred_element_type=F32)
        z = jax.nn.gelu(z)
        if j < nb:
            u_ref[:, j * col_block:(j + 1) * col_block] = z
        else:
            vs.append(z)
    s1 = sum(jnp.sum(v, axis=-1, keepdims=True) for v in vs)
    mu = s1 * (1.0 / width)
    s2 = sum(jnp.sum((v - mu) * (v - mu), axis=-1, keepdims=True) for v in vs)
    rstd = lax.rsqrt(s2 * (1.0 / width) + EPS)
    for j, v in enumerate(vs):
        sl = slice(j * col_block, (j + 1) * col_block)
        vn_ref[:, sl] = ((v - mu) * rstd * lng_ref[:, sl] + lnb_ref[:, sl]).astype(BF16)
    gd = width // SGU_GROUPS
    row = lax.broadcasted_iota(jnp.int32, (CHUNK, CHUNK), 0)
    col = lax.broadcasted_iota(jnp.int32, (CHUNK, CHUNK), 1)
    causal = row >= col
    for g in range(SGU_GROUPS):
        w = jnp.where(causal, ws_ref[g], jnp.zeros((), BF16))
        cs = slice(g * gd, (g + 1) * gd)
        for c in range(tm // CHUNK):
            rs = slice(c * CHUNK, (c + 1) * CHUNK)
            fv = jnp.dot(w, vn_ref[rs, cs], preferred_element_type=F32) + bs_ref[g]
            y_ref[rs, cs] = (u_ref[rs, cs] * fv).astype(BF16)
    out = jnp.dot(y_ref[...], wout_ref[...], preferred_element_type=F32)
    o_ref[0] = x + mod[2:3] * out


def _cast_blocks(rows, steps):
    for br in range(BF16_TILE_ROWS, rows + 1, BF16_TILE_ROWS):
        if rows % br == 0 and rows // br <= steps:
            return br, rows // br
    raise ValueError(f"no bf16-tile row block casts {rows} rows in {steps} steps")


def _sgu_layer(x, mod, norm_g, w_in, ln_g, ln_b, w_s, b_s, w_out, later_weights, *, tm=512):
    b, s, d = x.shape
    width = w_out.shape[0]
    gd = width // SGU_GROUPS
    bs_full = jnp.broadcast_to(b_s[:, :, None], (SGU_GROUPS, CHUNK, gd))
    steps_j = s // tm
    plans = [_cast_blocks(w.shape[0], b * steps_j) for w in later_weights]

    def cast_spec(w, plan):
        br, nblk = plan
        return pl.BlockSpec((br, w.shape[1]),
                            lambda i, j: (jnp.minimum(i * steps_j + j, nblk - 1), 0))

    cast_specs = [cast_spec(w, p) for w, p in zip(later_weights, plans)]
    kern = functools.partial(_sgu_kernel, tm=tm, width=width, col_block=512,
                             n_cast=len(later_weights))
    outs = pl.pallas_call(
        kern,
        out_shape=[jax.ShapeDtypeStruct((b, s, d), F32)]
        + [jax.ShapeDtypeStruct(w.shape, BF16) for w in later_weights],
        grid=(b, steps_j),
        in_specs=[
            pl.BlockSpec((1, tm, d), lambda i, j: (i, j, 0)),
            pl.BlockSpec((1, 6, d), lambda i, j: (i, 0, 0)),
            _const_spec((1, d)),
            _const_spec((d, 2 * width)),
            _const_spec((1, width)),
            _const_spec((1, width)),
            _const_spec((SGU_GROUPS, CHUNK, CHUNK)),
            _const_spec((SGU_GROUPS, CHUNK, gd)),
            _const_spec((width, d)),
        ] + cast_specs,
        out_specs=[pl.BlockSpec((1, tm, d), lambda i, j: (i, j, 0))] + cast_specs,
        scratch_shapes=[
            pltpu.VMEM((tm, width), F32),
            pltpu.VMEM((tm, width), BF16),
            pltpu.VMEM((tm, width), BF16),
        ],
        compiler_params=pltpu.CompilerParams(
            dimension_semantics=("arbitrary", "arbitrary"),
            vmem_limit_bytes=VMEM_LIMIT_BYTES),
        name="sgu",
    )(x, mod, norm_g.reshape(1, d), w_in.astype(BF16), ln_g.reshape(1, width),
      ln_b.reshape(1, width), w_s.astype(BF16), bs_full, w_out.astype(BF16), *later_weights)
    return outs[0], outs[1:]


def _ffn_kernel(x_ref, mod_ref, ng_ref, wgu_ref, wd_ref, o_ref, *, ffn, col_block):
    x = x_ref[0]
    mod = mod_ref[0]
    h = _modnorm(x, ng_ref[...], mod[4:5], mod[3:4]).astype(BF16)
    acc = jnp.zeros(x.shape, F32)
    for j in range(ffn // col_block):
        g = jnp.dot(h, wgu_ref[:, j * col_block:(j + 1) * col_block],
                    preferred_element_type=F32)
        u = jnp.dot(h, wgu_ref[:, ffn + j * col_block:ffn + (j + 1) * col_block],
                    preferred_element_type=F32)
        a = (jax.nn.silu(g) * u).astype(BF16)
        acc = acc + jnp.dot(a, wd_ref[j * col_block:(j + 1) * col_block, :],
                            preferred_element_type=F32)
    o_ref[0] = x + mod[5:6] * acc


def _ffn_layer(x, mod, norm_g, w_gate_up, w_down, *, tm=512):
    b, s, d = x.shape
    ffn = w_down.shape[0]
    kern = functools.partial(_ffn_kernel, ffn=ffn, col_block=256)
    return pl.pallas_call(
        kern,
        out_shape=jax.ShapeDtypeStruct((b, s, d), F32),
        grid=(b, s // tm),
        in_specs=[
            pl.BlockSpec((1, tm, d), lambda i, j: (i, j, 0)),
            pl.BlockSpec((1, 6, d), lambda i, j: (i, 0, 0)),
            _const_spec((1, d)),
            _const_spec((d, 2 * ffn)),
            _const_spec((ffn, d)),
        ],
        out_specs=pl.BlockSpec((1, tm, d), lambda i, j: (i, j, 0)),
        compiler_params=pltpu.CompilerParams(
            dimension_semantics=("parallel", "parallel"),
            vmem_limit_bytes=VMEM_LIMIT_BYTES),
        name="ffn",
    )(x, mod, norm_g.reshape(1, d), w_gate_up.astype(BF16), w_down.astype(BF16))


def _retention_kernel(x_ref, pos_ref, mod_ref, ng_ref, invf_ref, win_ref, wout_ref,
                      intra_ref, xi_ref, zeta_ref, o_ref,
                      state_ref, q_ref, qx_ref, k_ref, kz_ref, v_ref, sg_ref, ob_ref,
                      *, tm, rc, dk, dv, decay):
    heads = RET_HEADS
    qk = heads * dk

    @pl.when(pl.program_id(1) == 0)
    def _():
        state_ref[...] = jnp.zeros(state_ref.shape, F32)

    mod = mod_ref[0]
    half = dk // 2

    def rope(p, cos, sin):
        t1, t2 = p[:, :half], p[:, half:]
        return jnp.concatenate([t1 * cos - t2 * sin, t2 * cos + t1 * sin], axis=-1)

    for c in range(tm // rc):
        rs = slice(c * rc, (c + 1) * rc)
        x = x_ref[0, rs, :]
        h = _modnorm(x, ng_ref[...], mod[1:2], mod[0:1]).astype(BF16)
        ang = pos_ref[0, rs, :].astype(F32) * invf_ref[...]
        cos = jnp.cos(ang)
        sin = jnp.sin(ang)
        for hd in range(heads):
            cs = slice(hd * dk, (hd + 1) * dk)
            q = rope(jnp.dot(h, win_ref[:, cs], preferred_element_type=F32), cos, sin)
            q_ref[rs, cs] = q.astype(BF16)
            qx_ref[rs, cs] = (q * xi_ref[:, cs]).astype(BF16)
            k = rope(jnp.dot(h, win_ref[:, qk + hd * dk:qk + (hd + 1) * dk],
                             preferred_element_type=F32), cos, sin) * (dk ** -0.5)
            k_ref[rs, cs] = k.astype(BF16)
            kz_ref[rs, cs] = (k * zeta_ref[:, cs]).astype(BF16)
        for hd in range(heads):
            cs = slice(hd * dv, (hd + 1) * dv)
            v_ref[rs, cs] = jnp.dot(h, win_ref[:, 2 * qk + hd * dv:2 * qk + (hd + 1) * dv],
                                    preferred_element_type=F32).astype(BF16)
            g = jnp.dot(h, win_ref[:, 2 * qk + heads * dv + hd * dv:
                                   2 * qk + heads * dv + (hd + 1) * dv],
                        preferred_element_type=F32)
            sg_ref[rs, cs] = jax.nn.silu(g)
        for hd in range(heads):
            ks = slice(hd * dk, (hd + 1) * dk)
            vs = slice(hd * dv, (hd + 1) * dv)
            vh = v_ref[rs, vs]
            scores = lax.dot_general(q_ref[rs, ks], k_ref[rs, ks], (((1,), (1,)), ((), ())),
                                     preferred_element_type=F32)
            scores = (scores * intra_ref[hd]).astype(BF16)
            st = state_ref[hd]
            o = (jnp.dot(scores, vh, preferred_element_type=F32)
                 + jnp.dot(qx_ref[rs, ks], st.astype(BF16), preferred_element_type=F32))
            state_ref[hd] = st * decay[hd] + lax.dot_general(
                kz_ref[rs, ks], vh, (((0,), (0,)), ((), ())), preferred_element_type=F32)
            on = o * lax.rsqrt(jnp.mean(o * o, axis=-1, keepdims=True) + EPS)
            ob_ref[rs, vs] = (on * sg_ref[rs, vs]).astype(BF16)
        out = jnp.dot(ob_ref[rs, :], wout_ref[...], preferred_element_type=F32)
        o_ref[0, rs, :] = x + mod[2:3] * out


def _retention_layer(x, positions, mod, norm_g, w_in, w_out, *, tm=512, rc=256):
    b, s, d = x.shape
    heads = RET_HEADS
    dk = d // heads
    dv = w_out.shape[0] // heads
    qk = heads * dk
    vw = heads * dv
    log_gamma = jnp.log1p(-jnp.exp2(-5.0 - jnp.arange(heads, dtype=F32)))
    idx = jnp.arange(rc, dtype=F32)
    rel = idx[:, None] - idx[None, :]
    intra = jnp.where(rel >= 0, jnp.exp(log_gamma[:, None, None] * jnp.maximum(rel, 0.0)), 0.0)
    xi = jnp.exp(log_gamma[:, None] * (idx + 1.0))
    zeta = jnp.exp(log_gamma[:, None] * (rc - 1.0 - idx))
    xi_t = jnp.repeat(xi.T, dk, axis=1)
    zeta_t = jnp.repeat(zeta.T, dk, axis=1)
    decay = tuple(float(np.exp(np.log1p(-2.0 ** (-5.0 - hd)) * rc)) for hd in range(heads))
    inv_freq = 1.0 / (ROPE_BASE ** (jnp.arange(0, dk, 2, dtype=F32) / dk))
    kern = functools.partial(_retention_kernel, tm=tm, rc=rc, dk=dk, dv=dv, decay=decay)
    return pl.pallas_call(
        kern,
        out_shape=jax.ShapeDtypeStruct((b, s, d), F32),
        grid=(b, s // tm),
        in_specs=[
            pl.BlockSpec((1, tm, d), lambda i, j: (i, j, 0)),
            pl.BlockSpec((1, tm, 1), lambda i, j: (i, j, 0)),
            pl.BlockSpec((1, 6, d), lambda i, j: (i, 0, 0)),
            _const_spec((1, d)),
            _const_spec((1, dk // 2)),
            _const_spec((d, 2 * qk + 2 * vw)),
            _const_spec((vw, d)),
            _const_spec((heads, rc, rc)),
            _const_spec((rc, qk)),
            _const_spec((rc, qk)),
        ],
        out_specs=pl.BlockSpec((1, tm, d), lambda i, j: (i, j, 0)),
        scratch_shapes=[
            pltpu.VMEM((heads, dk, dv), F32),
            pltpu.VMEM((tm, qk), BF16),
            pltpu.VMEM((tm, qk), BF16),
            pltpu.VMEM((tm, qk), BF16),
            pltpu.VMEM((tm, qk), BF16),
            pltpu.VMEM((tm, vw), BF16),
            pltpu.VMEM((tm, vw), F32),
            pltpu.VMEM((tm, vw), BF16),
        ],
        compiler_params=pltpu.CompilerParams(
            dimension_semantics=("parallel", "arbitrary"),
            vmem_limit_bytes=VMEM_LIMIT_BYTES),
        name="retention",
    )(x, positions.reshape(b, s, 1), mod, norm_g.reshape(1, d), inv_freq.reshape(1, dk // 2),
      w_in.astype(BF16), w_out.astype(BF16), intra, xi_t, zeta_t)


META_E1, META_E2, META_R1, META_R2, META_W1, META_W2 = range(6)
META_ROWS = 8


def _router_kernel(x_ref, mod_ref, ng_ref, wr_ref, br_ref, meta_ref, meta_t_ref, cnt_ref,
                   carry_ref):
    @pl.when((pl.program_id(0) == 0) & (pl.program_id(1) == 0))
    def _():
        carry_ref[...] = jnp.zeros(carry_ref.shape, F32)

    x = x_ref[0]
    mod = mod_ref[0]
    h = _modnorm(x, ng_ref[...], mod[4:5], mod[3:4])
    tm = h.shape[0]
    lane = lax.broadcasted_iota(jnp.int32, (tm, LANES), 1)
    neg = jnp.float32(-jnp.inf)
    logits = jnp.full((tm, LANES), neg, F32)
    for e in range(N_EXPERTS):
        le = jnp.sum(h * wr_ref[e:e + 1, :], axis=-1, keepdims=True)
        logits = jnp.where(lane == e, le, logits)
    logits = logits + br_ref[...]
    m1 = jnp.max(logits, axis=-1, keepdims=True)
    i1 = jnp.min(jnp.where(logits == m1, lane, LANES), axis=-1, keepdims=True)
    rest = jnp.where(lane == i1, neg, logits)
    m2 = jnp.max(rest, axis=-1, keepdims=True)
    i2 = jnp.min(jnp.where(rest == m2, lane, LANES), axis=-1, keepdims=True)
    e2 = jnp.exp(m2 - m1)
    w1 = 1.0 / (1.0 + e2)
    w2 = e2 / (1.0 + e2)
    sel = jnp.where((lane == i1) | (lane == i2), 1.0, 0.0)
    row = lax.broadcasted_iota(jnp.int32, (tm, tm), 0)
    col = lax.broadcasted_iota(jnp.int32, (tm, tm), 1)
    tri = jnp.where(row > col, 1.0, 0.0).astype(BF16)
    before = jnp.dot(tri, sel.astype(BF16), preferred_element_type=F32) + carry_ref[...]
    r1 = jnp.sum(jnp.where(lane == i1, before, 0.0), axis=-1, keepdims=True)
    r2 = jnp.sum(jnp.where(lane == i2, before, 0.0), axis=-1, keepdims=True)
    carry_ref[...] += jnp.sum(sel, axis=0, keepdims=True)
    cnt_ref[...] = carry_ref[...]
    meta = jnp.zeros(logits.shape, F32)
    for k, val in ((META_E1, i1.astype(F32)), (META_E2, i2.astype(F32)), (META_R1, r1),
                   (META_R2, r2), (META_W1, w1), (META_W2, w2)):
        meta = jnp.where(lane == k, val, meta)
    meta_ref[0] = meta
    meta_t_ref[...] = meta.T[:META_ROWS, :]


def _router(x, mod, norm_g, w_router, b_router, *, tm=512):
    b, s, d = x.shape
    ne = w_router.shape[1]
    assert ne == N_EXPERTS
    wr = w_router.T
    br = jnp.zeros((1, LANES), F32).at[0, :ne].set(b_router)
    return pl.pallas_call(
        _router_kernel,
        out_shape=(jax.ShapeDtypeStruct((b, s, LANES), F32),
                   jax.ShapeDtypeStruct((META_ROWS, b * s), F32),
                   jax.ShapeDtypeStruct((1, LANES), F32)),
        grid=(b, s // tm),
        in_specs=[
            pl.BlockSpec((1, tm, d), lambda i, j: (i, j, 0)),
            pl.BlockSpec((1, 6, d), lambda i, j: (i, 0, 0)),
            _const_spec((1, d)),
            _const_spec((ne, d)),
            _const_spec((1, LANES)),
        ],
        out_specs=(pl.BlockSpec((1, tm, LANES), lambda i, j: (i, j, 0)),
                   pl.BlockSpec((META_ROWS, tm), lambda i, j: (0, i * (s // tm) + j)),
                   pl.BlockSpec((1, LANES), lambda i, j: (0, 0))),
        scratch_shapes=[pltpu.VMEM((1, LANES), F32)],
        compiler_params=pltpu.CompilerParams(
            dimension_semantics=("arbitrary", "arbitrary"),
            vmem_limit_bytes=VMEM_LIMIT_BYTES),
        name="router",
    )(x, mod, norm_g.reshape(1, d), wr, br)


N_ZERO_SLOTS = 2 * N_EXPERTS


def _dispatch_kernel(info_ref, pos_ref, x_ref, mod_ref, ng_ref, hs_hbm, hbuf, zeros_ref,
                     sem, zsem, *, td, tr):
    i = pl.program_id(0)
    slot = lax.rem(i, 2)

    def zero_copy(k):
        start = pl.multiple_of(info_ref[N_ZERO_SLOTS + k] * (tr * SUBLANES), tr * SUBLANES)
        return pltpu.make_async_copy(zeros_ref, hs_hbm.at[pl.ds(start, tr * SUBLANES)], zsem)

    @pl.when(i == 0)
    def _():
        zeros_ref[...] = jnp.zeros(zeros_ref.shape, F32)
        for k in range(N_ZERO_SLOTS):
            @pl.when(info_ref[k] == 1)
            def _():
                zero_copy(k).start()
        for k in range(N_ZERO_SLOTS):
            @pl.when(info_ref[k] == 1)
            def _():
                zero_copy(k).wait()

    mod = mod_ref[0]
    h = _modnorm(x_ref[...], ng_ref[...], mod[4:5], mod[3:4])
    _store_slabs(hbuf.at[slot], h, 0, td)

    def hs_slab(p8):
        return hs_hbm.at[pl.ds(pl.multiple_of(p8, SUBLANES), SUBLANES)]

    for t in range(td):
        src = hbuf.at[slot, pl.ds(t * SUBLANES, SUBLANES)]
        pltpu.make_async_copy(src, hs_slab(pos_ref[0, 0, t]), sem.at[slot]).start(priority=0)
        pltpu.make_async_copy(src, hs_slab(pos_ref[0, 0, td + t]),
                              sem.at[slot]).start(priority=1)

    def wait_rows(s):
        n_sub = 2 * td * SUBLANES
        pltpu.make_async_copy(hs_hbm.at[pl.ds(0, n_sub)], hs_hbm.at[pl.ds(0, n_sub)],
                              sem.at[s]).wait()

    @pl.when(i > 0)
    def _():
        wait_rows(1 - slot)

    @pl.when(i == pl.num_programs(0) - 1)
    def _():
        wait_rows(slot)


def _dispatch(x, mod, norm_g, pos, zero_info, *, n_rows, seq, td, tr):
    n, d = x.shape
    assert d == SUBLANES * LANES, "a row must fill exactly one (8, 128) slab"
    kern = functools.partial(_dispatch_kernel, td=td, tr=tr)
    return pl.pallas_call(
        kern,
        out_shape=jax.ShapeDtypeStruct((n_rows * SUBLANES, LANES), F32),
        grid_spec=pltpu.PrefetchScalarGridSpec(
            num_scalar_prefetch=1,
            grid=(n // td,),
            in_specs=[
                pl.BlockSpec((1, 1, 2 * td), lambda i, info: (i, 0, 0),
                             memory_space=pltpu.SMEM),
                pl.BlockSpec((td, d), lambda i, info: (i, 0)),
                pl.BlockSpec((1, 6, d), lambda i, info: ((i * td) // seq, 0, 0)),
                pl.BlockSpec((1, d), lambda i, info: (0, 0)),
            ],
            out_specs=pl.BlockSpec(memory_space=pl.ANY),
            scratch_shapes=[pltpu.VMEM((2, td * SUBLANES, LANES), F32),
                            pltpu.VMEM((tr * SUBLANES, LANES), F32),
                            pltpu.SemaphoreType.DMA((2,)), pltpu.SemaphoreType.DMA],
        ),
        compiler_params=pltpu.CompilerParams(
            dimension_semantics=("arbitrary",), vmem_limit_bytes=VMEM_LIMIT_BYTES),
        name="dispatch",
    )(zero_info, pos, x, mod, norm_g.reshape(1, d))


def _experts_kernel(te_ref, nu_ref, hs_ref, wgu_ref, wd_ref, y_ref, *, ffn, col_block, tr):
    i = pl.program_id(0)

    @pl.when(i < nu_ref[0])
    def _():
        h = _load_slabs(hs_ref, 0, tr).astype(BF16)
        y = jnp.zeros((tr, SUBLANES * LANES), F32)
        for j in range(ffn // col_block):
            g = jnp.dot(h, wgu_ref[0, :, j * col_block:(j + 1) * col_block],
                        preferred_element_type=F32)
            u = jnp.dot(h, wgu_ref[0, :, ffn + j * col_block:ffn + (j + 1) * col_block],
                        preferred_element_type=F32)
            a = (jax.nn.silu(g) * u).astype(BF16)
            y = y + jnp.dot(a, wd_ref[0, j * col_block:(j + 1) * col_block, :],
                            preferred_element_type=F32)
        _store_slabs(y_ref, y, 0, tr)

    @pl.when(i >= nu_ref[0])
    def _():
        y_ref[...] = jnp.zeros(y_ref.shape, F32)


def _experts(hs, tile_expert, n_used, w_gate_up, w_down, *, tr):
    ne, ffn, d = w_down.shape
    n_tiles = hs.shape[0] // (tr * SUBLANES)
    kern = functools.partial(_experts_kernel, ffn=ffn, col_block=256, tr=tr)
    return pl.pallas_call(
        kern,
        out_shape=jax.ShapeDtypeStruct(hs.shape, F32),
        grid_spec=pltpu.PrefetchScalarGridSpec(
            num_scalar_prefetch=2,
            grid=(n_tiles,),
            in_specs=[
                pl.BlockSpec((tr * SUBLANES, LANES),
                             lambda i, te, nu: (jnp.minimum(i, nu[0] - 1), 0)),
                pl.BlockSpec((1, d, 2 * ffn), lambda i, te, nu: (te[i], 0, 0)),
                pl.BlockSpec((1, ffn, d), lambda i, te, nu: (te[i], 0, 0)),
            ],
            out_specs=pl.BlockSpec((tr * SUBLANES, LANES), lambda i, te, nu: (i, 0)),
        ),
        compiler_params=pltpu.CompilerParams(
            dimension_semantics=("arbitrary",), vmem_limit_bytes=VMEM_LIMIT_BYTES),
        name="experts",
    )(tile_expert, n_used, hs, w_gate_up.astype(BF16), w_down.astype(BF16))


def _combine_kernel(posc_ref, posn_ref, x_ref, meta_ref, mod_ref, fg_ref, y_hbm, o_ref,
                    ybuf, sem, *, tc):
    i = pl.program_id(0)
    slot = lax.rem(i, 2)

    def issue(pos_ref, to_slot):
        for t in range(2 * tc):
            src = y_hbm.at[pl.ds(pl.multiple_of(pos_ref[0, 0, t], SUBLANES), SUBLANES)]
            pltpu.make_async_copy(src, ybuf.at[to_slot, pl.ds(t * SUBLANES, SUBLANES)],
                                  sem.at[to_slot]).start(priority=t % 2)

    @pl.when(i == 0)
    def _():
        issue(posc_ref, 0)

    @pl.when(i + 1 < pl.num_programs(0))
    def _():
        issue(posn_ref, 1 - slot)

    pltpu.make_async_copy(y_hbm.at[pl.ds(0, 2 * tc * SUBLANES)], ybuf.at[slot],
                          sem.at[slot]).wait()
    meta = meta_ref[...]
    lane = lax.broadcasted_iota(jnp.int32, meta.shape, 1)
    w1 = jnp.sum(jnp.where(lane == META_W1, meta, 0.0), axis=-1, keepdims=True)
    w2 = jnp.sum(jnp.where(lane == META_W2, meta, 0.0), axis=-1, keepdims=True)
    y = w1 * _load_slabs(ybuf.at[slot], 0, tc) + w2 * _load_slabs(ybuf.at[slot], tc, tc)
    xo = x_ref[...] + mod_ref[0][5:6] * y
    ms = jnp.mean(xo * xo, axis=-1, keepdims=True)
    o_ref[...] = xo * lax.rsqrt(ms + EPS) * fg_ref[...]


def _combine(x, meta, mod, final_g, y_sorted, pos, *, seq, tc):
    n, d = x.shape
    nblk = n // tc
    kern = functools.partial(_combine_kernel, tc=tc)
    smem_pos = lambda f: pl.BlockSpec((1, 1, 2 * tc), f, memory_space=pltpu.SMEM)
    return pl.pallas_call(
        kern,
        out_shape=jax.ShapeDtypeStruct((n, d), F32),
        grid=(nblk,),
        in_specs=[
            smem_pos(lambda i: (i, 0, 0)),
            smem_pos(lambda i: (jnp.minimum(i + 1, nblk - 1), 0, 0)),
            pl.BlockSpec((tc, d), lambda i: (i, 0)),
            pl.BlockSpec((tc, LANES), lambda i: (i, 0)),
            pl.BlockSpec((1, 6, d), lambda i: ((i * tc) // seq, 0, 0)),
            _const_spec((1, d)),
            pl.BlockSpec(memory_space=pl.ANY),
        ],
        out_specs=pl.BlockSpec((tc, d), lambda i: (i, 0)),
        scratch_shapes=[pltpu.VMEM((2, 2 * tc * SUBLANES, LANES), F32),
                        pltpu.SemaphoreType.DMA((2,))],
        compiler_params=pltpu.CompilerParams(
            dimension_semantics=("arbitrary",), vmem_limit_bytes=VMEM_LIMIT_BYTES),
        name="combine",
    )(pos, pos, x, meta, mod, final_g.reshape(1, d), y_sorted)


def _moe_layer(x, mod, norm_g, final_g, w_router, b_router, w_gate_up, w_down,
               *, tr=512, td=512, tc=256):
    b, s, d = x.shape
    n = b * s
    ne = w_router.shape[1]
    meta, meta_t, cnt = _router(x, mod, norm_g, w_router, b_router)
    meta = meta.reshape(n, LANES)
    e1 = meta_t[META_E1].astype(jnp.int32)
    e2 = meta_t[META_E2].astype(jnp.int32)
    r1 = meta_t[META_R1].astype(jnp.int32)
    r2 = meta_t[META_R2].astype(jnp.int32)
    counts = cnt[0, :ne].astype(jnp.int32)
    tiles_per = (counts + tr - 1) // tr
    tile_end = jnp.cumsum(tiles_per)
    n_used = tile_end[-1]
    group_start = (tile_end - tiles_per) * tr
    pos1 = group_start[e1] + r1
    pos2 = group_start[e2] + r2
    nt = 2 * n // tr + ne
    tile_ids = jnp.arange(nt, dtype=jnp.int32)
    te = jnp.sum((tile_ids[:, None] >= tile_end[None, :]).astype(jnp.int32), axis=1)
    te = jnp.minimum(te, ne - 1)
    te = jnp.where(tile_ids < n_used, te, te[n_used - 1]).astype(jnp.int32)
    zero_tiles = jnp.concatenate([tile_end - 1, n_used + jnp.arange(ne, dtype=jnp.int32)])
    zero_valid = jnp.concatenate([(counts % tr) != 0, n_used + jnp.arange(ne) < nt])
    zero_info = jnp.concatenate([zero_valid.astype(jnp.int32),
                                 jnp.clip(zero_tiles, 0, nt - 1).astype(jnp.int32)])

    def blocked(t):
        return SUBLANES * jnp.concatenate(
            [pos1.reshape(n // t, 1, t), pos2.reshape(n // t, 1, t)], axis=-1)

    hs = _dispatch(x.reshape(n, d), mod, norm_g, blocked(td), zero_info, n_rows=nt * tr,
                   seq=s, td=td, tr=tr)
    y_sorted = _experts(hs, te, n_used.reshape(1).astype(jnp.int32), w_gate_up, w_down, tr=tr)
    out = _combine(x.reshape(n, d), meta, mod, final_g, y_sorted, blocked(tc), seq=s, tc=tc)
    return out.reshape(b, s, d)


def kernel(x, c, positions, ada_w, ada_b, mix_norm_g, ffn_norm_g, sgu_w_in, sgu_ln_g, sgu_ln_b,
           sgu_w_s, sgu_b_s, sgu_w_out, ffn_w_gate_up, ffn_w_down, ret_w_in, ret_w_out,
           moe_w_router, moe_b_router, moe_w_gate_up, moe_w_down, final_norm_g):
    b, s, d = x.shape
    depth = ada_w.shape[0]
    assert depth == 2, "layer 0 = SGU + SwiGLU, layer 1 = retention + MoE"
    mod = _adaln(c, ada_w, ada_b).reshape(depth, b, 6, d)
    ne, _, two_ffn = moe_w_gate_up[0].shape
    ffn = moe_w_down[0].shape[1]
    later = [ffn_w_gate_up[0], ffn_w_down[0], ret_w_in[0], ret_w_out[0],
             moe_w_gate_up[0].reshape(ne * d, two_ffn), moe_w_down[0].reshape(ne * ffn, d)]
    x, (ffn_gu, ffn_dn, ret_in, ret_out, moe_gu, moe_dn) = _sgu_layer(
        x, mod[0], mix_norm_g[0], sgu_w_in[0], sgu_ln_g[0], sgu_ln_b[0], sgu_w_s[0],
        sgu_b_s[0], sgu_w_out[0], later)
    x = _ffn_layer(x, mod[0], ffn_norm_g[0], ffn_gu, ffn_dn)
    x = _retention_layer(x, positions, mod[1], mix_norm_g[1], ret_in, ret_out)
    return _moe_layer(x, mod[1], ffn_norm_g[1], final_norm_g, moe_w_router[0], moe_b_router[0],
                      moe_gu.reshape(ne, d, two_ffn), moe_dn.reshape(ne, ffn, d))
```

```python
import functools

import jax
import jax.numpy as jnp
import numpy as np
from jax import lax
from jax.experimental import pallas as pl
from jax.experimental.pallas import tpu as pltpu

F32 = jnp.float32
BF16 = jnp.bfloat16

CHUNK = 128
SGU_GROUPS = 8
RET_HEADS = 4
N_EXPERTS = 8
ROPE_BASE = 10000.0
EPS = 1e-6
LANES = 128
SUBLANES = 8
BF16_TILE_ROWS = 16
VMEM_LIMIT_BYTES = 56 * 1024 * 1024


def _const_spec(shape):
    nd = len(shape)
    return pl.BlockSpec(shape, lambda *_: (0,) * nd, pipeline_mode=pl.Buffered(1))


def _store_slabs(ref, rows, first, count):
    for c in range(SUBLANES):
        ref[pl.ds(first * SUBLANES + c, count, stride=SUBLANES), :] = \
            rows[:, c * LANES:(c + 1) * LANES]


def _load_slabs(ref, first, count):
    return jnp.concatenate(
        [ref[pl.ds(first * SUBLANES + c, count, stride=SUBLANES), :] for c in range(SUBLANES)],
        axis=-1)


def _modnorm(x, g, scale, shift):
    ms = jnp.mean(x * x, axis=-1, keepdims=True)
    return (x * lax.rsqrt(ms + EPS) * g) * (1.0 + scale) + shift


def _adaln_kernel(c_ref, w_ref, b_ref, o_ref):
    sc = jax.nn.silu(c_ref[...])
    o_ref[0] = jnp.dot(sc, w_ref[0], precision=lax.Precision.HIGHEST,
                       preferred_element_type=F32) + b_ref[0]


def _adaln(c, ada_w, ada_b):
    depth, d, six_d = ada_w.shape
    b = c.shape[0]
    tn = 1024
    return pl.pallas_call(
        _adaln_kernel,
        out_shape=jax.ShapeDtypeStruct((depth, b, six_d), F32),
        grid=(depth, six_d // tn),
        in_specs=[
            pl.BlockSpec((b, d), lambda l, j: (0, 0)),
            pl.BlockSpec((1, d, tn), lambda l, j: (l, 0, j)),
            pl.BlockSpec((1, 1, tn), lambda l, j: (l, 0, j)),
        ],
        out_specs=pl.BlockSpec((1, b, tn), lambda l, j: (l, 0, j)),
        compiler_params=pltpu.CompilerParams(
            dimension_semantics=("parallel", "parallel"),
            vmem_limit_bytes=VMEM_LIMIT_BYTES),
        name="adaln",
    )(c, ada_w, ada_b.reshape(depth, 1, six_d))


def _sgu_kernel(x_ref, mod_ref, ng_ref, win_ref, lng_ref, lnb_ref, ws_ref, bs_ref,
                wout_ref, *rest, tm, width, col_block, n_cast):
    cast_in = rest[:n_cast]
    o_ref = rest[n_cast]
    cast_out = rest[n_cast + 1:2 * n_cast + 1]
    u_ref, vn_ref, y_ref = rest[2 * n_cast + 1:]
    for src, dst in zip(cast_in, cast_out):
        dst[...] = src[...].astype(BF16)
    x = x_ref[0]
    mod = mod_ref[0]
    h = _modnorm(x, ng_ref[...], mod[1:2], mod[0:1]).astype(BF16)
    nb = width // col_block
    vs = []
    for j in range(2 * nb):
        z = jnp.dot(h, win_ref[:, j * col_block:(j + 1) * col_block],
                    preferred_element_type=F32)
        z = jax.nn.gelu(z)
        if j < nb:
            u_ref[:, j * col_block:(j + 1) * col_block] = z
        else:
            vs.append(z)
    s1 = sum(jnp.sum(v, axis=-1, keepdims=True) for v in vs)
    mu = s1 * (1.0 / width)
    s2 = sum(jnp.sum((v - mu) * (v - mu), axis=-1, keepdims=True) for v in vs)
    rstd = lax.rsqrt(s2 * (1.0 / width) + EPS)
    for j, v in enumerate(vs):
        sl = slice(j * col_block, (j + 1) * col_block)
        vn_ref[:, sl] = ((v - mu) * rstd * lng_ref[:, sl] + lnb_ref[:, sl]).astype(BF16)
    gd = width // SGU_GROUPS
    row = lax.broadcasted_iota(jnp.int32, (CHUNK, CHUNK), 0)
    col = lax.broadcasted_iota(jnp.int32, (CHUNK, CHUNK), 1)
    causal = row >= col
    for g in range(SGU_GROUPS):
        w = jnp.where(causal, ws_ref[g], jnp.zeros((), BF16))
        cs = slice(g * gd, (g + 1) * gd)
        for c in range(tm // CHUNK):
            rs = slice(c * CHUNK, (c + 1) * CHUNK)
            fv = jnp.dot(w, vn_ref[rs, cs], preferred_element_type=F32) + bs_ref[g]
            y_ref[rs, cs] = (u_ref[rs, cs] * fv).astype(BF16)
    out = jnp.dot(y_ref[...], wout_ref[...], preferred_element_type=F32)
    o_ref[0] = x + mod[2:3] * out


def _cast_blocks(rows, steps):
    for br in range(BF16_TILE_ROWS, rows + 1, BF16_TILE_ROWS):
        if rows % br == 0 and rows // br <= steps:
            return br, rows // br
    raise ValueError(f"no bf16-tile row block casts {rows} rows in {steps} steps")


def _sgu_layer(x, mod, norm_g, w_in, ln_g, ln_b, w_s, b_s, w_out, later_weights, *, tm=512):
    b, s, d = x.shape
    width = w_out.shape[0]
    gd = width // SGU_GROUPS
    bs_full = jnp.broadcast_to(b_s[:, :, None], (SGU_GROUPS, CHUNK, gd))
    steps_j = s // tm
    plans = [_cast_blocks(w.shape[0], b * steps_j) for w in later_weights]

    def cast_spec(w, plan):
        br, nblk = plan
        return pl.BlockSpec((br, w.shape[1]),
                            lambda i, j: (jnp.minimum(i * steps_j + j, nblk - 1), 0))

    cast_specs = [cast_spec(w, p) for w, p in zip(later_weights, plans)]
    kern = functools.partial(_sgu_kernel, tm=tm, width=width, col_block=512,
                             n_cast=len(later_weights))
    outs = pl.pallas_call(
        kern,
        out_shape=[jax.ShapeDtypeStruct((b, s, d), F32)]
        + [jax.ShapeDtypeStruct(w.shape, BF16) for w in later_weights],
        grid=(b, steps_j),
        in_specs=[
            pl.BlockSpec((1, tm, d), lambda i, j: (i, j, 0)),
            pl.BlockSpec((1, 6, d), lambda i, j: (i, 0, 0)),
            _const_spec((1, d)),
            _const_spec((d, 2 * width)),
            _const_spec((1, width)),
            _const_spec((1, width)),
            _const_spec((SGU_GROUPS, CHUNK, CHUNK)),
            _const_spec((SGU_GROUPS, CHUNK, gd)),
            _const_spec((width, d)),
        ] + cast_specs,
        out_specs=[pl.BlockSpec((1, tm, d), lambda i, j: (i, j, 0))] + cast_specs,
        scratch_shapes=[
            pltpu.VMEM((tm, width), F32),
            pltpu.VMEM((tm, width), BF16),
            pltpu.VMEM((tm, width), BF16),
        ],
        compiler_params=pltpu.CompilerParams(
            dimension_semantics=("arbitrary", "arbitrary"),
            vmem_limit_bytes=VMEM_LIMIT_BYTES),
        name="sgu",
    )(x, mod, norm_g.reshape(1, d), w_in.astype(BF16), ln_g.reshape(1, width),
      ln_b.reshape(1, width), w_s.astype(BF16), bs_full, w_out.astype(BF16), *later_weights)
    return outs[0], outs[1:]


def _ffn_kernel(x_ref, mod_ref, ng_ref, wgu_ref, wd_ref, o_ref, *, ffn, col_block):
    x = x_ref[0]
    mod = mod_ref[0]
    h = _modnorm(x, ng_ref[...], mod[4:5], mod[3:4]).astype(BF16)
    acc = jnp.zeros(x.shape, F32)
    for j in range(ffn // col_block):
        g = jnp.dot(h, wgu_ref[:, j * col_block:(j + 1) * col_block],
                    preferred_element_type=F32)
        u = jnp.dot(h, wgu_ref[:, ffn + j * col_block:ffn + (j + 1) * col_block],
                    preferred_element_type=F32)
        a = (jax.nn.silu(g) * u).astype(BF16)
        acc = acc + jnp.dot(a, wd_ref[j * col_block:(j + 1) * col_block, :],
                            preferred_element_type=F32)
    o_ref[0] = x + mod[5:6] * acc


def _ffn_layer(x, mod, norm_g, w_gate_up, w_down, *, tm=1024):
    b, s, d = x.shape
    ffn = w_down.shape[0]
    kern = functools.partial(_ffn_kernel, ffn=ffn, col_block=256)
    return pl.pallas_call(
        kern,
        out_shape=jax.ShapeDtypeStruct((b, s, d), F32),
        grid=(b, s // tm),
        in_specs=[
            pl.BlockSpec((1, tm, d), lambda i, j: (i, j, 0)),
            pl.BlockSpec((1, 6, d), lambda i, j: (i, 0, 0)),
            _const_spec((1, d)),
            _const_spec((d, 2 * ffn)),
            _const_spec((ffn, d)),
        ],
        out_specs=pl.BlockSpec((1, tm, d), lambda i, j: (i, j, 0)),
        compiler_params=pltpu.CompilerParams(
            dimension_semantics=("parallel", "parallel"),
            vmem_limit_bytes=VMEM_LIMIT_BYTES),
        name="ffn",
    )(x, mod, norm_g.reshape(1, d), w_gate_up.astype(BF16), w_down.astype(BF16))


def _retention_kernel(x_ref, pos_ref, mod_ref, ng_ref, invf_ref, win_ref, wout_ref,
                      intra_ref, xi_ref, zeta_ref, o_ref,
                      state_ref, q_ref, qx_ref, k_ref, kz_ref, v_ref, sg_ref, ob_ref,
                      *, tm, rc, dk, dv, decay):
    heads = RET_HEADS
    qk = heads * dk

    @pl.when(pl.program_id(1) == 0)
    def _():
        state_ref[...] = jnp.zeros(state_ref.shape, F32)

    mod = mod_ref[0]
    half = dk // 2

    def rope(p, cos, sin):
        t1, t2 = p[:, :half], p[:, half:]
        return jnp.concatenate([t1 * cos - t2 * sin, t2 * cos + t1 * sin], axis=-1)

    for c in range(tm // rc):
        rs = slice(c * rc, (c + 1) * rc)
        x = x_ref[0, rs, :]
        h = _modnorm(x, ng_ref[...], mod[1:2], mod[0:1]).astype(BF16)
        ang = pos_ref[0, rs, :].astype(F32) * invf_ref[...]
        cos = jnp.cos(ang)
        sin = jnp.sin(ang)
        for hd in range(heads):
            cs = slice(hd * dk, (hd + 1) * dk)
            q = rope(jnp.dot(h, win_ref[:, cs], preferred_element_type=F32), cos, sin)
            q_ref[rs, cs] = q.astype(BF16)
            qx_ref[rs, cs] = (q * xi_ref[:, cs]).astype(BF16)
            k = rope(jnp.dot(h, win_ref[:, qk + hd * dk:qk + (hd + 1) * dk],
                             preferred_element_type=F32), cos, sin) * (dk ** -0.5)
            k_ref[rs, cs] = k.astype(BF16)
            kz_ref[rs, cs] = (k * zeta_ref[:, cs]).astype(BF16)
        for hd in range(heads):
            cs = slice(hd * dv, (hd + 1) * dv)
            v_ref[rs, cs] = jnp.dot(h, win_ref[:, 2 * qk + hd * dv:2 * qk + (hd + 1) * dv],
                                    preferred_element_type=F32).astype(BF16)
            g = jnp.dot(h, win_ref[:, 2 * qk + heads * dv + hd * dv:
                                   2 * qk + heads * dv + (hd + 1) * dv],
                        preferred_element_type=F32)
            sg_ref[rs, cs] = jax.nn.silu(g)
        for hd in range(heads):
            ks = slice(hd * dk, (hd + 1) * dk)
            vs = slice(hd * dv, (hd + 1) * dv)
            vh = v_ref[rs, vs]
            scores = lax.dot_general(q_ref[rs, ks], k_ref[rs, ks], (((1,), (1,)), ((), ())),
                                     preferred_element_type=F32)
            scores = (scores * intra_ref[hd]).astype(BF16)
            st = state_ref[hd]
            o = (jnp.dot(scores, vh, preferred_element_type=F32)
                 + jnp.dot(qx_ref[rs, ks], st.astype(BF16), preferred_element_type=F32))
            state_ref[hd] = st * decay[hd] + lax.dot_general(
                kz_ref[rs, ks], vh, (((0,), (0,)), ((), ())), preferred_element_type=F32)
            on = o * lax.rsqrt(jnp.mean(o * o, axis=-1, keepdims=True) + EPS)
            ob_ref[rs, vs] = (on * sg_ref[rs, vs]).astype(BF16)
        out = jnp.dot(ob_ref[rs, :], wout_ref[...], preferred_element_type=F32)
        o_ref[0, rs, :] = x + mod[2:3] * out


def _retention_layer(x, positions, mod, norm_g, w_in, w_out, *, tm=512, rc=256):
    b, s, d = x.shape
    heads = RET_HEADS
    dk = d // heads
    dv = w_out.shape[0] // heads
    qk = heads * dk
    vw = heads * dv
    log_gamma = jnp.log1p(-jnp.exp2(-5.0 - jnp.arange(heads, dtype=F32)))
    idx = jnp.arange(rc, dtype=F32)
    rel = idx[:, None] - idx[None, :]
    intra = jnp.where(rel >= 0, jnp.exp(log_gamma[:, None, None] * jnp.maximum(rel, 0.0)), 0.0)
    xi = jnp.exp(log_gamma[:, None] * (idx + 1.0))
    zeta = jnp.exp(log_gamma[:, None] * (rc - 1.0 - idx))
    xi_t = jnp.repeat(xi.T, dk, axis=1)
    zeta_t = jnp.repeat(zeta.T, dk, axis=1)
    decay = tuple(float(np.exp(np.log1p(-2.0 ** (-5.0 - hd)) * rc)) for hd in range(heads))
    inv_freq = 1.0 / (ROPE_BASE ** (jnp.arange(0, dk, 2, dtype=F32) / dk))
    kern = functools.partial(_retention_kernel, tm=tm, rc=rc, dk=dk, dv=dv, decay=decay)
    return pl.pallas_call(
        kern,
        out_shape=jax.ShapeDtypeStruct((b, s, d), F32),
        grid=(b, s // tm),
        in_specs=[
            pl.BlockSpec((1, tm, d), lambda i, j: (i, j, 0)),
            pl.BlockSpec((1, tm, 1), lambda i, j: (i, j, 0)),
            pl.BlockSpec((1, 6, d), lambda i, j: (i, 0, 0)),
            _const_spec((1, d)),
            _const_spec((1, dk // 2)),
            _const_spec((d, 2 * qk + 2 * vw)),
            _const_spec((vw, d)),
            _const_spec((heads, rc, rc)),
            _const_spec((rc, qk)),
            _const_spec((rc, qk)),
        ],
        out_specs=pl.BlockSpec((1, tm, d), lambda i, j: (i, j, 0)),
        scratch_shapes=[
            pltpu.VMEM((heads, dk, dv), F32),
            pltpu.VMEM((tm, qk), BF16),
            pltpu.VMEM((tm, qk), BF16),
            pltpu.VMEM((tm, qk), BF16),
            pltpu.VMEM((tm, qk), BF16),
            pltpu.VMEM((tm, vw), BF16),
            pltpu.VMEM((tm, vw), F32),
            pltpu.VMEM((tm, vw), BF16),
        ],
        compiler_params=pltpu.CompilerParams(
            dimension_semantics=("parallel", "arbitrary"),
            vmem_limit_bytes=VMEM_LIMIT_BYTES),
        name="retention",
    )(x, positions.reshape(b, s, 1), mod, norm_g.reshape(1, d), inv_freq.reshape(1, dk // 2),
      w_in.astype(BF16), w_out.astype(BF16), intra, xi_t, zeta_t)


META_E1, META_E2, META_R1, META_R2, META_W1, META_W2 = range(6)
META_ROWS = 8


def _router_kernel(x_ref, mod_ref, ng_ref, wr_ref, br_ref, meta_ref, meta_t_ref, cnt_ref,
                   carry_ref):
    @pl.when((pl.program_id(0) == 0) & (pl.program_id(1) == 0))
    def _():
        carry_ref[...] = jnp.zeros(carry_ref.shape, F32)

    x = x_ref[0]
    mod = mod_ref[0]
    h = _modnorm(x, ng_ref[...], mod[4:5], mod[3:4])
    tm = h.shape[0]
    lane = lax.broadcasted_iota(jnp.int32, (tm, LANES), 1)
    neg = jnp.float32(-jnp.inf)
    logits = jnp.full((tm, LANES), neg, F32)
    for e in range(N_EXPERTS):
        le = jnp.sum(h * wr_ref[e:e + 1, :], axis=-1, keepdims=True)
        logits = jnp.where(lane == e, le, logits)
    logits = logits + br_ref[...]
    m1 = jnp.max(logits, axis=-1, keepdims=True)
    i1 = jnp.min(jnp.where(logits == m1, lane, LANES), axis=-1, keepdims=True)
    rest = jnp.where(lane == i1, neg, logits)
    m2 = jnp.max(rest, axis=-1, keepdims=True)
    i2 = jnp.min(jnp.where(rest == m2, lane, LANES), axis=-1, keepdims=True)
    e2 = jnp.exp(m2 - m1)
    w1 = 1.0 / (1.0 + e2)
    w2 = e2 / (1.0 + e2)
    sel = jnp.where((lane == i1) | (lane == i2), 1.0, 0.0)
    row = lax.broadcasted_iota(jnp.int32, (tm, tm), 0)
    col = lax.broadcasted_iota(jnp.int32, (tm, tm), 1)
    tri = jnp.where(row > col, 1.0, 0.0).astype(BF16)
    before = jnp.dot(tri, sel.astype(BF16), preferred_element_type=F32) + carry_ref[...]
    r1 = jnp.sum(jnp.where(lane == i1, before, 0.0), axis=-1, keepdims=True)
    r2 = jnp.sum(jnp.where(lane == i2, before, 0.0), axis=-1, keepdims=True)
    carry_ref[...] += jnp.sum(sel, axis=0, keepdims=True)
    cnt_ref[...] = carry_ref[...]
    meta = jnp.zeros(logits.shape, F32)
    for k, val in ((META_E1, i1.astype(F32)), (META_E2, i2.astype(F32)), (META_R1, r1),
                   (META_R2, r2), (META_W1, w1), (META_W2, w2)):
        meta = jnp.where(lane == k, val, meta)
    meta_ref[0] = meta
    meta_t_ref[...] = meta.T[:META_ROWS, :]


def _router(x, mod, norm_g, w_router, b_router, *, tm=512):
    b, s, d = x.shape
    ne = w_router.shape[1]
    assert ne == N_EXPERTS
    wr = w_router.T
    br = jnp.zeros((1, LANES), F32).at[0, :ne].set(b_router)
    return pl.pallas_call(
        _router_kernel,
        out_shape=(jax.ShapeDtypeStruct((b, s, LANES), F32),
                   jax.ShapeDtypeStruct((META_ROWS, b * s), F32),
                   jax.ShapeDtypeStruct((1, LANES), F32)),
        grid=(b, s // tm),
        in_specs=[
            pl.BlockSpec((1, tm, d), lambda i, j: (i, j, 0)),
            pl.BlockSpec((1, 6, d), lambda i, j: (i, 0, 0)),
            _const_spec((1, d)),
            _const_spec((ne, d)),
            _const_spec((1, LANES)),
        ],
        out_specs=(pl.BlockSpec((1, tm, LANES), lambda i, j: (i, j, 0)),
                   pl.BlockSpec((META_ROWS, tm), lambda i, j: (0, i * (s // tm) + j)),
                   pl.BlockSpec((1, LANES), lambda i, j: (0, 0))),
        scratch_shapes=[pltpu.VMEM((1, LANES), F32)],
        compiler_params=pltpu.CompilerParams(
            dimension_semantics=("arbitrary", "arbitrary"),
            vmem_limit_bytes=VMEM_LIMIT_BYTES),
        name="router",
    )(x, mod, norm_g.reshape(1, d), wr, br)


N_ZERO_SLOTS = 2 * N_EXPERTS


def _dispatch_kernel(info_ref, pos_ref, x_ref, mod_ref, ng_ref, hs_hbm, hbuf, zeros_ref,
                     sem, zsem, *, td, tr):
    i = pl.program_id(0)
    slot = lax.rem(i, 2)

    def zero_copy(k):
        start = pl.multiple_of(info_ref[N_ZERO_SLOTS + k] * (tr * SUBLANES), tr * SUBLANES)
        return pltpu.make_async_copy(zeros_ref, hs_hbm.at[pl.ds(start, tr * SUBLANES)], zsem)

    @pl.when(i == 0)
    def _():
        zeros_ref[...] = jnp.zeros(zeros_ref.shape, F32)
        for k in range(N_ZERO_SLOTS):
            @pl.when(info_ref[k] == 1)
            def _():
                zero_copy(k).start()
        for k in range(N_ZERO_SLOTS):
            @pl.when(info_ref[k] == 1)
            def _():
                zero_copy(k).wait()

    mod = mod_ref[0]
    h = _modnorm(x_ref[...], ng_ref[...], mod[4:5], mod[3:4])
    _store_slabs(hbuf.at[slot], h, 0, td)

    def hs_slab(p8):
        return hs_hbm.at[pl.ds(pl.multiple_of(p8, SUBLANES), SUBLANES)]

    for t in range(td):
        src = hbuf.at[slot, pl.ds(t * SUBLANES, SUBLANES)]
        pltpu.make_async_copy(src, hs_slab(pos_ref[0, 0, t]), sem.at[slot]).start(priority=0)
        pltpu.make_async_copy(src, hs_slab(pos_ref[0, 0, td + t]),
                              sem.at[slot]).start(priority=1)

    def wait_rows(s):
        n_sub = 2 * td * SUBLANES
        pltpu.make_async_copy(hs_hbm.at[pl.ds(0, n_sub)], hs_hbm.at[pl.ds(0, n_sub)],
                              sem.at[s]).wait()

    @pl.when(i > 0)
    def _():
        wait_rows(1 - slot)

    @pl.when(i == pl.num_programs(0) - 1)
    def _():
        wait_rows(slot)


def _dispatch(x, mod, norm_g, pos, zero_info, *, n_rows, seq, td, tr):
    n, d = x.shape
    assert d == SUBLANES * LANES, "a row must fill exactly one (8, 128) slab"
    kern = functools.partial(_dispatch_kernel, td=td, tr=tr)
    return pl.pallas_call(
        kern,
        out_shape=jax.ShapeDtypeStruct((n_rows * SUBLANES, LANES), F32),
        grid_spec=pltpu.PrefetchScalarGridSpec(
            num_scalar_prefetch=1,
            grid=(n // td,),
            in_specs=[
                pl.BlockSpec((1, 1, 2 * td), lambda i, info: (i, 0, 0),
                             memory_space=pltpu.SMEM),
                pl.BlockSpec((td, d), lambda i, info: (i, 0)),
                pl.BlockSpec((1, 6, d), lambda i, info: ((i * td) // seq, 0, 0)),
                pl.BlockSpec((1, d), lambda i, info: (0, 0)),
            ],
            out_specs=pl.BlockSpec(memory_space=pl.ANY),
            scratch_shapes=[pltpu.VMEM((2, td * SUBLANES, LANES), F32),
                            pltpu.VMEM((tr * SUBLANES, LANES), F32),
                            pltpu.SemaphoreType.DMA((2,)), pltpu.SemaphoreType.DMA],
        ),
        compiler_params=pltpu.CompilerParams(
            dimension_semantics=("arbitrary",), vmem_limit_bytes=VMEM_LIMIT_BYTES),
        name="dispatch",
    )(zero_info, pos, x, mod, norm_g.reshape(1, d))


def _experts_kernel(te_ref, nu_ref, hs_ref, wgu_ref, wd_ref, y_ref, *, ffn, col_block, tr):
    i = pl.program_id(0)

    @pl.when(i < nu_ref[0])
    def _():
        h = _load_slabs(hs_ref, 0, tr).astype(BF16)
        y = jnp.zeros((tr, SUBLANES * LANES), F32)
        for j in range(ffn // col_block):
            g = jnp.dot(h, wgu_ref[0, :, j * col_block:(j + 1) * col_block],
                        preferred_element_type=F32)
            u = jnp.dot(h, wgu_ref[0, :, ffn + j * col_block:ffn + (j + 1) * col_block],
                        preferred_element_type=F32)
            a = (jax.nn.silu(g) * u).astype(BF16)
            y = y + jnp.dot(a, wd_ref[0, j * col_block:(j + 1) * col_block, :],
                            preferred_element_type=F32)
        _store_slabs(y_ref, y, 0, tr)

    @pl.when(i >= nu_ref[0])
    def _():
        y_ref[...] = jnp.zeros(y_ref.shape, F32)


def _experts(hs, tile_expert, n_used, w_gate_up, w_down, *, tr):
    ne, ffn, d = w_down.shape
    n_tiles = hs.shape[0] // (tr * SUBLANES)
    kern = functools.partial(_experts_kernel, ffn=ffn, col_block=256, tr=tr)
    return pl.pallas_call(
        kern,
        out_shape=jax.ShapeDtypeStruct(hs.shape, F32),
        grid_spec=pltpu.PrefetchScalarGridSpec(
            num_scalar_prefetch=2,
            grid=(n_tiles,),
            in_specs=[
                pl.BlockSpec((tr * SUBLANES, LANES),
                             lambda i, te, nu: (jnp.minimum(i, nu[0] - 1), 0)),
                pl.BlockSpec((1, d, 2 * ffn), lambda i, te, nu: (te[i], 0, 0)),
                pl.BlockSpec((1, ffn, d), lambda i, te, nu: (te[i], 0, 0)),
            ],
            out_specs=pl.BlockSpec((tr * SUBLANES, LANES), lambda i, te, nu: (i, 0)),
        ),
        compiler_params=pltpu.CompilerParams(
            dimension_semantics=("arbitrary",), vmem_limit_bytes=VMEM_LIMIT_BYTES),
        name="experts",
    )(tile_expert, n_used, hs, w_gate_up.astype(BF16), w_down.astype(BF16))


def _combine_kernel(posc_ref, posn_ref, x_ref, meta_ref, mod_ref, fg_ref, y_hbm, o_ref,
                    ybuf, sem, *, tc):
    i = pl.program_id(0)
    slot = lax.rem(i, 2)

    def issue(pos_ref, to_slot):
        for t in range(2 * tc):
            src = y_hbm.at[pl.ds(pl.multiple_of(pos_ref[0, 0, t], SUBLANES), SUBLANES)]
            pltpu.make_async_copy(src, ybuf.at[to_slot, pl.ds(t * SUBLANES, SUBLANES)],
                                  sem.at[to_slot]).start(priority=t % 2)

    @pl.when(i == 0)
    def _():
        issue(posc_ref, 0)

    @pl.when(i + 1 < pl.num_programs(0))
    def _():
        issue(posn_ref, 1 - slot)

    pltpu.make_async_copy(y_hbm.at[pl.ds(0, 2 * tc * SUBLANES)], ybuf.at[slot],
                          sem.at[slot]).wait()
    meta = meta_ref[...]
    lane = lax.broadcasted_iota(jnp.int32, meta.shape, 1)
    w1 = jnp.sum(jnp.where(lane == META_W1, meta, 0.0), axis=-1, keepdims=True)
    w2 = jnp.sum(jnp.where(lane == META_W2, meta, 0.0), axis=-1, keepdims=True)
    y = w1 * _load_slabs(ybuf.at[slot], 0, tc) + w2 * _load_slabs(ybuf.at[slot], tc, tc)
    xo = x_ref[...] + mod_ref[0][5:6] * y
    ms = jnp.mean(xo * xo, axis=-1, keepdims=True)
    o_ref[...] = xo * lax.rsqrt(ms + EPS) * fg_ref[...]


def _combine(x, meta, mod, final_g, y_sorted, pos, *, seq, tc):
    n, d = x.shape
    nblk = n // tc
    kern = functools.partial(_combine_kernel, tc=tc)
    smem_pos = lambda f: pl.BlockSpec((1, 1, 2 * tc), f, memory_space=pltpu.SMEM)
    return pl.pallas_call(
        kern,
        out_shape=jax.ShapeDtypeStruct((n, d), F32),
        grid=(nblk,),
        in_specs=[
            smem_pos(lambda i: (i, 0, 0)),
            smem_pos(lambda i: (jnp.minimum(i + 1, nblk - 1), 0, 0)),
            pl.BlockSpec((tc, d), lambda i: (i, 0)),
            pl.BlockSpec((tc, LANES), lambda i: (i, 0)),
            pl.BlockSpec((1, 6, d), lambda i: ((i * tc) // seq, 0, 0)),
            _const_spec((1, d)),
            pl.BlockSpec(memory_space=pl.ANY),
        ],
        out_specs=pl.BlockSpec((tc, d), lambda i: (i, 0)),
        scratch_shapes=[pltpu.VMEM((2, 2 * tc * SUBLANES, LANES), F32),
                        pltpu.SemaphoreType.DMA((2,))],
        compiler_params=pltpu.CompilerParams(
            dimension_semantics=("arbitrary",), vmem_limit_bytes=VMEM_LIMIT_BYTES),
        name="combine",
    )(pos, pos, x, meta, mod, final_g.reshape(1, d), y_sorted)


def _moe_layer(x, mod, norm_g, final_g, w_router, b_router, w_gate_up, w_down,
               *, tr=512, td=1024, tc=512):
    b, s, d = x.shape
    n = b * s
    ne = w_router.shape[1]
    meta, meta_t, cnt = _router(x, mod, norm_g, w_router, b_router)
    meta = meta.reshape(n, LANES)
    e1 = meta_t[META_E1].astype(jnp.int32)
    e2 = meta_t[META_E2].astype(jnp.int32)
    r1 = meta_t[META_R1].astype(jnp.int32)
    r2 = meta_t[META_R2].astype(jnp.int32)
    counts = cnt[0, :ne].astype(jnp.int32)
    tiles_per = (counts + tr - 1) // tr
    tile_end = jnp.cumsum(tiles_per)
    n_used = tile_end[-1]
    group_start = (tile_end - tiles_per) * tr
    pos1 = group_start[e1] + r1
    pos2 = group_start[e2] + r2
    nt = 2 * n // tr + ne
    tile_ids = jnp.arange(nt, dtype=jnp.int32)
    te = jnp.sum((tile_ids[:, None] >= tile_end[None, :]).astype(jnp.int32), axis=1)
    te = jnp.minimum(te, ne - 1)
    te = jnp.where(tile_ids < n_used, te, te[n_used - 1]).astype(jnp.int32)
    zero_tiles = jnp.concatenate([tile_end - 1, n_used + jnp.arange(ne, dtype=jnp.int32)])
    zero_valid = jnp.concatenate([(counts % tr) != 0, n_used + jnp.arange(ne) < nt])
    zero_info = jnp.concatenate([zero_valid.astype(jnp.int32),
                                 jnp.clip(zero_tiles, 0, nt - 1).astype(jnp.int32)])

    def blocked(t):
        return SUBLANES * jnp.concatenate(
            [pos1.reshape(n // t, 1, t), pos2.reshape(n // t, 1, t)], axis=-1)

    hs = _dispatch(x.reshape(n, d), mod, norm_g, blocked(td), zero_info, n_rows=nt * tr,
                   seq=s, td=td, tr=tr)
    y_sorted = _experts(hs, te, n_used.reshape(1).astype(jnp.int32), w_gate_up, w_down, tr=tr)
    out = _combine(x.reshape(n, d), meta, mod, final_g, y_sorted, blocked(tc), seq=s, tc=tc)
    return out.reshape(b, s, d)


def kernel(x, c, positions, ada_w, ada_b, mix_norm_g, ffn_norm_g, sgu_w_in, sgu_ln_g, sgu_ln_b,
           sgu_w_s, sgu_b_s, sgu_w_out, ffn_w_gate_up, ffn_w_down, ret_w_in, ret_w_out,
           moe_w_router, moe_b_router, moe_w_gate_up, moe_w_down, final_norm_g):
    b, s, d = x.shape
    depth = ada_w.shape[0]
    assert depth == 2, "layer 0 = SGU + SwiGLU, layer 1 = retention + MoE"
    mod = _adaln(c, ada_w, ada_b).reshape(depth, b, 6, d)
    ne, _, two_ffn = moe_w_gate_up[0].shape
    ffn = moe_w_down[0].shape[1]
    later = [ffn_w_gate_up[0], ffn_w_down[0], ret_w_in[0], ret_w_out[0],
             moe_w_gate_up[0].reshape(ne * d, two_ffn), moe_w_down[0].reshape(ne * ffn, d)]
    x, (ffn_gu, ffn_dn, ret_in, ret_out, moe_gu, moe_dn) = _sgu_layer(
        x, mod[0], mix_norm_g[0], sgu_w_in[0], sgu_ln_g[0], sgu_ln_b[0], sgu_w_s[0],
        sgu_b_s[0], sgu_w_out[0], later)
    x = _ffn_layer(x, mod[0], ffn_norm_g[0], ffn_gu, ffn_dn)
    x = _retention_layer(x, positions, mod[1], mix_norm_g[1], ret_in, ret_out)
    return _moe_layer(x, mod[1], ffn_norm_g[1], final_norm_g, moe_w_router[0], moe_b_router[0],
                      moe_gu.reshape(ne, d, two_ffn), moe_dn.reshape(ne, ffn, d))
```

```python
import functools

import jax
import jax.numpy as jnp
import numpy as np
from jax import lax
from jax.experimental import pallas as pl
from jax.experimental.pallas import tpu as pltpu

F32 = jnp.float32
BF16 = jnp.bfloat16

CHUNK = 128
SGU_GROUPS = 8
RET_HEADS = 4
N_EXPERTS = 8
ROPE_BASE = 10000.0
EPS = 1e-6
LANES = 128
SUBLANES = 8
BF16_TILE_ROWS = 16
VMEM_LIMIT_BYTES = 56 * 1024 * 1024


def _const_spec(shape):
    nd = len(shape)
    return pl.BlockSpec(shape, lambda *_: (0,) * nd, pipeline_mode=pl.Buffered(1))


def _store_slabs(ref, rows, first, count):
    for c in range(SUBLANES):
        ref[pl.ds(first * SUBLANES + c, count, stride=SUBLANES), :] = \
            rows[:, c * LANES:(c + 1) * LANES]


def _load_slabs(ref, first, count):
    return jnp.concatenate(
        [ref[pl.ds(first * SUBLANES + c, count, stride=SUBLANES), :] for c in range(SUBLANES)],
        axis=-1)


def _modnorm(x, g, scale, shift):
    ms = jnp.mean(x * x, axis=-1, keepdims=True)
    return (x * lax.rsqrt(ms + EPS) * g) * (1.0 + scale) + shift


def _adaln_kernel(c_ref, w_ref, b_ref, o_ref):
    sc = jax.nn.silu(c_ref[...])
    o_ref[0] = jnp.dot(sc, w_ref[0], precision=lax.Precision.HIGHEST,
                       preferred_element_type=F32) + b_ref[0]


def _adaln(c, ada_w, ada_b):
    depth, d, six_d = ada_w.shape
    b = c.shape[0]
    tn = 1024
    return pl.pallas_call(
        _adaln_kernel,
        out_shape=jax.ShapeDtypeStruct((depth, b, six_d), F32),
        grid=(depth, six_d // tn),
        in_specs=[
            pl.BlockSpec((b, d), lambda l, j: (0, 0)),
            pl.BlockSpec((1, d, tn), lambda l, j: (l, 0, j)),
            pl.BlockSpec((1, 1, tn), lambda l, j: (l, 0, j)),
        ],
        out_specs=pl.BlockSpec((1, b, tn), lambda l, j: (l, 0, j)),
        compiler_params=pltpu.CompilerParams(
            dimension_semantics=("parallel", "parallel"),
            vmem_limit_bytes=VMEM_LIMIT_BYTES),
        name="adaln",
    )(c, ada_w, ada_b.reshape(depth, 1, six_d))


def _sgu_kernel(x_ref, mod_ref, ng_ref, win_ref, lng_ref, lnb_ref, ws_ref, bs_ref,
                wout_ref, *rest, tm, width, col_block, n_cast):
    cast_in = rest[:n_cast]
    o_ref = rest[n_cast]
    cast_out = rest[n_cast + 1:2 * n_cast + 1]
    u_ref, vn_ref, y_ref = rest[2 * n_cast + 1:]
    for src, dst in zip(cast_in, cast_out):
        dst[...] = src[...].astype(BF16)
    x = x_ref[0]
    mod = mod_ref[0]
    h = _modnorm(x, ng_ref[...], mod[1:2], mod[0:1]).astype(BF16)
    nb = width // col_block
    vs = []
    for j in range(2 * nb):
        z = jnp.dot(h, win_ref[:, j * col_block:(j + 1) * col_block],
                    preferred_element_type=F32)
        z = jax.nn.gelu(z)
        if j < nb:
            u_ref[:, j * col_block:(j + 1) * col_block] = z
        else:
            vs.append(z)
    s1 = sum(jnp.sum(v, axis=-1, keepdims=True) for v in vs)
    mu = s1 * (1.0 / width)
    s2 = sum(jnp.sum((v - mu) * (v - mu), axis=-1, keepdims=True) for v in vs)
    rstd = lax.rsqrt(s2 * (1.0 / width) + EPS)
    for j, v in enumerate(vs):
        sl = slice(j * col_block, (j + 1) * col_block)
        vn_ref[:, sl] = ((v - mu) * rstd * lng_ref[:, sl] + lnb_ref[:, sl]).astype(BF16)
    gd = width // SGU_GROUPS
    row = lax.broadcasted_iota(jnp.int32, (CHUNK, CHUNK), 0)
    col = lax.broadcasted_iota(jnp.int32, (CHUNK, CHUNK), 1)
    causal = row >= col
    for g in range(SGU_GROUPS):
        w = jnp.where(causal, ws_ref[g], jnp.zeros((), BF16))
        cs = slice(g * gd, (g + 1) * gd)
        for c in range(tm // CHUNK):
            rs = slice(c * CHUNK, (c + 1) * CHUNK)
            fv = jnp.dot(w, vn_ref[rs, cs], preferred_element_type=F32) + bs_ref[g]
            y_ref[rs, cs] = (u_ref[rs, cs] * fv).astype(BF16)
    out = jnp.dot(y_ref[...], wout_ref[...], preferred_element_type=F32)
    o_ref[0] = x + mod[2:3] * out


def _cast_blocks(rows, steps):
    for br in range(BF16_TILE_ROWS, rows + 1, BF16_TILE_ROWS):
        if rows % br == 0 and rows // br <= steps:
            return br, rows // br
    raise ValueError(f"no bf16-tile row block casts {rows} rows in {steps} steps")


def _sgu_layer(x, mod, norm_g, w_in, ln_g, ln_b, w_s, b_s, w_out, later_weights, *, tm=512):
    b, s, d = x.shape
    width = w_out.shape[0]
    gd = width // SGU_GROUPS
    bs_full = jnp.broadcast_to(b_s[:, :, None], (SGU_GROUPS, CHUNK, gd))
    steps_j = s // tm
    plans = [_cast_blocks(w.shape[0], b * steps_j) for w in later_weights]

    def cast_spec(w, plan):
        br, nblk = plan
        return pl.BlockSpec((br, w.shape[1]),
                            lambda i, j: (jnp.minimum(i * steps_j + j, nblk - 1), 0))

    cast_specs = [cast_spec(w, p) for w, p in zip(later_weights, plans)]
    kern = functools.partial(_sgu_kernel, tm=tm, width=width, col_block=512,
                             n_cast=len(later_weights))
    outs = pl.pallas_call(
        kern,
        out_shape=[jax.ShapeDtypeStruct((b, s, d), F32)]
        + [jax.ShapeDtypeStruct(w.shape, BF16) for w in later_weights],
        grid=(b, steps_j),
        in_specs=[
            pl.BlockSpec((1, tm, d), lambda i, j: (i, j, 0)),
            pl.BlockSpec((1, 6, d), lambda i, j: (i, 0, 0)),
            _const_spec((1, d)),
            _const_spec((d, 2 * width)),
            _const_spec((1, width)),
            _const_spec((1, width)),
            _const_spec((SGU_GROUPS, CHUNK, CHUNK)),
            _const_spec((SGU_GROUPS, CHUNK, gd)),
            _const_spec((width, d)),
        ] + cast_specs,
        out_specs=[pl.BlockSpec((1, tm, d), lambda i, j: (i, j, 0))] + cast_specs,
        scratch_shapes=[
            pltpu.VMEM((tm, width), F32),
            pltpu.VMEM((tm, width), BF16),
            pltpu.VMEM((tm, width), BF16),
        ],
        compiler_params=pltpu.CompilerParams(
            dimension_semantics=("arbitrary", "arbitrary"),
            vmem_limit_bytes=VMEM_LIMIT_BYTES),
        name="sgu",
    )(x, mod, norm_g.reshape(1, d), w_in.astype(BF16), ln_g.reshape(1, width),
      ln_b.reshape(1, width), w_s.astype(BF16), bs_full, w_out.astype(BF16), *later_weights)
    return outs[0], outs[1:]


def _ffn_kernel(x_ref, mod_ref, ng_ref, wgu_ref, wd_ref, o_ref, *, ffn, col_block):
    x = x_ref[0]
    mod = mod_ref[0]
    h = _modnorm(x, ng_ref[...], mod[4:5], mod[3:4]).astype(BF16)
    acc = jnp.zeros(x.shape, F32)
    for j in range(ffn // col_block):
        g = jnp.dot(h, wgu_ref[:, j * col_block:(j + 1) * col_block],
                    preferred_element_type=F32)
        u = jnp.dot(h, wgu_ref[:, ffn + j * col_block:ffn + (j + 1) * col_block],
                    preferred_element_type=F32)
        a = (jax.nn.silu(g) * u).astype(BF16)
        acc = acc + jnp.dot(a, wd_ref[j * col_block:(j + 1) * col_block, :],
                            preferred_element_type=F32)
    o_ref[0] = x + mod[5:6] * acc


def _ffn_layer(x, mod, norm_g, w_gate_up, w_down, *, tm=1024):
    b, s, d = x.shape
    ffn = w_down.shape[0]
    kern = functools.partial(_ffn_kernel, ffn=ffn, col_block=256)
    return pl.pallas_call(
        kern,
        out_shape=jax.ShapeDtypeStruct((b, s, d), F32),
        grid=(b, s // tm),
        in_specs=[
            pl.BlockSpec((1, tm, d), lambda i, j: (i, j, 0)),
            pl.BlockSpec((1, 6, d), lambda i, j: (i, 0, 0)),
            _const_spec((1, d)),
            _const_spec((d, 2 * ffn)),
            _const_spec((ffn, d)),
        ],
        out_specs=pl.BlockSpec((1, tm, d), lambda i, j: (i, j, 0)),
        compiler_params=pltpu.CompilerParams(
            dimension_semantics=("parallel", "parallel"),
            vmem_limit_bytes=VMEM_LIMIT_BYTES),
        name="ffn",
    )(x, mod, norm_g.reshape(1, d), w_gate_up.astype(BF16), w_down.astype(BF16))


def _retention_kernel(x_ref, pos_ref, mod_ref, ng_ref, invf_ref, win_ref, wout_ref,
                      intra_ref, xi_ref, zeta_ref, o_ref,
                      state_ref, q_ref, qx_ref, k_ref, kz_ref, v_ref, sg_ref, ob_ref,
                      *, tm, rc, dk, dv, decay):
    heads = RET_HEADS
    qk = heads * dk

    @pl.when(pl.program_id(1) == 0)
    def _():
        state_ref[...] = jnp.zeros(state_ref.shape, F32)

    mod = mod_ref[0]
    half = dk // 2

    def rope(p, cos, sin):
        t1, t2 = p[:, :half], p[:, half:]
        return jnp.concatenate([t1 * cos - t2 * sin, t2 * cos + t1 * sin], axis=-1)

    for c in range(tm // rc):
        rs = slice(c * rc, (c + 1) * rc)
        x = x_ref[0, rs, :]
        h = _modnorm(x, ng_ref[...], mod[1:2], mod[0:1]).astype(BF16)
        ang = pos_ref[0, rs, :].astype(F32) * invf_ref[...]
        cos = jnp.cos(ang)
        sin = jnp.sin(ang)
        for hd in range(heads):
            cs = slice(hd * dk, (hd + 1) * dk)
            q = rope(jnp.dot(h, win_ref[:, cs], preferred_element_type=F32), cos, sin)
            q_ref[rs, cs] = q.astype(BF16)
            qx_ref[rs, cs] = (q * xi_ref[:, cs]).astype(BF16)
            k = rope(jnp.dot(h, win_ref[:, qk + hd * dk:qk + (hd + 1) * dk],
                             preferred_element_type=F32), cos, sin) * (dk ** -0.5)
            k_ref[rs, cs] = k.astype(BF16)
            kz_ref[rs, cs] = (k * zeta_ref[:, cs]).astype(BF16)
        for hd in range(heads):
            cs = slice(hd * dv, (hd + 1) * dv)
            v_ref[rs, cs] = jnp.dot(h, win_ref[:, 2 * qk + hd * dv:2 * qk + (hd + 1) * dv],
                                    preferred_element_type=F32).astype(BF16)
            g = jnp.dot(h, win_ref[:, 2 * qk + heads * dv + hd * dv:
                                   2 * qk + heads * dv + (hd + 1) * dv],
                        preferred_element_type=F32)
            sg_ref[rs, cs] = jax.nn.silu(g)
        for hd in range(heads):
            ks = slice(hd * dk, (hd + 1) * dk)
            vs = slice(hd * dv, (hd + 1) * dv)
            vh = v_ref[rs, vs]
            scores = lax.dot_general(q_ref[rs, ks], k_ref[rs, ks], (((1,), (1,)), ((), ())),
                                     preferred_element_type=F32)
            scores = (scores * intra_ref[hd]).astype(BF16)
            st = state_ref[hd]
            o = (jnp.dot(scores, vh, preferred_element_type=F32)
                 + jnp.dot(qx_ref[rs, ks], st.astype(BF16), preferred_element_type=F32))
            state_ref[hd] = st * decay[hd] + lax.dot_general(
                kz_ref[rs, ks], vh, (((0,), (0,)), ((), ())), preferred_element_type=F32)
            on = o * lax.rsqrt(jnp.mean(o * o, axis=-1, keepdims=True) + EPS)
            ob_ref[rs, vs] = (on * sg_ref[rs, vs]).astype(BF16)
        out = jnp.dot(ob_ref[rs, :], wout_ref[...], preferred_element_type=F32)
        o_ref[0, rs, :] = x + mod[2:3] * out


def _retention_layer(x, positions, mod, norm_g, w_in, w_out, *, tm=512, rc=256):
    b, s, d = x.shape
    heads = RET_HEADS
    dk = d // heads
    dv = w_out.shape[0] // heads
    qk = heads * dk
    vw = heads * dv
    log_gamma = jnp.log1p(-jnp.exp2(-5.0 - jnp.arange(heads, dtype=F32)))
    idx = jnp.arange(rc, dtype=F32)
    rel = idx[:, None] - idx[None, :]
    intra = jnp.where(rel >= 0, jnp.exp(log_gamma[:, None, None] * jnp.maximum(rel, 0.0)), 0.0)
    xi = jnp.exp(log_gamma[:, None] * (idx + 1.0))
    zeta = jnp.exp(log_gamma[:, None] * (rc - 1.0 - idx))
    xi_t = jnp.repeat(xi.T, dk, axis=1)
    zeta_t = jnp.repeat(zeta.T, dk, axis=1)
    decay = tuple(float(np.exp(np.log1p(-2.0 ** (-5.0 - hd)) * rc)) for hd in range(heads))
    inv_freq = 1.0 / (ROPE_BASE ** (jnp.arange(0, dk, 2, dtype=F32) / dk))
    kern = functools.partial(_retention_kernel, tm=tm, rc=rc, dk=dk, dv=dv, decay=decay)
    return pl.pallas_call(
        kern,
        out_shape=jax.ShapeDtypeStruct((b, s, d), F32),
        grid=(b, s // tm),
        in_specs=[
            pl.BlockSpec((1, tm, d), lambda i, j: (i, j, 0)),
            pl.BlockSpec((1, tm, 1), lambda i, j: (i, j, 0)),
            pl.BlockSpec((1, 6, d), lambda i, j: (i, 0, 0)),
            _const_spec((1, d)),
            _const_spec((1, dk // 2)),
            _const_spec((d, 2 * qk + 2 * vw)),
            _const_spec((vw, d)),
            _const_spec((heads, rc, rc)),
            _const_spec((rc, qk)),
            _const_spec((rc, qk)),
        ],
        out_specs=pl.BlockSpec((1, tm, d), lambda i, j: (i, j, 0)),
        scratch_shapes=[
            pltpu.VMEM((heads, dk, dv), F32),
            pltpu.VMEM((tm, qk), BF16),
            pltpu.VMEM((tm, qk), BF16),
            pltpu.VMEM((tm, qk), BF16),
            pltpu.VMEM((tm, qk), BF16),
            pltpu.VMEM((tm, vw), BF16),
            pltpu.VMEM((tm, vw), F32),
            pltpu.VMEM((tm, vw), BF16),
        ],
        compiler_params=pltpu.CompilerParams(
            dimension_semantics=("parallel", "arbitrary"),
            vmem_limit_bytes=VMEM_LIMIT_BYTES),
        name="retention",
    )(x, positions.reshape(b, s, 1), mod, norm_g.reshape(1, d), inv_freq.reshape(1, dk // 2),
      w_in.astype(BF16), w_out.astype(BF16), intra, xi_t, zeta_t)


META_E1, META_E2, META_R1, META_R2, META_W1, META_W2 = range(6)
META_ROWS = 8


def _router_kernel(x_ref, mod_ref, ng_ref, wr_ref, br_ref, meta_ref, meta_t_ref, cnt_ref,
                   carry_ref):
    @pl.when((pl.program_id(0) == 0) & (pl.program_id(1) == 0))
    def _():
        carry_ref[...] = jnp.zeros(carry_ref.shape, F32)

    x = x_ref[0]
    mod = mod_ref[0]
    h = _modnorm(x, ng_ref[...], mod[4:5], mod[3:4])
    tm = h.shape[0]
    lane = lax.broadcasted_iota(jnp.int32, (tm, LANES), 1)
    neg = jnp.float32(-jnp.inf)
    logits = jnp.full((tm, LANES), neg, F32)
    for e in range(N_EXPERTS):
        le = jnp.sum(h * wr_ref[e:e + 1, :], axis=-1, keepdims=True)
        logits = jnp.where(lane == e, le, logits)
    logits = logits + br_ref[...]
    m1 = jnp.max(logits, axis=-1, keepdims=True)
    i1 = jnp.min(jnp.where(logits == m1, lane, LANES), axis=-1, keepdims=True)
    rest = jnp.where(lane == i1, neg, logits)
    m2 = jnp.max(rest, axis=-1, keepdims=True)
    i2 = jnp.min(jnp.where(rest == m2, lane, LANES), axis=-1, keepdims=True)
    e2 = jnp.exp(m2 - m1)
    w1 = 1.0 / (1.0 + e2)
    w2 = e2 / (1.0 + e2)
    sel = jnp.where((lane == i1) | (lane == i2), 1.0, 0.0)
    row = lax.broadcasted_iota(jnp.int32, (tm, tm), 0)
    col = lax.broadcasted_iota(jnp.int32, (tm, tm), 1)
    tri = jnp.where(row > col, 1.0, 0.0).astype(BF16)
    before = jnp.dot(tri, sel.astype(BF16), preferred_element_type=F32) + carry_ref[...]
    r1 = jnp.sum(jnp.where(lane == i1, before, 0.0), axis=-1, keepdims=True)
    r2 = jnp.sum(jnp.where(lane == i2, before, 0.0), axis=-1, keepdims=True)
    carry_ref[...] += jnp.sum(sel, axis=0, keepdims=True)
    cnt_ref[...] = carry_ref[...]
    meta = jnp.zeros(logits.shape, F32)
    for k, val in ((META_E1, i1.astype(F32)), (META_E2, i2.astype(F32)), (META_R1, r1),
                   (META_R2, r2), (META_W1, w1), (META_W2, w2)):
        meta = jnp.where(lane == k, val, meta)
    meta_ref[0] = meta
    meta_t_ref[...] = meta.T[:META_ROWS, :]


def _router(x, mod, norm_g, w_router, b_router, *, tm=512):
    b, s, d = x.shape
    ne = w_router.shape[1]
    assert ne == N_EXPERTS
    wr = w_router.T
    br = jnp.zeros((1, LANES), F32).at[0, :ne].set(b_router)
    return pl.pallas_call(
        _router_kernel,
        out_shape=(jax.ShapeDtypeStruct((b, s, LANES), F32),
                   jax.ShapeDtypeStruct((META_ROWS, b * s), F32),
                   jax.ShapeDtypeStruct((1, LANES), F32)),
        grid=(b, s // tm),
        in_specs=[
            pl.BlockSpec((1, tm, d), lambda i, j: (i, j, 0)),
            pl.BlockSpec((1, 6, d), lambda i, j: (i, 0, 0)),
            _const_spec((1, d)),
            _const_spec((ne, d)),
            _const_spec((1, LANES)),
        ],
        out_specs=(pl.BlockSpec((1, tm, LANES), lambda i, j: (i, j, 0)),
                   pl.BlockSpec((META_ROWS, tm), lambda i, j: (0, i * (s // tm) + j)),
                   pl.BlockSpec((1, LANES), lambda i, j: (0, 0))),
        scratch_shapes=[pltpu.VMEM((1, LANES), F32)],
        compiler_params=pltpu.CompilerParams(
            dimension_semantics=("arbitrary", "arbitrary"),
            vmem_limit_bytes=VMEM_LIMIT_BYTES),
        name="router",
    )(x, mod, norm_g.reshape(1, d), wr, br)


N_ZERO_SLOTS = 2 * N_EXPERTS


def _dispatch_kernel(info_ref, pos_ref, x_ref, mod_ref, ng_ref, hs_hbm, hbuf, zeros_ref,
                     sem, zsem, *, td, tr):
    i = pl.program_id(0)
    slot = lax.rem(i, 2)

    def zero_copy(k):
        start = pl.multiple_of(info_ref[N_ZERO_SLOTS + k] * (tr * SUBLANES), tr * SUBLANES)
        return pltpu.make_async_copy(zeros_ref, hs_hbm.at[pl.ds(start, tr * SUBLANES)], zsem)

    @pl.when(i == 0)
    def _():
        zeros_ref[...] = jnp.zeros(zeros_ref.shape, F32)
        for k in range(N_ZERO_SLOTS):
            @pl.when(info_ref[k] == 1)
            def _():
                zero_copy(k).start()
        for k in range(N_ZERO_SLOTS):
            @pl.when(info_ref[k] == 1)
            def _():
                zero_copy(k).wait()

    mod = mod_ref[0]
    h = _modnorm(x_ref[...], ng_ref[...], mod[4:5], mod[3:4])
    _store_slabs(hbuf.at[slot], h, 0, td)

    def hs_slab(p8):
        return hs_hbm.at[pl.ds(pl.multiple_of(p8, SUBLANES), SUBLANES)]

    for t in range(td):
        src = hbuf.at[slot, pl.ds(t * SUBLANES, SUBLANES)]
        pltpu.make_async_copy(src, hs_slab(pos_ref[0, 0, t]), sem.at[slot]).start(priority=0)
        pltpu.make_async_copy(src, hs_slab(pos_ref[0, 0, td + t]),
                              sem.at[slot]).start(priority=1)

    def wait_rows(s):
        n_sub = 2 * td * SUBLANES
        pltpu.make_async_copy(hs_hbm.at[pl.ds(0, n_sub)], hs_hbm.at[pl.ds(0, n_sub)],
                              sem.at[s]).wait()

    @pl.when(i > 0)
    def _():
        wait_rows(1 - slot)

    @pl.when(i == pl.num_programs(0) - 1)
    def _():
        wait_rows(slot)


def _dispatch(x, mod, norm_g, pos, zero_info, *, n_rows, seq, td, tr):
    n, d = x.shape
    assert d == SUBLANES * LANES, "a row must fill exactly one (8, 128) slab"
    kern = functools.partial(_dispatch_kernel, td=td, tr=tr)
    return pl.pallas_call(
        kern,
        out_shape=jax.ShapeDtypeStruct((n_rows * SUBLANES, LANES), F32),
        grid_spec=pltpu.PrefetchScalarGridSpec(
            num_scalar_prefetch=1,
            grid=(n // td,),
            in_specs=[
                pl.BlockSpec((1, 1, 2 * td), lambda i, info: (i, 0, 0),
                             memory_space=pltpu.SMEM),
                pl.BlockSpec((td, d), lambda i, info: (i, 0)),
                pl.BlockSpec((1, 6, d), lambda i, info: ((i * td) // seq, 0, 0)),
                pl.BlockSpec((1, d), lambda i, info: (0, 0)),
            ],
            out_specs=pl.BlockSpec(memory_space=pl.ANY),
            scratch_shapes=[pltpu.VMEM((2, td * SUBLANES, LANES), F32),
                            pltpu.VMEM((tr * SUBLANES, LANES), F32),
                            pltpu.SemaphoreType.DMA((2,)), pltpu.SemaphoreType.DMA],
        ),
        compiler_params=pltpu.CompilerParams(
            dimension_semantics=("arbitrary",), vmem_limit_bytes=VMEM_LIMIT_BYTES),
        name="dispatch",
    )(zero_info, pos, x, mod, norm_g.reshape(1, d))


def _experts_kernel(te_ref, nu_ref, hs_ref, wgu_ref, wd_ref, y_ref, *, ffn, col_block, tr):
    i = pl.program_id(0)

    @pl.when(i < nu_ref[0])
    def _():
        h = _load_slabs(hs_ref, 0, tr).astype(BF16)
        y = jnp.zeros((tr, SUBLANES * LANES), F32)
        for j in range(ffn // col_block):
            g = jnp.dot(h, wgu_ref[0, :, j * col_block:(j + 1) * col_block],
                        preferred_element_type=F32)
            u = jnp.dot(h, wgu_ref[0, :, ffn + j * col_block:ffn + (j + 1) * col_block],
                        preferred_element_type=F32)
            a = (jax.nn.silu(g) * u).astype(BF16)
            y = y + jnp.dot(a, wd_ref[0, j * col_block:(j + 1) * col_block, :],
                            preferred_element_type=F32)
        _store_slabs(y_ref, y, 0, tr)

    @pl.when(i >= nu_ref[0])
    def _():
        y_ref[...] = jnp.zeros(y_ref.shape, F32)


def _experts(hs, tile_expert, n_used, w_gate_up, w_down, *, tr):
    ne, ffn, d = w_down.shape
    n_tiles = hs.shape[0] // (tr * SUBLANES)
    kern = functools.partial(_experts_kernel, ffn=ffn, col_block=256, tr=tr)
    return pl.pallas_call(
        kern,
        out_shape=jax.ShapeDtypeStruct(hs.shape, F32),
        grid_spec=pltpu.PrefetchScalarGridSpec(
            num_scalar_prefetch=2,
            grid=(n_tiles,),
            in_specs=[
                pl.BlockSpec((tr * SUBLANES, LANES),
                             lambda i, te, nu: (jnp.minimum(i, nu[0] - 1), 0)),
                pl.BlockSpec((1, d, 2 * ffn), lambda i, te, nu: (te[i], 0, 0)),
                pl.BlockSpec((1, ffn, d), lambda i, te, nu: (te[i], 0, 0)),
            ],
            out_specs=pl.BlockSpec((tr * SUBLANES, LANES), lambda i, te, nu: (i, 0)),
        ),
        compiler_params=pltpu.CompilerParams(
            dimension_semantics=("arbitrary",), vmem_limit_bytes=VMEM_LIMIT_BYTES),
        name="experts",
    )(tile_expert, n_used, hs, w_gate_up.astype(BF16), w_down.astype(BF16))


def _combine_kernel(posc_ref, posn_ref, x_ref, meta_ref, mod_ref, fg_ref, y_hbm, o_ref,
                    ybuf, sem, *, tc):
    i = pl.program_id(0)
    slot = lax.rem(i, 2)

    def issue(pos_ref, to_slot):
        for t in range(2 * tc):
            src = y_hbm.at[pl.ds(pl.multiple_of(pos_ref[0, 0, t], SUBLANES), SUBLANES)]
            pltpu.make_async_copy(src, ybuf.at[to_slot, pl.ds(t * SUBLANES, SUBLANES)],
                                  sem.at[to_slot]).start(priority=t % 2)

    @pl.when(i == 0)
    def _():
        issue(posc_ref, 0)

    @pl.when(i + 1 < pl.num_programs(0))
    def _():
        issue(posn_ref, 1 - slot)

    pltpu.make_async_copy(y_hbm.at[pl.ds(0, 2 * tc * SUBLANES)], ybuf.at[slot],
                          sem.at[slot]).wait()
    meta = meta_ref[...]
    lane = lax.broadcasted_iota(jnp.int32, meta.shape, 1)
    w1 = jnp.sum(jnp.where(lane == META_W1, meta, 0.0), axis=-1, keepdims=True)
    w2 = jnp.sum(jnp.where(lane == META_W2, meta, 0.0), axis=-1, keepdims=True)
    y = w1 * _load_slabs(ybuf.at[slot], 0, tc) + w2 * _load_slabs(ybuf.at[slot], tc, tc)
    xo = x_ref[...] + mod_ref[0][5:6] * y
    ms = jnp.mean(xo * xo, axis=-1, keepdims=True)
    o_ref[...] = xo * lax.rsqrt(ms + EPS) * fg_ref[...]


def _combine(x, meta, mod, final_g, y_sorted, pos, *, seq, tc):
    n, d = x.shape
    nblk = n // tc
    kern = functools.partial(_combine_kernel, tc=tc)
    smem_pos = lambda f: pl.BlockSpec((1, 1, 2 * tc), f, memory_space=pltpu.SMEM)
    return pl.pallas_call(
        kern,
        out_shape=jax.ShapeDtypeStruct((n, d), F32),
        grid=(nblk,),
        in_specs=[
            smem_pos(lambda i: (i, 0, 0)),
            smem_pos(lambda i: (jnp.minimum(i + 1, nblk - 1), 0, 0)),
            pl.BlockSpec((tc, d), lambda i: (i, 0)),
            pl.BlockSpec((tc, LANES), lambda i: (i, 0)),
            pl.BlockSpec((1, 6, d), lambda i: ((i * tc) // seq, 0, 0)),
            _const_spec((1, d)),
            pl.BlockSpec(memory_space=pl.ANY),
        ],
        out_specs=pl.BlockSpec((tc, d), lambda i: (i, 0)),
        scratch_shapes=[pltpu.VMEM((2, 2 * tc * SUBLANES, LANES), F32),
                        pltpu.SemaphoreType.DMA((2,))],
        compiler_params=pltpu.CompilerParams(
            dimension_semantics=("arbitrary",), vmem_limit_bytes=VMEM_LIMIT_BYTES),
        name="combine",
    )(pos, pos, x, meta, mod, final_g.reshape(1, d), y_sorted)


def _moe_layer(x, mod, norm_g, final_g, w_router, b_router, w_gate_up, w_down,
               *, tr=512, td=1024, tc=256):
    b, s, d = x.shape
    n = b * s
    ne = w_router.shape[1]
    meta, meta_t, cnt = _router(x, mod, norm_g, w_router, b_router)
    meta = meta.reshape(n, LANES)
    e1 = meta_t[META_E1].astype(jnp.int32)
    e2 = meta_t[META_E2].astype(jnp.int32)
    r1 = meta_t[META_R1].astype(jnp.int32)
    r2 = meta_t[META_R2].astype(jnp.int32)
    counts = cnt[0, :ne].astype(jnp.int32)
    tiles_per = (counts + tr - 1) // tr
    tile_end = jnp.cumsum(tiles_per)
    n_used = tile_end[-1]
    group_start = (tile_end - tiles_per) * tr
    pos1 = group_start[e1] + r1
    pos2 = group_start[e2] + r2
    nt = 2 * n // tr + ne
    tile_ids = jnp.arange(nt, dtype=jnp.int32)
    te = jnp.sum((tile_ids[:, None] >= tile_end[None, :]).astype(jnp.int32), axis=1)
    te = jnp.minimum(te, ne - 1)
    te = jnp.where(tile_ids < n_used, te, te[n_used - 1]).astype(jnp.int32)
    zero_tiles = jnp.concatenate([tile_end - 1, n_used + jnp.arange(ne, dtype=jnp.int32)])
    zero_valid = jnp.concatenate([(counts % tr) != 0, n_used + jnp.arange(ne) < nt])
    zero_info = jnp.concatenate([zero_valid.astype(jnp.int32),
                                 jnp.clip(zero_tiles, 0, nt - 1).astype(jnp.int32)])

    def blocked(t):
        return SUBLANES * jnp.concatenate(
            [pos1.reshape(n // t, 1, t), pos2.reshape(n // t, 1, t)], axis=-1)

    hs = _dispatch(x.reshape(n, d), mod, norm_g, blocked(td), zero_info, n_rows=nt * tr,
                   seq=s, td=td, tr=tr)
    y_sorted = _experts(hs, te, n_used.reshape(1).astype(jnp.int32), w_gate_up, w_down, tr=tr)
    out = _combine(x.reshape(n, d), meta, mod, final_g, y_sorted, blocked(tc), seq=s, tc=tc)
    return out.reshape(b, s, d)


def kernel(x, c, positions, ada_w, ada_b, mix_norm_g, ffn_norm_g, sgu_w_in, sgu_ln_g, sgu_ln_b,
           sgu_w_s, sgu_b_s, sgu_w_out, ffn_w_gate_up, ffn_w_down, ret_w_in, ret_w_out,
           moe_w_router, moe_b_router, moe_w_gate_up, moe_w_down, final_norm_g):
    b, s, d = x.shape
    depth = ada_w.shape[0]
    assert depth == 2, "layer 0 = SGU + SwiGLU, layer 1 = retention + MoE"
    mod = _adaln(c, ada_w, ada_b).reshape(depth, b, 6, d)
    ne, _, two_ffn = moe_w_gate_up[0].shape
    ffn = moe_w_down[0].shape[1]
    later = [ffn_w_gate_up[0], ffn_w_down[0], ret_w_in[0], ret_w_out[0],
             moe_w_gate_up[0].reshape(ne * d, two_ffn), moe_w_down[0].reshape(ne * ffn, d)]
    x, (ffn_gu, ffn_dn, ret_in, ret_out, moe_gu, moe_dn) = _sgu_layer(
        x, mod[0], mix_norm_g[0], sgu_w_in[0], sgu_ln_g[0], sgu_ln_b[0], sgu_w_s[0],
        sgu_b_s[0], sgu_w_out[0], later)
    x = _ffn_layer(x, mod[0], ffn_norm_g[0], ffn_gu, ffn_dn)
    x = _retention_layer(x, positions, mod[1], mix_norm_g[1], ret_in, ret_out)
    return _moe_layer(x, mod[1], ffn_norm_g[1], final_norm_g, moe_w_router[0], moe_b_router[0],
                      moe_gu.reshape(ne, d, two_ffn), moe_dn.reshape(ne, ffn, d))
```

```python
import functools

import jax
import jax.numpy as jnp
import numpy as np
from jax import lax
from jax.experimental import pallas as pl
from jax.experimental.pallas import tpu as pltpu

F32 = jnp.float32
BF16 = jnp.bfloat16

CHUNK = 128
SGU_GROUPS = 8
RET_HEADS = 4
N_EXPERTS = 8
ROPE_BASE = 10000.0
EPS = 1e-6
LANES = 128
SUBLANES = 8
BF16_TILE_ROWS = 16
VMEM_LIMIT_BYTES = 56 * 1024 * 1024


def _const_spec(shape):
    nd = len(shape)
    return pl.BlockSpec(shape, lambda *_: (0,) * nd, pipeline_mode=pl.Buffered(1))


def _store_slabs(ref, rows, first, count):
    for c in range(SUBLANES):
        ref[pl.ds(first * SUBLANES + c, count, stride=SUBLANES), :] = \
            rows[:, c * LANES:(c + 1) * LANES]


def _load_slabs(ref, first, count):
    return jnp.concatenate(
        [ref[pl.ds(first * SUBLANES + c, count, stride=SUBLANES), :] for c in range(SUBLANES)],
        axis=-1)


def _modnorm(x, g, scale, shift):
    ms = jnp.mean(x * x, axis=-1, keepdims=True)
    return (x * lax.rsqrt(ms + EPS) * g) * (1.0 + scale) + shift


def _adaln_kernel(c_ref, w_ref, b_ref, o_ref):
    sc = jax.nn.silu(c_ref[...])
    o_ref[0] = jnp.dot(sc, w_ref[0], precision=lax.Precision.HIGHEST,
                       preferred_element_type=F32) + b_ref[0]


def _adaln(c, ada_w, ada_b):
    depth, d, six_d = ada_w.shape
    b = c.shape[0]
    tn = 1024
    return pl.pallas_call(
        _adaln_kernel,
        out_shape=jax.ShapeDtypeStruct((depth, b, six_d), F32),
        grid=(depth, six_d // tn),
        in_specs=[
            pl.BlockSpec((b, d), lambda l, j: (0, 0)),
            pl.BlockSpec((1, d, tn), lambda l, j: (l, 0, j)),
            pl.BlockSpec((1, 1, tn), lambda l, j: (l, 0, j)),
        ],
        out_specs=pl.BlockSpec((1, b, tn), lambda l, j: (l, 0, j)),
        compiler_params=pltpu.CompilerParams(
            dimension_semantics=("parallel", "parallel"),
            vmem_limit_bytes=VMEM_LIMIT_BYTES),
        name="adaln",
    )(c, ada_w, ada_b.reshape(depth, 1, six_d))


def _sgu_kernel(x_ref, mod_ref, ng_ref, win_ref, lng_ref, lnb_ref, ws_ref, bs_ref,
                wout_ref, *rest, tm, width, col_block, n_cast):
    cast_in = rest[:n_cast]
    o_ref = rest[n_cast]
    cast_out = rest[n_cast + 1:2 * n_cast + 1]
    u_ref, vn_ref, y_ref = rest[2 * n_cast + 1:]
    for src, dst in zip(cast_in, cast_out):
        dst[...] = src[...].astype(BF16)
    x = x_ref[0]
    mod = mod_ref[0]
    h = _modnorm(x, ng_ref[...], mod[1:2], mod[0:1]).astype(BF16)
    nb = width // col_block
    vs = []
    for j in range(2 * nb):
        z = jnp.dot(h, win_ref[:, j * col_block:(j + 1) * col_block],
                    preferred_element_type=F32)
        z = jax.nn.gelu(z)
        if j < nb:
            u_ref[:, j * col_block:(j + 1) * col_block] = z
        else:
            vs.append(z)
    s1 = sum(jnp.sum(v, axis=-1, keepdims=True) for v in vs)
    mu = s1 * (1.0 / width)
    s2 = sum(jnp.sum((v - mu) * (v - mu), axis=-1, keepdims=True) for v in vs)
    rstd = lax.rsqrt(s2 * (1.0 / width) + EPS)
    for j, v in enumerate(vs):
        sl = slice(j * col_block, (j + 1) * col_block)
        vn_ref[:, sl] = ((v - mu) * rstd * lng_ref[:, sl] + lnb_ref[:, sl]).astype(BF16)
    gd = width // SGU_GROUPS
    row = lax.broadcasted_iota(jnp.int32, (CHUNK, CHUNK), 0)
    col = lax.broadcasted_iota(jnp.int32, (CHUNK, CHUNK), 1)
    causal = row >= col
    for g in range(SGU_GROUPS):
        w = jnp.where(causal, ws_ref[g], jnp.zeros((), BF16))
        cs = slice(g * gd, (g + 1) * gd)
        for c in range(tm // CHUNK):
            rs = slice(c * CHUNK, (c + 1) * CHUNK)
            fv = jnp.dot(w, vn_ref[rs, cs], preferred_element_type=F32) + bs_ref[g]
            y_ref[rs, cs] = (u_ref[rs, cs] * fv).astype(BF16)
    out = jnp.dot(y_ref[...], wout_ref[...], preferred_element_type=F32)
    o_ref[0] = x + mod[2:3] * out


def _cast_blocks(rows, steps):
    for br in range(BF16_TILE_ROWS, rows + 1, BF16_TILE_ROWS):
        if rows % br == 0 and rows // br <= steps:
            return br, rows // br
    raise ValueError(f"no bf16-tile row block casts {rows} rows in {steps} steps")


def _sgu_layer(x, mod, norm_g, w_in, ln_g, ln_b, w_s, b_s, w_out, later_weights, *, tm=512):
    b, s, d = x.shape
    width = w_out.shape[0]
    gd = width // SGU_GROUPS
    bs_full = jnp.broadcast_to(b_s[:, :, None], (SGU_GROUPS, CHUNK, gd))
    steps_j = s // tm
    plans = [_cast_blocks(w.shape[0], b * steps_j) for w in later_weights]

    def cast_spec(w, plan):
        br, nblk = plan
        return pl.BlockSpec((br, w.shape[1]),
                            lambda i, j: (jnp.minimum(i * steps_j + j, nblk - 1), 0))

    cast_specs = [cast_spec(w, p) for w, p in zip(later_weights, plans)]
    kern = functools.partial(_sgu_kernel, tm=tm, width=width, col_block=512,
                             n_cast=len(later_weights))
    outs = pl.pallas_call(
        kern,
        out_shape=[jax.ShapeDtypeStruct((b, s, d), F32)]
        + [jax.ShapeDtypeStruct(w.shape, BF16) for w in later_weights],
        grid=(b, steps_j),
        in_specs=[
            pl.BlockSpec((1, tm, d), lambda i, j: (i, j, 0)),
            pl.BlockSpec((1, 6, d), lambda i, j: (i, 0, 0)),
            _const_spec((1, d)),
            _const_spec((d, 2 * width)),
            _const_spec((1, width)),
            _const_spec((1, width)),
            _const_spec((SGU_GROUPS, CHUNK, CHUNK)),
            _const_spec((SGU_GROUPS, CHUNK, gd)),
            _const_spec((width, d)),
        ] + cast_specs,
        out_specs=[pl.BlockSpec((1, tm, d), lambda i, j: (i, j, 0))] + cast_specs,
        scratch_shapes=[
            pltpu.VMEM((tm, width), F32),
            pltpu.VMEM((tm, width), BF16),
            pltpu.VMEM((tm, width), BF16),
        ],
        compiler_params=pltpu.CompilerParams(
            dimension_semantics=("arbitrary", "arbitrary"),
            vmem_limit_bytes=VMEM_LIMIT_BYTES),
        name="sgu",
    )(x, mod, norm_g.reshape(1, d), w_in.astype(BF16), ln_g.reshape(1, width),
      ln_b.reshape(1, width), w_s.astype(BF16), bs_full, w_out.astype(BF16), *later_weights)
    return outs[0], outs[1:]


def _ffn_kernel(x_ref, mod_ref, ng_ref, wgu_ref, wd_ref, o_ref, *, ffn, col_block):
    x = x_ref[0]
    mod = mod_ref[0]
    h = _modnorm(x, ng_ref[...], mod[4:5], mod[3:4]).astype(BF16)
    acc = jnp.zeros(x.shape, F32)
    for j in range(ffn // col_block):
        g = jnp.dot(h, wgu_ref[:, j * col_block:(j + 1) * col_block],
                    preferred_element_type=F32)
        u = jnp.dot(h, wgu_ref[:, ffn + j * col_block:ffn + (j + 1) * col_block],
                    preferred_element_type=F32)
        a = (jax.nn.silu(g) * u).astype(BF16)
        acc = acc + jnp.dot(a, wd_ref[j * col_block:(j + 1) * col_block, :],
                            preferred_element_type=F32)
    o_ref[0] = x + mod[5:6] * acc


def _ffn_layer(x, mod, norm_g, w_gate_up, w_down, *, tm=1024):
    b, s, d = x.shape
    ffn = w_down.shape[0]
    kern = functools.partial(_ffn_kernel, ffn=ffn, col_block=256)
    return pl.pallas_call(
        kern,
        out_shape=jax.ShapeDtypeStruct((b, s, d), F32),
        grid=(b, s // tm),
        in_specs=[
            pl.BlockSpec((1, tm, d), lambda i, j: (i, j, 0)),
            pl.BlockSpec((1, 6, d), lambda i, j: (i, 0, 0)),
            _const_spec((1, d)),
            _const_spec((d, 2 * ffn)),
            _const_spec((ffn, d)),
        ],
        out_specs=pl.BlockSpec((1, tm, d), lambda i, j: (i, j, 0)),
        compiler_params=pltpu.CompilerParams(
            dimension_semantics=("parallel", "parallel"),
            vmem_limit_bytes=VMEM_LIMIT_BYTES),
        name="ffn",
    )(x, mod, norm_g.reshape(1, d), w_gate_up.astype(BF16), w_down.astype(BF16))


def _retention_kernel(x_ref, pos_ref, mod_ref, ng_ref, invf_ref, win_ref, wout_ref,
                      intra_ref, xi_ref, zeta_ref, o_ref,
                      state_ref, q_ref, qx_ref, k_ref, kz_ref, v_ref, sg_ref, ob_ref,
                      *, tm, rc, dk, dv, decay):
    heads = RET_HEADS
    qk = heads * dk

    @pl.when(pl.program_id(1) == 0)
    def _():
        state_ref[...] = jnp.zeros(state_ref.shape, F32)

    mod = mod_ref[0]
    half = dk // 2

    def rope(p, cos, sin):
        t1, t2 = p[:, :half], p[:, half:]
        return jnp.concatenate([t1 * cos - t2 * sin, t2 * cos + t1 * sin], axis=-1)

    for c in range(tm // rc):
        rs = slice(c * rc, (c + 1) * rc)
        x = x_ref[0, rs, :]
        h = _modnorm(x, ng_ref[...], mod[1:2], mod[0:1]).astype(BF16)
        ang = pos_ref[0, rs, :].astype(F32) * invf_ref[...]
        cos = jnp.cos(ang)
        sin = jnp.sin(ang)
        for hd in range(heads):
            cs = slice(hd * dk, (hd + 1) * dk)
            q = rope(jnp.dot(h, win_ref[:, cs], preferred_element_type=F32), cos, sin)
            q_ref[rs, cs] = q.astype(BF16)
            qx_ref[rs, cs] = (q * xi_ref[:, cs]).astype(BF16)
            k = rope(jnp.dot(h, win_ref[:, qk + hd * dk:qk + (hd + 1) * dk],
                             preferred_element_type=F32), cos, sin) * (dk ** -0.5)
            k_ref[rs, cs] = k.astype(BF16)
            kz_ref[rs, cs] = (k * zeta_ref[:, cs]).astype(BF16)
        for hd in range(heads):
            cs = slice(hd * dv, (hd + 1) * dv)
            v_ref[rs, cs] = jnp.dot(h, win_ref[:, 2 * qk + hd * dv:2 * qk + (hd + 1) * dv],
                                    preferred_element_type=F32).astype(BF16)
            g = jnp.dot(h, win_ref[:, 2 * qk + heads * dv + hd * dv:
                                   2 * qk + heads * dv + (hd + 1) * dv],
                        preferred_element_type=F32)
            sg_ref[rs, cs] = jax.nn.silu(g).astype(BF16)
        for hd in range(heads):
            ks = slice(hd * dk, (hd + 1) * dk)
            vs = slice(hd * dv, (hd + 1) * dv)
            vh = v_ref[rs, vs]
            scores = lax.dot_general(q_ref[rs, ks], k_ref[rs, ks], (((1,), (1,)), ((), ())),
                                     preferred_element_type=F32)
            scores = (scores * intra_ref[hd]).astype(BF16)
            st = state_ref[hd]
            o = (jnp.dot(scores, vh, preferred_element_type=F32)
                 + jnp.dot(qx_ref[rs, ks], st.astype(BF16), preferred_element_type=F32))
            state_ref[hd] = st * decay[hd] + lax.dot_general(
                kz_ref[rs, ks], vh, (((0,), (0,)), ((), ())), preferred_element_type=F32)
            on = o * lax.rsqrt(jnp.mean(o * o, axis=-1, keepdims=True) + EPS)
            ob_ref[rs, vs] = on.astype(BF16) * sg_ref[rs, vs]
        out = jnp.dot(ob_ref[rs, :], wout_ref[...], preferred_element_type=F32)
        o_ref[0, rs, :] = x + mod[2:3] * out


def _retention_layer(x, positions, mod, norm_g, w_in, w_out, *, tm=512, rc=256):
    b, s, d = x.shape
    heads = RET_HEADS
    dk = d // heads
    dv = w_out.shape[0] // heads
    qk = heads * dk
    vw = heads * dv
    log_gamma = jnp.log1p(-jnp.exp2(-5.0 - jnp.arange(heads, dtype=F32)))
    idx = jnp.arange(rc, dtype=F32)
    rel = idx[:, None] - idx[None, :]
    intra = jnp.where(rel >= 0, jnp.exp(log_gamma[:, None, None] * jnp.maximum(rel, 0.0)), 0.0)
    xi = jnp.exp(log_gamma[:, None] * (idx + 1.0))
    zeta = jnp.exp(log_gamma[:, None] * (rc - 1.0 - idx))
    xi_t = jnp.repeat(xi.T, dk, axis=1)
    zeta_t = jnp.repeat(zeta.T, dk, axis=1)
    decay = tuple(float(np.exp(np.log1p(-2.0 ** (-5.0 - hd)) * rc)) for hd in range(heads))
    inv_freq = 1.0 / (ROPE_BASE ** (jnp.arange(0, dk, 2, dtype=F32) / dk))
    kern = functools.partial(_retention_kernel, tm=tm, rc=rc, dk=dk, dv=dv, decay=decay)
    return pl.pallas_call(
        kern,
        out_shape=jax.ShapeDtypeStruct((b, s, d), F32),
        grid=(b, s // tm),
        in_specs=[
            pl.BlockSpec((1, tm, d), lambda i, j: (i, j, 0)),
            pl.BlockSpec((1, tm, 1), lambda i, j: (i, j, 0)),
            pl.BlockSpec((1, 6, d), lambda i, j: (i, 0, 0)),
            _const_spec((1, d)),
            _const_spec((1, dk // 2)),
            _const_spec((d, 2 * qk + 2 * vw)),
            _const_spec((vw, d)),
            _const_spec((heads, rc, rc)),
            _const_spec((rc, qk)),
            _const_spec((rc, qk)),
        ],
        out_specs=pl.BlockSpec((1, tm, d), lambda i, j: (i, j, 0)),
        scratch_shapes=[
            pltpu.VMEM((heads, dk, dv), F32),
            pltpu.VMEM((tm, qk), BF16),
            pltpu.VMEM((tm, qk), BF16),
            pltpu.VMEM((tm, qk), BF16),
            pltpu.VMEM((tm, qk), BF16),
            pltpu.VMEM((tm, vw), BF16),
            pltpu.VMEM((tm, vw), BF16),
            pltpu.VMEM((tm, vw), BF16),
        ],
        compiler_params=pltpu.CompilerParams(
            dimension_semantics=("parallel", "arbitrary"),
            vmem_limit_bytes=VMEM_LIMIT_BYTES),
        name="retention",
    )(x, positions.reshape(b, s, 1), mod, norm_g.reshape(1, d), inv_freq.reshape(1, dk // 2),
      w_in.astype(BF16), w_out.astype(BF16), intra, xi_t, zeta_t)


META_E1, META_E2, META_R1, META_R2, META_W1, META_W2 = range(6)
META_ROWS = 8


def _router_kernel(x_ref, mod_ref, ng_ref, wr_ref, br_ref, meta_ref, meta_t_ref, cnt_ref,
                   carry_ref):
    @pl.when((pl.program_id(0) == 0) & (pl.program_id(1) == 0))
    def _():
        carry_ref[...] = jnp.zeros(carry_ref.shape, F32)

    x = x_ref[0]
    mod = mod_ref[0]
    h = _modnorm(x, ng_ref[...], mod[4:5], mod[3:4])
    tm = h.shape[0]
    lane = lax.broadcasted_iota(jnp.int32, (tm, LANES), 1)
    neg = jnp.float32(-jnp.inf)
    logits = jnp.full((tm, LANES), neg, F32)
    for e in range(N_EXPERTS):
        le = jnp.sum(h * wr_ref[e:e + 1, :], axis=-1, keepdims=True)
        logits = jnp.where(lane == e, le, logits)
    logits = logits + br_ref[...]
    m1 = jnp.max(logits, axis=-1, keepdims=True)
    i1 = jnp.min(jnp.where(logits == m1, lane, LANES), axis=-1, keepdims=True)
    rest = jnp.where(lane == i1, neg, logits)
    m2 = jnp.max(rest, axis=-1, keepdims=True)
    i2 = jnp.min(jnp.where(rest == m2, lane, LANES), axis=-1, keepdims=True)
    e2 = jnp.exp(m2 - m1)
    w1 = 1.0 / (1.0 + e2)
    w2 = e2 / (1.0 + e2)
    sel = jnp.where((lane == i1) | (lane == i2), 1.0, 0.0)
    row = lax.broadcasted_iota(jnp.int32, (tm, tm), 0)
    col = lax.broadcasted_iota(jnp.int32, (tm, tm), 1)
    tri = jnp.where(row > col, 1.0, 0.0).astype(BF16)
    before = jnp.dot(tri, sel.astype(BF16), preferred_element_type=F32) + carry_ref[...]
    r1 = jnp.sum(jnp.where(lane == i1, before, 0.0), axis=-1, keepdims=True)
    r2 = jnp.sum(jnp.where(lane == i2, before, 0.0), axis=-1, keepdims=True)
    carry_ref[...] += jnp.sum(sel, axis=0, keepdims=True)
    cnt_ref[...] = carry_ref[...]
    meta = jnp.zeros(logits.shape, F32)
    for k, val in ((META_E1, i1.astype(F32)), (META_E2, i2.astype(F32)), (META_R1, r1),
                   (META_R2, r2), (META_W1, w1), (META_W2, w2)):
        meta = jnp.where(lane == k, val, meta)
    meta_ref[0] = meta
    meta_t_ref[...] = meta.T[:META_ROWS, :]


def _router(x, mod, norm_g, w_router, b_router, *, tm=512):
    b, s, d = x.shape
    ne = w_router.shape[1]
    assert ne == N_EXPERTS
    wr = w_router.T
    br = jnp.zeros((1, LANES), F32).at[0, :ne].set(b_router)
    return pl.pallas_call(
        _router_kernel,
        out_shape=(jax.ShapeDtypeStruct((b, s, LANES), F32),
                   jax.ShapeDtypeStruct((META_ROWS, b * s), F32),
                   jax.ShapeDtypeStruct((1, LANES), F32)),
        grid=(b, s // tm),
        in_specs=[
            pl.BlockSpec((1, tm, d), lambda i, j: (i, j, 0)),
            pl.BlockSpec((1, 6, d), lambda i, j: (i, 0, 0)),
            _const_spec((1, d)),
            _const_spec((ne, d)),
            _const_spec((1, LANES)),
        ],
        out_specs=(pl.BlockSpec((1, tm, LANES), lambda i, j: (i, j, 0)),
                   pl.BlockSpec((META_ROWS, tm), lambda i, j: (0, i * (s // tm) + j)),
                   pl.BlockSpec((1, LANES), lambda i, j: (0, 0))),
        scratch_shapes=[pltpu.VMEM((1, LANES), F32)],
        compiler_params=pltpu.CompilerParams(
            dimension_semantics=("arbitrary", "arbitrary"),
            vmem_limit_bytes=VMEM_LIMIT_BYTES),
        name="router",
    )(x, mod, norm_g.reshape(1, d), wr, br)


N_ZERO_SLOTS = 2 * N_EXPERTS


def _dispatch_kernel(info_ref, pos_ref, x_ref, mod_ref, ng_ref, hs_hbm, hbuf, zeros_ref,
                     sem, zsem, *, td, tr):
    i = pl.program_id(0)
    slot = lax.rem(i, 2)

    def zero_copy(k):
        start = pl.multiple_of(info_ref[N_ZERO_SLOTS + k] * (tr * SUBLANES), tr * SUBLANES)
        return pltpu.make_async_copy(zeros_ref, hs_hbm.at[pl.ds(start, tr * SUBLANES)], zsem)

    @pl.when(i == 0)
    def _():
        zeros_ref[...] = jnp.zeros(zeros_ref.shape, F32)
        for k in range(N_ZERO_SLOTS):
            @pl.when(info_ref[k] == 1)
            def _():
                zero_copy(k).start()
        for k in range(N_ZERO_SLOTS):
            @pl.when(info_ref[k] == 1)
            def _():
                zero_copy(k).wait()

    mod = mod_ref[0]
    h = _modnorm(x_ref[...], ng_ref[...], mod[4:5], mod[3:4])
    _store_slabs(hbuf.at[slot], h, 0, td)

    def hs_slab(p8):
        return hs_hbm.at[pl.ds(pl.multiple_of(p8, SUBLANES), SUBLANES)]

    for t in range(td):
        src = hbuf.at[slot, pl.ds(t * SUBLANES, SUBLANES)]
        pltpu.make_async_copy(src, hs_slab(pos_ref[0, 0, t]), sem.at[slot]).start(priority=0)
        pltpu.make_async_copy(src, hs_slab(pos_ref[0, 0, td + t]),
                              sem.at[slot]).start(priority=1)

    def wait_rows(s):
        n_sub = 2 * td * SUBLANES
        pltpu.make_async_copy(hs_hbm.at[pl.ds(0, n_sub)], hs_hbm.at[pl.ds(0, n_sub)],
                              sem.at[s]).wait()

    @pl.when(i > 0)
    def _():
        wait_rows(1 - slot)

    @pl.when(i == pl.num_programs(0) - 1)
    def _():
        wait_rows(slot)


def _dispatch(x, mod, norm_g, pos, zero_info, *, n_rows, seq, td, tr):
    n, d = x.shape
    assert d == SUBLANES * LANES, "a row must fill exactly one (8, 128) slab"
    kern = functools.partial(_dispatch_kernel, td=td, tr=tr)
    return pl.pallas_call(
        kern,
        out_shape=jax.ShapeDtypeStruct((n_rows * SUBLANES, LANES), F32),
        grid_spec=pltpu.PrefetchScalarGridSpec(
            num_scalar_prefetch=1,
            grid=(n // td,),
            in_specs=[
                pl.BlockSpec((1, 1, 2 * td), lambda i, info: (i, 0, 0),
                             memory_space=pltpu.SMEM),
                pl.BlockSpec((td, d), lambda i, info: (i, 0)),
                pl.BlockSpec((1, 6, d), lambda i, info: ((i * td) // seq, 0, 0)),
                pl.BlockSpec((1, d), lambda i, info: (0, 0)),
            ],
            out_specs=pl.BlockSpec(memory_space=pl.ANY),
            scratch_shapes=[pltpu.VMEM((2, td * SUBLANES, LANES), F32),
                            pltpu.VMEM((tr * SUBLANES, LANES), F32),
                            pltpu.SemaphoreType.DMA((2,)), pltpu.SemaphoreType.DMA],
        ),
        compiler_params=pltpu.CompilerParams(
            dimension_semantics=("arbitrary",), vmem_limit_bytes=VMEM_LIMIT_BYTES),
        name="dispatch",
    )(zero_info, pos, x, mod, norm_g.reshape(1, d))


def _experts_kernel(te_ref, nu_ref, hs_ref, wgu_ref, wd_ref, y_ref, *, ffn, col_block, tr):
    i = pl.program_id(0)

    @pl.when(i < nu_ref[0])
    def _():
        h = _load_slabs(hs_ref, 0, tr).astype(BF16)
        y = jnp.zeros((tr, SUBLANES * LANES), F32)
        for j in range(ffn // col_block):
            g = jnp.dot(h, wgu_ref[0, :, j * col_block:(j + 1) * col_block],
                        preferred_element_type=F32)
            u = jnp.dot(h, wgu_ref[0, :, ffn + j * col_block:ffn + (j + 1) * col_block],
                        preferred_element_type=F32)
            a = (jax.nn.silu(g) * u).astype(BF16)
            y = y + jnp.dot(a, wd_ref[0, j * col_block:(j + 1) * col_block, :],
                            preferred_element_type=F32)
        _store_slabs(y_ref, y, 0, tr)

    @pl.when(i >= nu_ref[0])
    def _():
        y_ref[...] = jnp.zeros(y_ref.shape, F32)


def _experts(hs, tile_expert, n_used, w_gate_up, w_down, *, tr):
    ne, ffn, d = w_down.shape
    n_tiles = hs.shape[0] // (tr * SUBLANES)
    kern = functools.partial(_experts_kernel, ffn=ffn, col_block=256, tr=tr)
    return pl.pallas_call(
        kern,
        out_shape=jax.ShapeDtypeStruct(hs.shape, F32),
        grid_spec=pltpu.PrefetchScalarGridSpec(
            num_scalar_prefetch=2,
            grid=(n_tiles,),
            in_specs=[
                pl.BlockSpec((tr * SUBLANES, LANES),
                             lambda i, te, nu: (jnp.minimum(i, nu[0] - 1), 0)),
                pl.BlockSpec((1, d, 2 * ffn), lambda i, te, nu: (te[i], 0, 0)),
                pl.BlockSpec((1, ffn, d), lambda i, te, nu: (te[i], 0, 0)),
            ],
            out_specs=pl.BlockSpec((tr * SUBLANES, LANES), lambda i, te, nu: (i, 0)),
        ),
        compiler_params=pltpu.CompilerParams(
            dimension_semantics=("arbitrary",), vmem_limit_bytes=VMEM_LIMIT_BYTES),
        name="experts",
    )(tile_expert, n_used, hs, w_gate_up.astype(BF16), w_down.astype(BF16))


def _combine_kernel(posc_ref, posn_ref, x_ref, meta_ref, mod_ref, fg_ref, y_hbm, o_ref,
                    ybuf, sem, *, tc):
    i = pl.program_id(0)
    slot = lax.rem(i, 2)

    def issue(pos_ref, to_slot):
        for t in range(2 * tc):
            src = y_hbm.at[pl.ds(pl.multiple_of(pos_ref[0, 0, t], SUBLANES), SUBLANES)]
            pltpu.make_async_copy(src, ybuf.at[to_slot, pl.ds(t * SUBLANES, SUBLANES)],
                                  sem.at[to_slot]).start(priority=t % 2)

    @pl.when(i == 0)
    def _():
        issue(posc_ref, 0)

    def wait_slot(s):
        pltpu.make_async_copy(y_hbm.at[pl.ds(0, 2 * tc * SUBLANES)], ybuf.at[s],
                              sem.at[s]).wait()

    wait_slot(slot)
    issue(posn_ref, 1 - slot)
    meta = meta_ref[...]
    lane = lax.broadcasted_iota(jnp.int32, meta.shape, 1)
    w1 = jnp.sum(jnp.where(lane == META_W1, meta, 0.0), axis=-1, keepdims=True)
    w2 = jnp.sum(jnp.where(lane == META_W2, meta, 0.0), axis=-1, keepdims=True)
    y = w1 * _load_slabs(ybuf.at[slot], 0, tc) + w2 * _load_slabs(ybuf.at[slot], tc, tc)
    xo = x_ref[...] + mod_ref[0][5:6] * y
    ms = jnp.mean(xo * xo, axis=-1, keepdims=True)
    o_ref[...] = xo * lax.rsqrt(ms + EPS) * fg_ref[...]

    @pl.when(i == pl.num_programs(0) - 1)
    def _():
        wait_slot(1 - slot)


def _combine(x, meta, mod, final_g, y_sorted, pos, *, seq, tc):
    n, d = x.shape
    nblk = n // tc
    kern = functools.partial(_combine_kernel, tc=tc)
    smem_pos = lambda f: pl.BlockSpec((1, 1, 2 * tc), f, memory_space=pltpu.SMEM)
    return pl.pallas_call(
        kern,
        out_shape=jax.ShapeDtypeStruct((n, d), F32),
        grid=(nblk,),
        in_specs=[
            smem_pos(lambda i: (i, 0, 0)),
            smem_pos(lambda i: (jnp.minimum(i + 1, nblk - 1), 0, 0)),
            pl.BlockSpec((tc, d), lambda i: (i, 0)),
            pl.BlockSpec((tc, LANES), lambda i: (i, 0)),
            pl.BlockSpec((1, 6, d), lambda i: ((i * tc) // seq, 0, 0)),
            _const_spec((1, d)),
            pl.BlockSpec(memory_space=pl.ANY),
        ],
        out_specs=pl.BlockSpec((tc, d), lambda i: (i, 0)),
        scratch_shapes=[pltpu.VMEM((2, 2 * tc * SUBLANES, LANES), F32),
                        pltpu.SemaphoreType.DMA((2,))],
        compiler_params=pltpu.CompilerParams(
            dimension_semantics=("arbitrary",), vmem_limit_bytes=VMEM_LIMIT_BYTES),
        name="combine",
    )(pos, pos, x, meta, mod, final_g.reshape(1, d), y_sorted)


def _moe_layer(x, mod, norm_g, final_g, w_router, b_router, w_gate_up, w_down,
               *, tr=512, td=1024, tc=256):
    b, s, d = x.shape
    n = b * s
    ne = w_router.shape[1]
    meta, meta_t, cnt = _router(x, mod, norm_g, w_router, b_router)
    meta = meta.reshape(n, LANES)
    e1 = meta_t[META_E1].astype(jnp.int32)
    e2 = meta_t[META_E2].astype(jnp.int32)
    r1 = meta_t[META_R1].astype(jnp.int32)
    r2 = meta_t[META_R2].astype(jnp.int32)
    counts = cnt[0, :ne].astype(jnp.int32)
    tiles_per = (counts + tr - 1) // tr
    tile_end = jnp.cumsum(tiles_per)
    n_used = tile_end[-1]
    group_start = (tile_end - tiles_per) * tr
    pos1 = group_start[e1] + r1
    pos2 = group_start[e2] + r2
    nt = 2 * n // tr + ne
    tile_ids = jnp.arange(nt, dtype=jnp.int32)
    te = jnp.sum((tile_ids[:, None] >= tile_end[None, :]).astype(jnp.int32), axis=1)
    te = jnp.minimum(te, ne - 1)
    te = jnp.where(tile_ids < n_used, te, te[n_used - 1]).astype(jnp.int32)
    zero_tiles = jnp.concatenate([tile_end - 1, n_used + jnp.arange(ne, dtype=jnp.int32)])
    zero_valid = jnp.concatenate([(counts % tr) != 0, n_used + jnp.arange(ne) < nt])
    zero_info = jnp.concatenate([zero_valid.astype(jnp.int32),
                                 jnp.clip(zero_tiles, 0, nt - 1).astype(jnp.int32)])

    def blocked(t):
        return SUBLANES * jnp.concatenate(
            [pos1.reshape(n // t, 1, t), pos2.reshape(n // t, 1, t)], axis=-1)

    hs = _dispatch(x.reshape(n, d), mod, norm_g, blocked(td), zero_info, n_rows=nt * tr,
                   seq=s, td=td, tr=tr)
    y_sorted = _experts(hs, te, n_used.reshape(1).astype(jnp.int32), w_gate_up, w_down, tr=tr)
    out = _combine(x.reshape(n, d), meta, mod, final_g, y_sorted, blocked(tc), seq=s, tc=tc)
    return out.reshape(b, s, d)


def kernel(x, c, positions, ada_w, ada_b, mix_norm_g, ffn_norm_g, sgu_w_in, sgu_ln_g, sgu_ln_b,
           sgu_w_s, sgu_b_s, sgu_w_out, ffn_w_gate_up, ffn_w_down, ret_w_in, ret_w_out,
           moe_w_router, moe_b_router, moe_w_gate_up, moe_w_down, final_norm_g):
    b, s, d = x.shape
    depth = ada_w.shape[0]
    assert depth == 2, "layer 0 = SGU + SwiGLU, layer 1 = retention + MoE"
    mod = _adaln(c, ada_w, ada_b).reshape(depth, b, 6, d)
    ne, _, two_ffn = moe_w_gate_up[0].shape
    ffn = moe_w_down[0].shape[1]
    later = [ffn_w_gate_up[0], ffn_w_down[0], ret_w_in[0], ret_w_out[0],
             moe_w_gate_up[0].reshape(ne * d, two_ffn), moe_w_down[0].reshape(ne * ffn, d)]
    x, (ffn_gu, ffn_dn, ret_in, ret_out, moe_gu, moe_dn) = _sgu_layer(
        x, mod[0], mix_norm_g[0], sgu_w_in[0], sgu_ln_g[0], sgu_ln_b[0], sgu_w_s[0],
        sgu_b_s[0], sgu_w_out[0], later)
    x = _ffn_layer(x, mod[0], ffn_norm_g[0], ffn_gu, ffn_dn)
    x = _retention_layer(x, positions, mod[1], mix_norm_g[1], ret_in, ret_out)
    return _moe_layer(x, mod[1], ffn_norm_g[1], final_norm_g, moe_w_router[0], moe_b_router[0],
                      moe_gu.reshape(ne, d, two_ffn), moe_dn.reshape(ne, ffn, d))
```

```python
import functools

import jax
import jax.numpy as jnp
import numpy as np
from jax import lax
from jax.experimental import pallas as pl
from jax.experimental.pallas import tpu as pltpu

F32 = jnp.float32
BF16 = jnp.bfloat16

CHUNK = 128
SGU_GROUPS = 8
RET_HEADS = 4
N_EXPERTS = 8
ROPE_BASE = 10000.0
EPS = 1e-6
LANES = 128
SUBLANES = 8
BF16_TILE_ROWS = 16
VMEM_LIMIT_BYTES = 56 * 1024 * 1024


def _const_spec(shape):
    nd = len(shape)
    return pl.BlockSpec(shape, lambda *_: (0,) * nd, pipeline_mode=pl.Buffered(1))


def _store_slabs(ref, rows, first, count):
    for c in range(SUBLANES):
        ref[pl.ds(first * SUBLANES + c, count, stride=SUBLANES), :] = \
            rows[:, c * LANES:(c + 1) * LANES]


def _load_slabs(ref, first, count):
    return jnp.concatenate(
        [ref[pl.ds(first * SUBLANES + c, count, stride=SUBLANES), :] for c in range(SUBLANES)],
        axis=-1)


def _modnorm(x, g, scale, shift):
    ms = jnp.mean(x * x, axis=-1, keepdims=True)
    return (x * lax.rsqrt(ms + EPS) * g) * (1.0 + scale) + shift


def _adaln_kernel(c_ref, w_ref, b_ref, o_ref):
    sc = jax.nn.silu(c_ref[...])
    o_ref[0] = jnp.dot(sc, w_ref[0], precision=lax.Precision.HIGHEST,
                       preferred_element_type=F32) + b_ref[0]


def _adaln(c, ada_w, ada_b):
    depth, d, six_d = ada_w.shape
    b = c.shape[0]
    tn = 1024
    return pl.pallas_call(
        _adaln_kernel,
        out_shape=jax.ShapeDtypeStruct((depth, b, six_d), F32),
        grid=(depth, six_d // tn),
        in_specs=[
            pl.BlockSpec((b, d), lambda l, j: (0, 0)),
            pl.BlockSpec((1, d, tn), lambda l, j: (l, 0, j)),
            pl.BlockSpec((1, 1, tn), lambda l, j: (l, 0, j)),
        ],
        out_specs=pl.BlockSpec((1, b, tn), lambda l, j: (l, 0, j)),
        compiler_params=pltpu.CompilerParams(
            dimension_semantics=("parallel", "parallel"),
            vmem_limit_bytes=VMEM_LIMIT_BYTES),
        name="adaln",
    )(c, ada_w, ada_b.reshape(depth, 1, six_d))


def _sgu_kernel(x_ref, mod_ref, ng_ref, win_ref, lng_ref, lnb_ref, ws_ref, bs_ref,
                wout_ref, *rest, tm, width, col_block, n_cast):
    cast_in = rest[:n_cast]
    o_ref = rest[n_cast]
    cast_out = rest[n_cast + 1:2 * n_cast + 1]
    u_ref, vn_ref, y_ref = rest[2 * n_cast + 1:]
    for src, dst in zip(cast_in, cast_out):
        dst[...] = src[...].astype(BF16)
    x = x_ref[0]
    mod = mod_ref[0]
    h = _modnorm(x, ng_ref[...], mod[1:2], mod[0:1]).astype(BF16)
    nb = width // col_block
    vs = []
    for j in range(2 * nb):
        z = jnp.dot(h, win_ref[:, j * col_block:(j + 1) * col_block],
                    preferred_element_type=F32)
        z = jax.nn.gelu(z)
        if j < nb:
            u_ref[:, j * col_block:(j + 1) * col_block] = z
        else:
            vs.append(z)
    s1 = sum(jnp.sum(v, axis=-1, keepdims=True) for v in vs)
    mu = s1 * (1.0 / width)
    s2 = sum(jnp.sum((v - mu) * (v - mu), axis=-1, keepdims=True) for v in vs)
    rstd = lax.rsqrt(s2 * (1.0 / width) + EPS)
    for j, v in enumerate(vs):
        sl = slice(j * col_block, (j + 1) * col_block)
        vn_ref[:, sl] = ((v - mu) * rstd * lng_ref[:, sl] + lnb_ref[:, sl]).astype(BF16)
    gd = width // SGU_GROUPS
    row = lax.broadcasted_iota(jnp.int32, (CHUNK, CHUNK), 0)
    col = lax.broadcasted_iota(jnp.int32, (CHUNK, CHUNK), 1)
    causal = row >= col
    for g in range(SGU_GROUPS):
        w = jnp.where(causal, ws_ref[g], jnp.zeros((), BF16))
        cs = slice(g * gd, (g + 1) * gd)
        for c in range(tm // CHUNK):
            rs = slice(c * CHUNK, (c + 1) * CHUNK)
            fv = jnp.dot(w, vn_ref[rs, cs], preferred_element_type=F32) + bs_ref[g]
            y_ref[rs, cs] = (u_ref[rs, cs] * fv).astype(BF16)
    out = jnp.dot(y_ref[...], wout_ref[...], preferred_element_type=F32)
    o_ref[0] = x + mod[2:3] * out


def _cast_blocks(rows, steps):
    for br in range(BF16_TILE_ROWS, rows + 1, BF16_TILE_ROWS):
        if rows % br == 0 and rows // br <= steps:
            return br, rows // br
    raise ValueError(f"no bf16-tile row block casts {rows} rows in {steps} steps")


def _sgu_layer(x, mod, norm_g, w_in, ln_g, ln_b, w_s, b_s, w_out, later_weights, *, tm=512):
    b, s, d = x.shape
    width = w_out.shape[0]
    gd = width // SGU_GROUPS
    bs_full = jnp.broadcast_to(b_s[:, :, None], (SGU_GROUPS, CHUNK, gd))
    steps_j = s // tm
    plans = [_cast_blocks(w.shape[0], b * steps_j) for w in later_weights]

    def cast_spec(w, plan):
        br, nblk = plan
        return pl.BlockSpec((br, w.shape[1]),
                            lambda i, j: (jnp.minimum(i * steps_j + j, nblk - 1), 0))

    cast_specs = [cast_spec(w, p) for w, p in zip(later_weights, plans)]
    kern = functools.partial(_sgu_kernel, tm=tm, width=width, col_block=512,
                             n_cast=len(later_weights))
    outs = pl.pallas_call(
        kern,
        out_shape=[jax.ShapeDtypeStruct((b, s, d), F32)]
        + [jax.ShapeDtypeStruct(w.shape, BF16) for w in later_weights],
        grid=(b, steps_j),
        in_specs=[
            pl.BlockSpec((1, tm, d), lambda i, j: (i, j, 0)),
            pl.BlockSpec((1, 6, d), lambda i, j: (i, 0, 0)),
            _const_spec((1, d)),
            _const_spec((d, 2 * width)),
            _const_spec((1, width)),
            _const_spec((1, width)),
            _const_spec((SGU_GROUPS, CHUNK, CHUNK)),
            _const_spec((SGU_GROUPS, CHUNK, gd)),
            _const_spec((width, d)),
        ] + cast_specs,
        out_specs=[pl.BlockSpec((1, tm, d), lambda i, j: (i, j, 0))] + cast_specs,
        scratch_shapes=[
            pltpu.VMEM((tm, width), F32),
            pltpu.VMEM((tm, width), BF16),
            pltpu.VMEM((tm, width), BF16),
        ],
        compiler_params=pltpu.CompilerParams(
            dimension_semantics=("arbitrary", "arbitrary"),
            vmem_limit_bytes=VMEM_LIMIT_BYTES),
        name="sgu",
    )(x, mod, norm_g.reshape(1, d), w_in.astype(BF16), ln_g.reshape(1, width),
      ln_b.reshape(1, width), w_s.astype(BF16), bs_full, w_out.astype(BF16), *later_weights)
    return outs[0], outs[1:]


def _ffn_kernel(x_ref, mod_ref, ng_ref, wgu_ref, wd_ref, o_ref, *, ffn, col_block):
    x = x_ref[0]
    mod = mod_ref[0]
    h = _modnorm(x, ng_ref[...], mod[4:5], mod[3:4]).astype(BF16)
    acc = jnp.zeros(x.shape, F32)
    for j in range(ffn // col_block):
        g = jnp.dot(h, wgu_ref[:, j * col_block:(j + 1) * col_block],
                    preferred_element_type=F32)
        u = jnp.dot(h, wgu_ref[:, ffn + j * col_block:ffn + (j + 1) * col_block],
                    preferred_element_type=F32)
        a = (jax.nn.silu(g) * u).astype(BF16)
        acc = acc + jnp.dot(a, wd_ref[j * col_block:(j + 1) * col_block, :],
                            preferred_element_type=F32)
    o_ref[0] = x + mod[5:6] * acc


def _ffn_layer(x, mod, norm_g, w_gate_up, w_down, *, tm=1024):
    b, s, d = x.shape
    ffn = w_down.shape[0]
    kern = functools.partial(_ffn_kernel, ffn=ffn, col_block=256)
    return pl.pallas_call(
        kern,
        out_shape=jax.ShapeDtypeStruct((b, s, d), F32),
        grid=(b, s // tm),
        in_specs=[
            pl.BlockSpec((1, tm, d), lambda i, j: (i, j, 0)),
            pl.BlockSpec((1, 6, d), lambda i, j: (i, 0, 0)),
            _const_spec((1, d)),
            _const_spec((d, 2 * ffn)),
            _const_spec((ffn, d)),
        ],
        out_specs=pl.BlockSpec((1, tm, d), lambda i, j: (i, j, 0)),
        compiler_params=pltpu.CompilerParams(
            dimension_semantics=("parallel", "parallel"),
            vmem_limit_bytes=VMEM_LIMIT_BYTES),
        name="ffn",
    )(x, mod, norm_g.reshape(1, d), w_gate_up.astype(BF16), w_down.astype(BF16))


def _retention_kernel(x_ref, pos_ref, mod_ref, ng_ref, invf_ref, win_ref, wout_ref,
                      intra_ref, xi_ref, zeta_ref, o_ref,
                      state_ref, q_ref, qx_ref, k_ref, kz_ref, v_ref, sg_ref, ob_ref,
                      *, tm, rc, dk, dv, decay):
    heads = RET_HEADS
    qk = heads * dk

    @pl.when(pl.program_id(1) == 0)
    def _():
        state_ref[...] = jnp.zeros(state_ref.shape, F32)

    mod = mod_ref[0]
    half = dk // 2

    def rope(p, cos, sin):
        t1, t2 = p[:, :half], p[:, half:]
        return jnp.concatenate([t1 * cos - t2 * sin, t2 * cos + t1 * sin], axis=-1)

    for c in range(tm // rc):
        rs = slice(c * rc, (c + 1) * rc)
        x = x_ref[0, rs, :]
        h = _modnorm(x, ng_ref[...], mod[1:2], mod[0:1]).astype(BF16)
        ang = pos_ref[0, rs, :].astype(F32) * invf_ref[...]
        cos = jnp.cos(ang)
        sin = jnp.sin(ang)
        for hd in range(heads):
            cs = slice(hd * dk, (hd + 1) * dk)
            q = rope(jnp.dot(h, win_ref[:, cs], preferred_element_type=F32), cos, sin)
            q_ref[rs, cs] = q.astype(BF16)
            qx_ref[rs, cs] = (q * xi_ref[:, cs]).astype(BF16)
            k = rope(jnp.dot(h, win_ref[:, qk + hd * dk:qk + (hd + 1) * dk],
                             preferred_element_type=F32), cos, sin) * (dk ** -0.5)
            k_ref[rs, cs] = k.astype(BF16)
            kz_ref[rs, cs] = (k * zeta_ref[:, cs]).astype(BF16)
        for hd in range(heads):
            cs = slice(hd * dv, (hd + 1) * dv)
            v_ref[rs, cs] = jnp.dot(h, win_ref[:, 2 * qk + hd * dv:2 * qk + (hd + 1) * dv],
                                    preferred_element_type=F32).astype(BF16)
            g = jnp.dot(h, win_ref[:, 2 * qk + heads * dv + hd * dv:
                                   2 * qk + heads * dv + (hd + 1) * dv],
                        preferred_element_type=F32)
            sg_ref[rs, cs] = jax.nn.silu(g).astype(BF16)
        for hd in range(heads):
            ks = slice(hd * dk, (hd + 1) * dk)
            vs = slice(hd * dv, (hd + 1) * dv)
            vh = v_ref[rs, vs]
            scores = lax.dot_general(q_ref[rs, ks], k_ref[rs, ks], (((1,), (1,)), ((), ())),
                                     preferred_element_type=F32)
            scores = (scores * intra_ref[hd]).astype(BF16)
            st = state_ref[hd]
            o = (jnp.dot(scores, vh, preferred_element_type=F32)
                 + jnp.dot(qx_ref[rs, ks], st.astype(BF16), preferred_element_type=F32))
            state_ref[hd] = st * decay[hd] + lax.dot_general(
                kz_ref[rs, ks], vh, (((0,), (0,)), ((), ())), preferred_element_type=F32)
            on = o * lax.rsqrt(jnp.mean(o * o, axis=-1, keepdims=True) + EPS)
            ob_ref[rs, vs] = on.astype(BF16) * sg_ref[rs, vs]
        out = jnp.dot(ob_ref[rs, :], wout_ref[...], preferred_element_type=F32)
        o_ref[0, rs, :] = x + mod[2:3] * out


def _retention_layer(x, positions, mod, norm_g, w_in, w_out, *, tm=512, rc=256):
    b, s, d = x.shape
    heads = RET_HEADS
    dk = d // heads
    dv = w_out.shape[0] // heads
    qk = heads * dk
    vw = heads * dv
    log_gamma = jnp.log1p(-jnp.exp2(-5.0 - jnp.arange(heads, dtype=F32)))
    idx = jnp.arange(rc, dtype=F32)
    rel = idx[:, None] - idx[None, :]
    intra = jnp.where(rel >= 0, jnp.exp(log_gamma[:, None, None] * jnp.maximum(rel, 0.0)), 0.0)
    xi = jnp.exp(log_gamma[:, None] * (idx + 1.0))
    zeta = jnp.exp(log_gamma[:, None] * (rc - 1.0 - idx))
    xi_t = jnp.repeat(xi.T, dk, axis=1)
    zeta_t = jnp.repeat(zeta.T, dk, axis=1)
    decay = tuple(float(np.exp(np.log1p(-2.0 ** (-5.0 - hd)) * rc)) for hd in range(heads))
    inv_freq = 1.0 / (ROPE_BASE ** (jnp.arange(0, dk, 2, dtype=F32) / dk))
    kern = functools.partial(_retention_kernel, tm=tm, rc=rc, dk=dk, dv=dv, decay=decay)
    return pl.pallas_call(
        kern,
        out_shape=jax.ShapeDtypeStruct((b, s, d), F32),
        grid=(b, s // tm),
        in_specs=[
            pl.BlockSpec((1, tm, d), lambda i, j: (i, j, 0)),
            pl.BlockSpec((1, tm, 1), lambda i, j: (i, j, 0)),
            pl.BlockSpec((1, 6, d), lambda i, j: (i, 0, 0)),
            _const_spec((1, d)),
            _const_spec((1, dk // 2)),
            _const_spec((d, 2 * qk + 2 * vw)),
            _const_spec((vw, d)),
            _const_spec((heads, rc, rc)),
            _const_spec((rc, qk)),
            _const_spec((rc, qk)),
        ],
        out_specs=pl.BlockSpec((1, tm, d), lambda i, j: (i, j, 0)),
        scratch_shapes=[
            pltpu.VMEM((heads, dk, dv), F32),
            pltpu.VMEM((tm, qk), BF16),
            pltpu.VMEM((tm, qk), BF16),
            pltpu.VMEM((tm, qk), BF16),
            pltpu.VMEM((tm, qk), BF16),
            pltpu.VMEM((tm, vw), BF16),
            pltpu.VMEM((tm, vw), BF16),
            pltpu.VMEM((tm, vw), BF16),
        ],
        compiler_params=pltpu.CompilerParams(
            dimension_semantics=("parallel", "arbitrary"),
            vmem_limit_bytes=VMEM_LIMIT_BYTES),
        name="retention",
    )(x, positions.reshape(b, s, 1), mod, norm_g.reshape(1, d), inv_freq.reshape(1, dk // 2),
      w_in.astype(BF16), w_out.astype(BF16), intra, xi_t, zeta_t)


META_E1, META_E2, META_R1, META_R2, META_W1, META_W2 = range(6)
META_ROWS = 8


def _router_kernel(x_ref, mod_ref, ng_ref, wr_ref, br_ref, meta_ref, meta_t_ref, cnt_ref,
                   carry_ref):
    @pl.when((pl.program_id(0) == 0) & (pl.program_id(1) == 0))
    def _():
        carry_ref[...] = jnp.zeros(carry_ref.shape, F32)

    x = x_ref[0]
    mod = mod_ref[0]
    h = _modnorm(x, ng_ref[...], mod[4:5], mod[3:4])
    tm = h.shape[0]
    lane = lax.broadcasted_iota(jnp.int32, (tm, LANES), 1)
    neg = jnp.float32(-jnp.inf)
    logits = jnp.full((tm, LANES), neg, F32)
    for e in range(N_EXPERTS):
        le = jnp.sum(h * wr_ref[e:e + 1, :], axis=-1, keepdims=True)
        logits = jnp.where(lane == e, le, logits)
    logits = logits + br_ref[...]
    m1 = jnp.max(logits, axis=-1, keepdims=True)
    i1 = jnp.min(jnp.where(logits == m1, lane, LANES), axis=-1, keepdims=True)
    rest = jnp.where(lane == i1, neg, logits)
    m2 = jnp.max(rest, axis=-1, keepdims=True)
    i2 = jnp.min(jnp.where(rest == m2, lane, LANES), axis=-1, keepdims=True)
    e2 = jnp.exp(m2 - m1)
    w1 = 1.0 / (1.0 + e2)
    w2 = e2 / (1.0 + e2)
    sel = jnp.where((lane == i1) | (lane == i2), 1.0, 0.0)
    row = lax.broadcasted_iota(jnp.int32, (tm, tm), 0)
    col = lax.broadcasted_iota(jnp.int32, (tm, tm), 1)
    tri = jnp.where(row > col, 1.0, 0.0).astype(BF16)
    before = jnp.dot(tri, sel.astype(BF16), preferred_element_type=F32) + carry_ref[...]
    r1 = jnp.sum(jnp.where(lane == i1, before, 0.0), axis=-1, keepdims=True)
    r2 = jnp.sum(jnp.where(lane == i2, before, 0.0), axis=-1, keepdims=True)
    carry_ref[...] += jnp.sum(sel, axis=0, keepdims=True)
    cnt_ref[...] = carry_ref[...]
    meta = jnp.zeros(logits.shape, F32)
    for k, val in ((META_E1, i1.astype(F32)), (META_E2, i2.astype(F32)), (META_R1, r1),
                   (META_R2, r2), (META_W1, w1), (META_W2, w2)):
        meta = jnp.where(lane == k, val, meta)
    meta_ref[0] = meta
    meta_t_ref[...] = meta.T[:META_ROWS, :]


def _router(x, mod, norm_g, w_router, b_router, *, tm=512):
    b, s, d = x.shape
    ne = w_router.shape[1]
    assert ne == N_EXPERTS
    wr = w_router.T
    br = jnp.zeros((1, LANES), F32).at[0, :ne].set(b_router)
    return pl.pallas_call(
        _router_kernel,
        out_shape=(jax.ShapeDtypeStruct((b, s, LANES), F32),
                   jax.ShapeDtypeStruct((META_ROWS, b * s), F32),
                   jax.ShapeDtypeStruct((1, LANES), F32)),
        grid=(b, s // tm),
        in_specs=[
            pl.BlockSpec((1, tm, d), lambda i, j: (i, j, 0)),
            pl.BlockSpec((1, 6, d), lambda i, j: (i, 0, 0)),
            _const_spec((1, d)),
            _const_spec((ne, d)),
            _const_spec((1, LANES)),
        ],
        out_specs=(pl.BlockSpec((1, tm, LANES), lambda i, j: (i, j, 0)),
                   pl.BlockSpec((META_ROWS, tm), lambda i, j: (0, i * (s // tm) + j)),
                   pl.BlockSpec((1, LANES), lambda i, j: (0, 0))),
        scratch_shapes=[pltpu.VMEM((1, LANES), F32)],
        compiler_params=pltpu.CompilerParams(
            dimension_semantics=("arbitrary", "arbitrary"),
            vmem_limit_bytes=VMEM_LIMIT_BYTES),
        name="router",
    )(x, mod, norm_g.reshape(1, d), wr, br)


N_ZERO_SLOTS = 2 * N_EXPERTS


def _dispatch_kernel(info_ref, pos_ref, x_ref, mod_ref, ng_ref, hs_hbm, hbuf, zeros_ref,
                     sem, zsem, *, td, tr):
    i = pl.program_id(0)
    slot = lax.rem(i, 2)

    def zero_copy(k):
        start = pl.multiple_of(info_ref[N_ZERO_SLOTS + k] * (tr * SUBLANES), tr * SUBLANES)
        return pltpu.make_async_copy(zeros_ref, hs_hbm.at[pl.ds(start, tr * SUBLANES)], zsem)

    @pl.when(i == 0)
    def _():
        zeros_ref[...] = jnp.zeros(zeros_ref.shape, F32)
        for k in range(N_ZERO_SLOTS):
            @pl.when(info_ref[k] == 1)
            def _():
                zero_copy(k).start()
        for k in range(N_ZERO_SLOTS):
            @pl.when(info_ref[k] == 1)
            def _():
                zero_copy(k).wait()

    mod = mod_ref[0]
    h = _modnorm(x_ref[...], ng_ref[...], mod[4:5], mod[3:4])
    _store_slabs(hbuf.at[slot], h, 0, td)

    def hs_slab(p8):
        return hs_hbm.at[pl.ds(pl.multiple_of(p8, SUBLANES), SUBLANES)]

    for t in range(td):
        src = hbuf.at[slot, pl.ds(t * SUBLANES, SUBLANES)]
        pltpu.make_async_copy(src, hs_slab(pos_ref[0, 0, t]), sem.at[slot]).start(priority=0)
        pltpu.make_async_copy(src, hs_slab(pos_ref[0, 0, td + t]),
                              sem.at[slot]).start(priority=1)

    def wait_rows(s):
        n_sub = 2 * td * SUBLANES
        pltpu.make_async_copy(hs_hbm.at[pl.ds(0, n_sub)], hs_hbm.at[pl.ds(0, n_sub)],
                              sem.at[s]).wait()

    @pl.when(i > 0)
    def _():
        wait_rows(1 - slot)

    @pl.when(i == pl.num_programs(0) - 1)
    def _():
        wait_rows(slot)


def _dispatch(x, mod, norm_g, pos, zero_info, *, n_rows, seq, td, tr):
    n, d = x.shape
    assert d == SUBLANES * LANES, "a row must fill exactly one (8, 128) slab"
    kern = functools.partial(_dispatch_kernel, td=td, tr=tr)
    return pl.pallas_call(
        kern,
        out_shape=jax.ShapeDtypeStruct((n_rows * SUBLANES, LANES), F32),
        grid_spec=pltpu.PrefetchScalarGridSpec(
            num_scalar_prefetch=1,
            grid=(n // td,),
            in_specs=[
                pl.BlockSpec((1, 1, 2 * td), lambda i, info: (i, 0, 0),
                             memory_space=pltpu.SMEM),
                pl.BlockSpec((td, d), lambda i, info: (i, 0)),
                pl.BlockSpec((1, 6, d), lambda i, info: ((i * td) // seq, 0, 0)),
                pl.BlockSpec((1, d), lambda i, info: (0, 0)),
            ],
            out_specs=pl.BlockSpec(memory_space=pl.ANY),
            scratch_shapes=[pltpu.VMEM((2, td * SUBLANES, LANES), F32),
                            pltpu.VMEM((tr * SUBLANES, LANES), F32),
                            pltpu.SemaphoreType.DMA((2,)), pltpu.SemaphoreType.DMA],
        ),
        compiler_params=pltpu.CompilerParams(
            dimension_semantics=("arbitrary",), vmem_limit_bytes=VMEM_LIMIT_BYTES),
        name="dispatch",
    )(zero_info, pos, x, mod, norm_g.reshape(1, d))


def _experts_kernel(te_ref, nu_ref, hs_ref, wgu_ref, wd_ref, y_ref, *, ffn, col_block, tr):
    i = pl.program_id(0)

    @pl.when(i < nu_ref[0])
    def _():
        h = _load_slabs(hs_ref, 0, tr).astype(BF16)
        y = jnp.zeros((tr, SUBLANES * LANES), F32)
        for j in range(ffn // col_block):
            g = jnp.dot(h, wgu_ref[0, :, j * col_block:(j + 1) * col_block],
                        preferred_element_type=F32)
            u = jnp.dot(h, wgu_ref[0, :, ffn + j * col_block:ffn + (j + 1) * col_block],
                        preferred_element_type=F32)
            a = (jax.nn.silu(g) * u).astype(BF16)
            y = y + jnp.dot(a, wd_ref[0, j * col_block:(j + 1) * col_block, :],
                            preferred_element_type=F32)
        _store_slabs(y_ref, y, 0, tr)

    @pl.when(i >= nu_ref[0])
    def _():
        y_ref[...] = jnp.zeros(y_ref.shape, F32)


def _experts(hs, tile_expert, n_used, w_gate_up, w_down, *, tr):
    ne, ffn, d = w_down.shape
    n_tiles = hs.shape[0] // (tr * SUBLANES)
    kern = functools.partial(_experts_kernel, ffn=ffn, col_block=256, tr=tr)
    return pl.pallas_call(
        kern,
        out_shape=jax.ShapeDtypeStruct(hs.shape, F32),
        grid_spec=pltpu.PrefetchScalarGridSpec(
            num_scalar_prefetch=2,
            grid=(n_tiles,),
            in_specs=[
                pl.BlockSpec((tr * SUBLANES, LANES),
                             lambda i, te, nu: (jnp.minimum(i, nu[0] - 1), 0)),
                pl.BlockSpec((1, d, 2 * ffn), lambda i, te, nu: (te[i], 0, 0)),
                pl.BlockSpec((1, ffn, d), lambda i, te, nu: (te[i], 0, 0)),
            ],
            out_specs=pl.BlockSpec((tr * SUBLANES, LANES), lambda i, te, nu: (i, 0)),
        ),
        compiler_params=pltpu.CompilerParams(
            dimension_semantics=("arbitrary",), vmem_limit_bytes=VMEM_LIMIT_BYTES),
        name="experts",
    )(tile_expert, n_used, hs, w_gate_up.astype(BF16), w_down.astype(BF16))


def _combine_kernel(posc_ref, posn_ref, x_ref, meta_ref, mod_ref, fg_ref, y_hbm, o_ref,
                    ybuf, sem, *, tc):
    i = pl.program_id(0)
    slot = lax.rem(i, 2)

    def issue(pos_ref, to_slot):
        for t in range(2 * tc):
            src = y_hbm.at[pl.ds(pl.multiple_of(pos_ref[0, 0, t], SUBLANES), SUBLANES)]
            pltpu.make_async_copy(src, ybuf.at[to_slot, pl.ds(t * SUBLANES, SUBLANES)],
                                  sem.at[to_slot]).start(priority=t % 2)

    @pl.when(i == 0)
    def _():
        issue(posc_ref, 0)

    @pl.when(i + 1 < pl.num_programs(0))
    def _():
        issue(posn_ref, 1 - slot)

    pltpu.make_async_copy(y_hbm.at[pl.ds(0, 2 * tc * SUBLANES)], ybuf.at[slot],
                          sem.at[slot]).wait()
    meta = meta_ref[...]
    lane = lax.broadcasted_iota(jnp.int32, meta.shape, 1)
    w1 = jnp.sum(jnp.where(lane == META_W1, meta, 0.0), axis=-1, keepdims=True)
    w2 = jnp.sum(jnp.where(lane == META_W2, meta, 0.0), axis=-1, keepdims=True)
    y = w1 * _load_slabs(ybuf.at[slot], 0, tc) + w2 * _load_slabs(ybuf.at[slot], tc, tc)
    xo = x_ref[...] + mod_ref[0][5:6] * y
    ms = jnp.mean(xo * xo, axis=-1, keepdims=True)
    o_ref[...] = xo * lax.rsqrt(ms + EPS) * fg_ref[...]


def _combine(x, meta, mod, final_g, y_sorted, pos, *, seq, tc):
    n, d = x.shape
    nblk = n // tc
    kern = functools.partial(_combine_kernel, tc=tc)
    smem_pos = lambda f: pl.BlockSpec((1, 1, 2 * tc), f, memory_space=pltpu.SMEM)
    return pl.pallas_call(
        kern,
        out_shape=jax.ShapeDtypeStruct((n, d), F32),
        grid=(nblk,),
        in_specs=[
            smem_pos(lambda i: (i, 0, 0)),
            smem_pos(lambda i: (jnp.minimum(i + 1, nblk - 1), 0, 0)),
            pl.BlockSpec((tc, d), lambda i: (i, 0)),
            pl.BlockSpec((tc, LANES), lambda i: (i, 0)),
            pl.BlockSpec((1, 6, d), lambda i: ((i * tc) // seq, 0, 0)),
            _const_spec((1, d)),
            pl.BlockSpec(memory_space=pl.ANY),
        ],
        out_specs=pl.BlockSpec((tc, d), lambda i: (i, 0)),
        scratch_shapes=[pltpu.VMEM((2, 2 * tc * SUBLANES, LANES), F32),
                        pltpu.SemaphoreType.DMA((2,))],
        compiler_params=pltpu.CompilerParams(
            dimension_semantics=("arbitrary",), vmem_limit_bytes=VMEM_LIMIT_BYTES),
        name="combine",
    )(pos, pos, x, meta, mod, final_g.reshape(1, d), y_sorted)


def _moe_layer(x, mod, norm_g, final_g, w_router, b_router, w_gate_up, w_down,
               *, tr=512, td=1024, tc=256):
    b, s, d = x.shape
    n = b * s
    ne = w_router.shape[1]
    meta, meta_t, cnt = _router(x, mod, norm_g, w_router, b_router)
    meta = meta.reshape(n, LANES)
    e1 = meta_t[META_E1].astype(jnp.int32)
    e2 = meta_t[META_E2].astype(jnp.int32)
    r1 = meta_t[META_R1].astype(jnp.int32)
    r2 = meta_t[META_R2].astype(jnp.int32)
    counts = cnt[0, :ne].astype(jnp.int32)
    tiles_per = (counts + tr - 1) // tr
    tile_end = jnp.cumsum(tiles_per)
    n_used = tile_end[-1]
    group_start = (tile_end - tiles_per) * tr
    pos1 = group_start[e1] + r1
    pos2 = group_start[e2] + r2
    nt = 2 * n // tr + ne
    tile_ids = jnp.arange(nt, dtype=jnp.int32)
    te = jnp.sum((tile_ids[:, None] >= tile_end[None, :]).astype(jnp.int32), axis=1)
    te = jnp.minimum(te, ne - 1)
    te = jnp.where(tile_ids < n_used, te, te[n_used - 1]).astype(jnp.int32)
    zero_tiles = jnp.concatenate([tile_end - 1, n_used + jnp.arange(ne, dtype=jnp.int32)])
    zero_valid = jnp.concatenate([(counts % tr) != 0, n_used + jnp.arange(ne) < nt])
    zero_info = jnp.concatenate([zero_valid.astype(jnp.int32),
                                 jnp.clip(zero_tiles, 0, nt - 1).astype(jnp.int32)])

    def blocked(t):
        return SUBLANES * jnp.concatenate(
            [pos1.reshape(n // t, 1, t), pos2.reshape(n // t, 1, t)], axis=-1)

    hs = _dispatch(x.reshape(n, d), mod, norm_g, blocked(td), zero_info, n_rows=nt * tr,
                   seq=s, td=td, tr=tr)
    y_sorted = _experts(hs, te, n_used.reshape(1).astype(jnp.int32), w_gate_up, w_down, tr=tr)
    out = _combine(x.reshape(n, d), meta, mod, final_g, y_sorted, blocked(tc), seq=s, tc=tc)
    return out.reshape(b, s, d)


def kernel(x, c, positions, ada_w, ada_b, mix_norm_g, ffn_norm_g, sgu_w_in, sgu_ln_g, sgu_ln_b,
           sgu_w_s, sgu_b_s, sgu_w_out, ffn_w_gate_up, ffn_w_down, ret_w_in, ret_w_out,
           moe_w_router, moe_b_router, moe_w_gate_up, moe_w_down, final_norm_g):
    b, s, d = x.shape
    depth = ada_w.shape[0]
    assert depth == 2, "layer 0 = SGU + SwiGLU, layer 1 = retention + MoE"
    mod = _adaln(c, ada_w, ada_b).reshape(depth, b, 6, d)
    ne, _, two_ffn = moe_w_gate_up[0].shape
    ffn = moe_w_down[0].shape[1]
    later = [ffn_w_gate_up[0], ffn_w_down[0], ret_w_in[0], ret_w_out[0],
             moe_w_gate_up[0].reshape(ne * d, two_ffn), moe_w_down[0].reshape(ne * ffn, d)]
    x, (ffn_gu, ffn_dn, ret_in, ret_out, moe_gu, moe_dn) = _sgu_layer(
        x, mod[0], mix_norm_g[0], sgu_w_in[0], sgu_ln_g[0], sgu_ln_b[0], sgu_w_s[0],
        sgu_b_s[0], sgu_w_out[0], later)
    x = _ffn_layer(x, mod[0], ffn_norm_g[0], ffn_gu, ffn_dn)
    x = _retention_layer(x, positions, mod[1], mix_norm_g[1], ret_in, ret_out)
    return _moe_layer(x, mod[1], ffn_norm_g[1], final_norm_g, moe_w_router[0], moe_b_router[0],
                      moe_gu.reshape(ne, d, two_ffn), moe_dn.reshape(ne, ffn, d))
```

```python
import functools

import jax
import jax.numpy as jnp
import numpy as np
from jax import lax
from jax.experimental import pallas as pl
from jax.experimental.pallas import tpu as pltpu

F32 = jnp.float32
BF16 = jnp.bfloat16

CHUNK = 128
SGU_GROUPS = 8
RET_HEADS = 4
N_EXPERTS = 8
ROPE_BASE = 10000.0
EPS = 1e-6
LANES = 128
SUBLANES = 8
BF16_TILE_ROWS = 16
VMEM_LIMIT_BYTES = 56 * 1024 * 1024


def _const_spec(shape):
    nd = len(shape)
    return pl.BlockSpec(shape, lambda *_: (0,) * nd, pipeline_mode=pl.Buffered(1))


def _store_slabs(ref, rows, first, count):
    for c in range(SUBLANES):
        ref[pl.ds(first * SUBLANES + c, count, stride=SUBLANES), :] = \
            rows[:, c * LANES:(c + 1) * LANES]


def _load_slabs(ref, first, count):
    return jnp.concatenate(
        [ref[pl.ds(first * SUBLANES + c, count, stride=SUBLANES), :] for c in range(SUBLANES)],
        axis=-1)


def _modnorm(x, g, scale, shift):
    ms = jnp.mean(x * x, axis=-1, keepdims=True)
    return (x * lax.rsqrt(ms + EPS) * g) * (1.0 + scale) + shift


def _adaln_kernel(c_ref, w_ref, b_ref, o_ref):
    sc = jax.nn.silu(c_ref[...])
    o_ref[0] = jnp.dot(sc, w_ref[0], precision=lax.Precision.HIGHEST,
                       preferred_element_type=F32) + b_ref[0]


def _adaln(c, ada_w, ada_b):
    depth, d, six_d = ada_w.shape
    b = c.shape[0]
    tn = 1024
    return pl.pallas_call(
        _adaln_kernel,
        out_shape=jax.ShapeDtypeStruct((depth, b, six_d), F32),
        grid=(depth, six_d // tn),
        in_specs=[
            pl.BlockSpec((b, d), lambda l, j: (0, 0)),
            pl.BlockSpec((1, d, tn), lambda l, j: (l, 0, j)),
            pl.BlockSpec((1, 1, tn), lambda l, j: (l, 0, j)),
        ],
        out_specs=pl.BlockSpec((1, b, tn), lambda l, j: (l, 0, j)),
        compiler_params=pltpu.CompilerParams(
            dimension_semantics=("parallel", "parallel"),
            vmem_limit_bytes=VMEM_LIMIT_BYTES),
        name="adaln",
    )(c, ada_w, ada_b.reshape(depth, 1, six_d))


def _sgu_kernel(x_ref, mod_ref, ng_ref, win_ref, lng_ref, lnb_ref, ws_ref, bs_ref,
                wout_ref, *rest, tm, width, col_block, n_cast):
    cast_in = rest[:n_cast]
    o_ref = rest[n_cast]
    cast_out = rest[n_cast + 1:2 * n_cast + 1]
    vn_ref, y_ref = rest[2 * n_cast + 1:]
    for src, dst in zip(cast_in, cast_out):
        dst[...] = src[...].astype(BF16)
    x = x_ref[0]
    mod = mod_ref[0]
    h = _modnorm(x, ng_ref[...], mod[1:2], mod[0:1]).astype(BF16)
    nb = width // col_block

    def in_block(j):
        return jax.nn.gelu(jnp.dot(h, win_ref[:, j * col_block:(j + 1) * col_block],
                                   preferred_element_type=F32))

    vs = [in_block(j) for j in range(nb, 2 * nb)]
    s1 = sum(jnp.sum(v, axis=-1, keepdims=True) for v in vs)
    mu = s1 * (1.0 / width)
    s2 = sum(jnp.sum((v - mu) * (v - mu), axis=-1, keepdims=True) for v in vs)
    rstd = lax.rsqrt(s2 * (1.0 / width) + EPS)
    for j, v in enumerate(vs):
        sl = slice(j * col_block, (j + 1) * col_block)
        vn_ref[:, sl] = ((v - mu) * rstd * lng_ref[:, sl] + lnb_ref[:, sl]).astype(BF16)
    gd = width // SGU_GROUPS
    row = lax.broadcasted_iota(jnp.int32, (CHUNK, CHUNK), 0)
    col = lax.broadcasted_iota(jnp.int32, (CHUNK, CHUNK), 1)
    causal = row >= col
    groups_per_block = col_block // gd
    out = jnp.zeros(x.shape, F32)
    for j in range(nb):
        u = in_block(j)
        bs = slice(j * col_block, (j + 1) * col_block)
        for gg in range(groups_per_block):
            g = j * groups_per_block + gg
            w = jnp.where(causal, ws_ref[g], jnp.zeros((), BF16))
            cs = slice(g * gd, (g + 1) * gd)
            for c in range(tm // CHUNK):
                rs = slice(c * CHUNK, (c + 1) * CHUNK)
                fv = jnp.dot(w, vn_ref[rs, cs], preferred_element_type=F32) + bs_ref[g]
                y_ref[rs, cs] = (u[rs, gg * gd:(gg + 1) * gd] * fv).astype(BF16)
        out = out + jnp.dot(y_ref[:, bs], wout_ref[bs, :], preferred_element_type=F32)
    o_ref[0] = x + mod[2:3] * out


def _cast_blocks(rows, steps):
    for br in range(BF16_TILE_ROWS, rows + 1, BF16_TILE_ROWS):
        if rows % br == 0 and rows // br <= steps:
            return br, rows // br
    raise ValueError(f"no bf16-tile row block casts {rows} rows in {steps} steps")


def _sgu_layer(x, mod, norm_g, w_in, ln_g, ln_b, w_s, b_s, w_out, later_weights, *, tm=512):
    b, s, d = x.shape
    width = w_out.shape[0]
    gd = width // SGU_GROUPS
    bs_full = jnp.broadcast_to(b_s[:, :, None], (SGU_GROUPS, CHUNK, gd))
    steps_j = s // tm
    plans = [_cast_blocks(w.shape[0], b * steps_j) for w in later_weights]

    def cast_spec(w, plan):
        br, nblk = plan
        return pl.BlockSpec((br, w.shape[1]),
                            lambda i, j: (jnp.minimum(i * steps_j + j, nblk - 1), 0))

    cast_specs = [cast_spec(w, p) for w, p in zip(later_weights, plans)]
    kern = functools.partial(_sgu_kernel, tm=tm, width=width, col_block=512,
                             n_cast=len(later_weights))
    outs = pl.pallas_call(
        kern,
        out_shape=[jax.ShapeDtypeStruct((b, s, d), F32)]
        + [jax.ShapeDtypeStruct(w.shape, BF16) for w in later_weights],
        grid=(b, steps_j),
        in_specs=[
            pl.BlockSpec((1, tm, d), lambda i, j: (i, j, 0)),
            pl.BlockSpec((1, 6, d), lambda i, j: (i, 0, 0)),
            _const_spec((1, d)),
            _const_spec((d, 2 * width)),
            _const_spec((1, width)),
            _const_spec((1, width)),
            _const_spec((SGU_GROUPS, CHUNK, CHUNK)),
            _const_spec((SGU_GROUPS, CHUNK, gd)),
            _const_spec((width, d)),
        ] + cast_specs,
        out_specs=[pl.BlockSpec((1, tm, d), lambda i, j: (i, j, 0))] + cast_specs,
        scratch_shapes=[
            pltpu.VMEM((tm, width), BF16),
            pltpu.VMEM((tm, width), BF16),
        ],
        compiler_params=pltpu.CompilerParams(
            dimension_semantics=("arbitrary", "arbitrary"),
            vmem_limit_bytes=VMEM_LIMIT_BYTES),
        name="sgu",
    )(x, mod, norm_g.reshape(1, d), w_in.astype(BF16), ln_g.reshape(1, width),
      ln_b.reshape(1, width), w_s.astype(BF16), bs_full, w_out.astype(BF16), *later_weights)
    return outs[0], outs[1:]


def _ffn_kernel(x_ref, mod_ref, ng_ref, wgu_ref, wd_ref, o_ref, *, ffn, col_block):
    x = x_ref[0]
    mod = mod_ref[0]
    h = _modnorm(x, ng_ref[...], mod[4:5], mod[3:4]).astype(BF16)
    acc = jnp.zeros(x.shape, F32)
    for j in range(ffn // col_block):
        g = jnp.dot(h, wgu_ref[:, j * col_block:(j + 1) * col_block],
                    preferred_element_type=F32)
        u = jnp.dot(h, wgu_ref[:, ffn + j * col_block:ffn + (j + 1) * col_block],
                    preferred_element_type=F32)
        a = (jax.nn.silu(g) * u).astype(BF16)
        acc = acc + jnp.dot(a, wd_ref[j * col_block:(j + 1) * col_block, :],
                            preferred_element_type=F32)
    o_ref[0] = x + mod[5:6] * acc


def _ffn_layer(x, mod, norm_g, w_gate_up, w_down, *, tm=1024):
    b, s, d = x.shape
    ffn = w_down.shape[0]
    kern = functools.partial(_ffn_kernel, ffn=ffn, col_block=256)
    return pl.pallas_call(
        kern,
        out_shape=jax.ShapeDtypeStruct((b, s, d), F32),
        grid=(b, s // tm),
        in_specs=[
            pl.BlockSpec((1, tm, d), lambda i, j: (i, j, 0)),
            pl.BlockSpec((1, 6, d), lambda i, j: (i, 0, 0)),
            _const_spec((1, d)),
            _const_spec((d, 2 * ffn)),
            _const_spec((ffn, d)),
        ],
        out_specs=pl.BlockSpec((1, tm, d), lambda i, j: (i, j, 0)),
        compiler_params=pltpu.CompilerParams(
            dimension_semantics=("parallel", "parallel"),
            vmem_limit_bytes=VMEM_LIMIT_BYTES),
        name="ffn",
    )(x, mod, norm_g.reshape(1, d), w_gate_up.astype(BF16), w_down.astype(BF16))


def _retention_kernel(x_ref, pos_ref, mod_ref, ng_ref, invf_ref, win_ref, wout_ref,
                      intra_ref, xi_ref, zeta_ref, o_ref,
                      state_ref, q_ref, qx_ref, k_ref, kz_ref, v_ref, sg_ref, ob_ref,
                      *, tm, rc, dk, dv, decay):
    heads = RET_HEADS
    qk = heads * dk

    @pl.when(pl.program_id(1) == 0)
    def _():
        state_ref[...] = jnp.zeros(state_ref.shape, F32)

    mod = mod_ref[0]
    half = dk // 2

    def rope(p, cos, sin):
        t1, t2 = p[:, :half], p[:, half:]
        return jnp.concatenate([t1 * cos - t2 * sin, t2 * cos + t1 * sin], axis=-1)

    for c in range(tm // rc):
        rs = slice(c * rc, (c + 1) * rc)
        x = x_ref[0, rs, :]
        h = _modnorm(x, ng_ref[...], mod[1:2], mod[0:1]).astype(BF16)
        ang = pos_ref[0, rs, :].astype(F32) * invf_ref[...]
        cos = jnp.cos(ang)
        sin = jnp.sin(ang)
        for hd in range(heads):
            cs = slice(hd * dk, (hd + 1) * dk)
            q = rope(jnp.dot(h, win_ref[:, cs], preferred_element_type=F32), cos, sin)
            q_ref[rs, cs] = q.astype(BF16)
            qx_ref[rs, cs] = (q * xi_ref[:, cs]).astype(BF16)
            k = rope(jnp.dot(h, win_ref[:, qk + hd * dk:qk + (hd + 1) * dk],
                             preferred_element_type=F32), cos, sin) * (dk ** -0.5)
            k_ref[rs, cs] = k.astype(BF16)
            kz_ref[rs, cs] = (k * zeta_ref[:, cs]).astype(BF16)
        for hd in range(heads):
            cs = slice(hd * dv, (hd + 1) * dv)
            v_ref[rs, cs] = jnp.dot(h, win_ref[:, 2 * qk + hd * dv:2 * qk + (hd + 1) * dv],
                                    preferred_element_type=F32).astype(BF16)
            g = jnp.dot(h, win_ref[:, 2 * qk + heads * dv + hd * dv:
                                   2 * qk + heads * dv + (hd + 1) * dv],
                        preferred_element_type=F32)
            sg_ref[rs, cs] = jax.nn.silu(g).astype(BF16)
        for hd in range(heads):
            ks = slice(hd * dk, (hd + 1) * dk)
            vs = slice(hd * dv, (hd + 1) * dv)
            vh = v_ref[rs, vs]
            scores = lax.dot_general(q_ref[rs, ks], k_ref[rs, ks], (((1,), (1,)), ((), ())),
                                     preferred_element_type=F32)
            scores = (scores * intra_ref[hd]).astype(BF16)
            st = state_ref[hd]
            o = (jnp.dot(scores, vh, preferred_element_type=F32)
                 + jnp.dot(qx_ref[rs, ks], st.astype(BF16), preferred_element_type=F32))
            state_ref[hd] = st * decay[hd] + lax.dot_general(
                kz_ref[rs, ks], vh, (((0,), (0,)), ((), ())), preferred_element_type=F32)
            on = o * lax.rsqrt(jnp.mean(o * o, axis=-1, keepdims=True) + EPS)
            ob_ref[rs, vs] = on.astype(BF16) * sg_ref[rs, vs]
        out = jnp.dot(ob_ref[rs, :], wout_ref[...], preferred_element_type=F32)
        o_ref[0, rs, :] = x + mod[2:3] * out


def _retention_layer(x, positions, mod, norm_g, w_in, w_out, *, tm=512, rc=256):
    b, s, d = x.shape
    heads = RET_HEADS
    dk = d // heads
    dv = w_out.shape[0] // heads
    qk = heads * dk
    vw = heads * dv
    log_gamma = jnp.log1p(-jnp.exp2(-5.0 - jnp.arange(heads, dtype=F32)))
    idx = jnp.arange(rc, dtype=F32)
    rel = idx[:, None] - idx[None, :]
    intra = jnp.where(rel >= 0, jnp.exp(log_gamma[:, None, None] * jnp.maximum(rel, 0.0)), 0.0)
    xi = jnp.exp(log_gamma[:, None] * (idx + 1.0))
    zeta = jnp.exp(log_gamma[:, None] * (rc - 1.0 - idx))
    xi_t = jnp.repeat(xi.T, dk, axis=1)
    zeta_t = jnp.repeat(zeta.T, dk, axis=1)
    decay = tuple(float(np.exp(np.log1p(-2.0 ** (-5.0 - hd)) * rc)) for hd in range(heads))
    inv_freq = 1.0 / (ROPE_BASE ** (jnp.arange(0, dk, 2, dtype=F32) / dk))
    kern = functools.partial(_retention_kernel, tm=tm, rc=rc, dk=dk, dv=dv, decay=decay)
    return pl.pallas_call(
        kern,
        out_shape=jax.ShapeDtypeStruct((b, s, d), F32),
        grid=(b, s // tm),
        in_specs=[
            pl.BlockSpec((1, tm, d), lambda i, j: (i, j, 0)),
            pl.BlockSpec((1, tm, 1), lambda i, j: (i, j, 0)),
            pl.BlockSpec((1, 6, d), lambda i, j: (i, 0, 0)),
            _const_spec((1, d)),
            _const_spec((1, dk // 2)),
            _const_spec((d, 2 * qk + 2 * vw)),
            _const_spec((vw, d)),
            _const_spec((heads, rc, rc)),
            _const_spec((rc, qk)),
            _const_spec((rc, qk)),
        ],
        out_specs=pl.BlockSpec((1, tm, d), lambda i, j: (i, j, 0)),
        scratch_shapes=[
            pltpu.VMEM((heads, dk, dv), F32),
            pltpu.VMEM((tm, qk), BF16),
            pltpu.VMEM((tm, qk), BF16),
            pltpu.VMEM((tm, qk), BF16),
            pltpu.VMEM((tm, qk), BF16),
            pltpu.VMEM((tm, vw), BF16),
            pltpu.VMEM((tm, vw), BF16),
            pltpu.VMEM((tm, vw), BF16),
        ],
        compiler_params=pltpu.CompilerParams(
            dimension_semantics=("parallel", "arbitrary"),
            vmem_limit_bytes=VMEM_LIMIT_BYTES),
        name="retention",
    )(x, positions.reshape(b, s, 1), mod, norm_g.reshape(1, d), inv_freq.reshape(1, dk // 2),
      w_in.astype(BF16), w_out.astype(BF16), intra, xi_t, zeta_t)


META_E1, META_E2, META_R1, META_R2, META_W1, META_W2 = range(6)
META_ROWS = 8


def _router_kernel(x_ref, mod_ref, ng_ref, wr_ref, br_ref, meta_ref, meta_t_ref, cnt_ref,
                   carry_ref):
    @pl.when((pl.program_id(0) == 0) & (pl.program_id(1) == 0))
    def _():
        carry_ref[...] = jnp.zeros(carry_ref.shape, F32)

    x = x_ref[0]
    mod = mod_ref[0]
    h = _modnorm(x, ng_ref[...], mod[4:5], mod[3:4])
    tm = h.shape[0]
    lane = lax.broadcasted_iota(jnp.int32, (tm, LANES), 1)
    neg = jnp.float32(-jnp.inf)
    logits = jnp.full((tm, LANES), neg, F32)
    for e in range(N_EXPERTS):
        le = jnp.sum(h * wr_ref[e:e + 1, :], axis=-1, keepdims=True)
        logits = jnp.where(lane == e, le, logits)
    logits = logits + br_ref[...]
    m1 = jnp.max(logits, axis=-1, keepdims=True)
    i1 = jnp.min(jnp.where(logits == m1, lane, LANES), axis=-1, keepdims=True)
    rest = jnp.where(lane == i1, neg, logits)
    m2 = jnp.max(rest, axis=-1, keepdims=True)
    i2 = jnp.min(jnp.where(rest == m2, lane, LANES), axis=-1, keepdims=True)
    e2 = jnp.exp(m2 - m1)
    w1 = 1.0 / (1.0 + e2)
    w2 = e2 / (1.0 + e2)
    sel = jnp.where((lane == i1) | (lane == i2), 1.0, 0.0)
    row = lax.broadcasted_iota(jnp.int32, (tm, tm), 0)
    col = lax.broadcasted_iota(jnp.int32, (tm, tm), 1)
    tri = jnp.where(row > col, 1.0, 0.0).astype(BF16)
    before = jnp.dot(tri, sel.astype(BF16), preferred_element_type=F32) + carry_ref[...]
    r1 = jnp.sum(jnp.where(lane == i1, before, 0.0), axis=-1, keepdims=True)
    r2 = jnp.sum(jnp.where(lane == i2, before, 0.0), axis=-1, keepdims=True)
    carry_ref[...] += jnp.sum(sel, axis=0, keepdims=True)
    cnt_ref[...] = carry_ref[...]
    meta = jnp.zeros(logits.shape, F32)
    for k, val in ((META_E1, i1.astype(F32)), (META_E2, i2.astype(F32)), (META_R1, r1),
                   (META_R2, r2), (META_W1, w1), (META_W2, w2)):
        meta = jnp.where(lane == k, val, meta)
    meta_ref[0] = meta
    meta_t_ref[...] = meta.T[:META_ROWS, :]


def _router(x, mod, norm_g, w_router, b_router, *, tm=512):
    b, s, d = x.shape
    ne = w_router.shape[1]
    assert ne == N_EXPERTS
    wr = w_router.T
    br = jnp.zeros((1, LANES), F32).at[0, :ne].set(b_router)
    return pl.pallas_call(
        _router_kernel,
        out_shape=(jax.ShapeDtypeStruct((b, s, LANES), F32),
                   jax.ShapeDtypeStruct((META_ROWS, b * s), F32),
                   jax.ShapeDtypeStruct((1, LANES), F32)),
        grid=(b, s // tm),
        in_specs=[
            pl.BlockSpec((1, tm, d), lambda i, j: (i, j, 0)),
            pl.BlockSpec((1, 6, d), lambda i, j: (i, 0, 0)),
            _const_spec((1, d)),
            _const_spec((ne, d)),
            _const_spec((1, LANES)),
        ],
        out_specs=(pl.BlockSpec((1, tm, LANES), lambda i, j: (i, j, 0)),
                   pl.BlockSpec((META_ROWS, tm), lambda i, j: (0, i * (s // tm) + j)),
                   pl.BlockSpec((1, LANES), lambda i, j: (0, 0))),
        scratch_shapes=[pltpu.VMEM((1, LANES), F32)],
        compiler_params=pltpu.CompilerParams(
            dimension_semantics=("arbitrary", "arbitrary"),
            vmem_limit_bytes=VMEM_LIMIT_BYTES),
        name="router",
    )(x, mod, norm_g.reshape(1, d), wr, br)


N_ZERO_SLOTS = 2 * N_EXPERTS


def _dispatch_kernel(info_ref, pos_ref, x_ref, mod_ref, ng_ref, hs_hbm, hbuf, zeros_ref,
                     sem, zsem, *, td, tr):
    i = pl.program_id(0)
    slot = lax.rem(i, 2)

    def zero_copy(k):
        start = pl.multiple_of(info_ref[N_ZERO_SLOTS + k] * (tr * SUBLANES), tr * SUBLANES)
        return pltpu.make_async_copy(zeros_ref, hs_hbm.at[pl.ds(start, tr * SUBLANES)], zsem)

    @pl.when(i == 0)
    def _():
        zeros_ref[...] = jnp.zeros(zeros_ref.shape, F32)
        for k in range(N_ZERO_SLOTS):
            @pl.when(info_ref[k] == 1)
            def _():
                zero_copy(k).start()
        for k in range(N_ZERO_SLOTS):
            @pl.when(info_ref[k] == 1)
            def _():
                zero_copy(k).wait()

    mod = mod_ref[0]
    h = _modnorm(x_ref[...], ng_ref[...], mod[4:5], mod[3:4])
    _store_slabs(hbuf.at[slot], h, 0, td)

    def hs_slab(p8):
        return hs_hbm.at[pl.ds(pl.multiple_of(p8, SUBLANES), SUBLANES)]

    for t in range(td):
        src = hbuf.at[slot, pl.ds(t * SUBLANES, SUBLANES)]
        pltpu.make_async_copy(src, hs_slab(pos_ref[0, 0, t]), sem.at[slot]).start(priority=0)
        pltpu.make_async_copy(src, hs_slab(pos_ref[0, 0, td + t]),
                              sem.at[slot]).start(priority=1)

    def wait_rows(s):
        n_sub = 2 * td * SUBLANES
        pltpu.make_async_copy(hs_hbm.at[pl.ds(0, n_sub)], hs_hbm.at[pl.ds(0, n_sub)],
                              sem.at[s]).wait()

    @pl.when(i > 0)
    def _():
        wait_rows(1 - slot)

    @pl.when(i == pl.num_programs(0) - 1)
    def _():
        wait_rows(slot)


def _dispatch(x, mod, norm_g, pos, zero_info, *, n_rows, seq, td, tr):
    n, d = x.shape
    assert d == SUBLANES * LANES, "a row must fill exactly one (8, 128) slab"
    kern = functools.partial(_dispatch_kernel, td=td, tr=tr)
    return pl.pallas_call(
        kern,
        out_shape=jax.ShapeDtypeStruct((n_rows * SUBLANES, LANES), F32),
        grid_spec=pltpu.PrefetchScalarGridSpec(
            num_scalar_prefetch=1,
            grid=(n // td,),
            in_specs=[
                pl.BlockSpec((1, 1, 2 * td), lambda i, info: (i, 0, 0),
                             memory_space=pltpu.SMEM),
                pl.BlockSpec((td, d), lambda i, info: (i, 0)),
                pl.BlockSpec((1, 6, d), lambda i, info: ((i * td) // seq, 0, 0)),
                pl.BlockSpec((1, d), lambda i, info: (0, 0)),
            ],
            out_specs=pl.BlockSpec(memory_space=pl.ANY),
            scratch_shapes=[pltpu.VMEM((2, td * SUBLANES, LANES), F32),
                            pltpu.VMEM((tr * SUBLANES, LANES), F32),
                            pltpu.SemaphoreType.DMA((2,)), pltpu.SemaphoreType.DMA],
        ),
        compiler_params=pltpu.CompilerParams(
            dimension_semantics=("arbitrary",), vmem_limit_bytes=VMEM_LIMIT_BYTES),
        name="dispatch",
    )(zero_info, pos, x, mod, norm_g.reshape(1, d))


def _experts_kernel(te_ref, nu_ref, hs_ref, wgu_ref, wd_ref, y_ref, *, ffn, col_block, tr):
    i = pl.program_id(0)

    @pl.when(i < nu_ref[0])
    def _():
        h = _load_slabs(hs_ref, 0, tr).astype(BF16)
        y = jnp.zeros((tr, SUBLANES * LANES), F32)
        for j in range(ffn // col_block):
            g = jnp.dot(h, wgu_ref[0, :, j * col_block:(j + 1) * col_block],
                        preferred_element_type=F32)
            u = jnp.dot(h, wgu_ref[0, :, ffn + j * col_block:ffn + (j + 1) * col_block],
                        preferred_element_type=F32)
            a = (jax.nn.silu(g) * u).astype(BF16)
            y = y + jnp.dot(a, wd_ref[0, j * col_block:(j + 1) * col_block, :],
                            preferred_element_type=F32)
        _store_slabs(y_ref, y, 0, tr)

    @pl.when(i >= nu_ref[0])
    def _():
        y_ref[...] = jnp.zeros(y_ref.shape, F32)


def _experts(hs, tile_expert, n_used, w_gate_up, w_down, *, tr):
    ne, ffn, d = w_down.shape
    n_tiles = hs.shape[0] // (tr * SUBLANES)
    kern = functools.partial(_experts_kernel, ffn=ffn, col_block=256, tr=tr)
    return pl.pallas_call(
        kern,
        out_shape=jax.ShapeDtypeStruct(hs.shape, F32),
        grid_spec=pltpu.PrefetchScalarGridSpec(
            num_scalar_prefetch=2,
            grid=(n_tiles,),
            in_specs=[
                pl.BlockSpec((tr * SUBLANES, LANES),
                             lambda i, te, nu: (jnp.minimum(i, nu[0] - 1), 0)),
                pl.BlockSpec((1, d, 2 * ffn), lambda i, te, nu: (te[i], 0, 0)),
                pl.BlockSpec((1, ffn, d), lambda i, te, nu: (te[i], 0, 0)),
            ],
            out_specs=pl.BlockSpec((tr * SUBLANES, LANES), lambda i, te, nu: (i, 0)),
        ),
        compiler_params=pltpu.CompilerParams(
            dimension_semantics=("arbitrary",), vmem_limit_bytes=VMEM_LIMIT_BYTES),
        name="experts",
    )(tile_expert, n_used, hs, w_gate_up.astype(BF16), w_down.astype(BF16))


def _combine_kernel(posc_ref, posn_ref, x_ref, meta_ref, mod_ref, fg_ref, y_hbm, o_ref,
                    ybuf, sem, *, tc):
    i = pl.program_id(0)
    slot = lax.rem(i, 2)

    def issue(pos_ref, to_slot):
        for t in range(2 * tc):
            src = y_hbm.at[pl.ds(pl.multiple_of(pos_ref[0, 0, t], SUBLANES), SUBLANES)]
            pltpu.make_async_copy(src, ybuf.at[to_slot, pl.ds(t * SUBLANES, SUBLANES)],
                                  sem.at[to_slot]).start(priority=t % 2)

    @pl.when(i == 0)
    def _():
        issue(posc_ref, 0)

    @pl.when(i + 1 < pl.num_programs(0))
    def _():
        issue(posn_ref, 1 - slot)

    pltpu.make_async_copy(y_hbm.at[pl.ds(0, 2 * tc * SUBLANES)], ybuf.at[slot],
                          sem.at[slot]).wait()
    meta = meta_ref[...]
    lane = lax.broadcasted_iota(jnp.int32, meta.shape, 1)
    w1 = jnp.sum(jnp.where(lane == META_W1, meta, 0.0), axis=-1, keepdims=True)
    w2 = jnp.sum(jnp.where(lane == META_W2, meta, 0.0), axis=-1, keepdims=True)
    y = w1 * _load_slabs(ybuf.at[slot], 0, tc) + w2 * _load_slabs(ybuf.at[slot], tc, tc)
    xo = x_ref[...] + mod_ref[0][5:6] * y
    ms = jnp.mean(xo * xo, axis=-1, keepdims=True)
    o_ref[...] = xo * lax.rsqrt(ms + EPS) * fg_ref[...]


def _combine(x, meta, mod, final_g, y_sorted, pos, *, seq, tc):
    n, d = x.shape
    nblk = n // tc
    kern = functools.partial(_combine_kernel, tc=tc)
    smem_pos = lambda f: pl.BlockSpec((1, 1, 2 * tc), f, memory_space=pltpu.SMEM)
    return pl.pallas_call(
        kern,
        out_shape=jax.ShapeDtypeStruct((n, d), F32),
        grid=(nblk,),
        in_specs=[
            smem_pos(lambda i: (i, 0, 0)),
            smem_pos(lambda i: (jnp.minimum(i + 1, nblk - 1), 0, 0)),
            pl.BlockSpec((tc, d), lambda i: (i, 0)),
            pl.BlockSpec((tc, LANES), lambda i: (i, 0)),
            pl.BlockSpec((1, 6, d), lambda i: ((i * tc) // seq, 0, 0)),
            _const_spec((1, d)),
            pl.BlockSpec(memory_space=pl.ANY),
        ],
        out_specs=pl.BlockSpec((tc, d), lambda i: (i, 0)),
        scratch_shapes=[pltpu.VMEM((2, 2 * tc * SUBLANES, LANES), F32),
                        pltpu.SemaphoreType.DMA((2,))],
        compiler_params=pltpu.CompilerParams(
            dimension_semantics=("arbitrary",), vmem_limit_bytes=VMEM_LIMIT_BYTES),
        name="combine",
    )(pos, pos, x, meta, mod, final_g.reshape(1, d), y_sorted)


def _moe_layer(x, mod, norm_g, final_g, w_router, b_router, w_gate_up, w_down,
               *, tr=512, td=1024, tc=256):
    b, s, d = x.shape
    n = b * s
    ne = w_router.shape[1]
    meta, meta_t, cnt = _router(x, mod, norm_g, w_router, b_router)
    meta = meta.reshape(n, LANES)
    e1 = meta_t[META_E1].astype(jnp.int32)
    e2 = meta_t[META_E2].astype(jnp.int32)
    r1 = meta_t[META_R1].astype(jnp.int32)
    r2 = meta_t[META_R2].astype(jnp.int32)
    counts = cnt[0, :ne].astype(jnp.int32)
    tiles_per = (counts + tr - 1) // tr
    tile_end = jnp.cumsum(tiles_per)
    n_used = tile_end[-1]
    group_start = (tile_end - tiles_per) * tr
    pos1 = group_start[e1] + r1
    pos2 = group_start[e2] + r2
    nt = 2 * n // tr + ne
    tile_ids = jnp.arange(nt, dtype=jnp.int32)
    te = jnp.sum((tile_ids[:, None] >= tile_end[None, :]).astype(jnp.int32), axis=1)
    te = jnp.minimum(te, ne - 1)
    te = jnp.where(tile_ids < n_used, te, te[n_used - 1]).astype(jnp.int32)
    zero_tiles = jnp.concatenate([tile_end - 1, n_used + jnp.arange(ne, dtype=jnp.int32)])
    zero_valid = jnp.concatenate([(counts % tr) != 0, n_used + jnp.arange(ne) < nt])
    zero_info = jnp.concatenate([zero_valid.astype(jnp.int32),
                                 jnp.clip(zero_tiles, 0, nt - 1).astype(jnp.int32)])

    def blocked(t):
        return SUBLANES * jnp.concatenate(
            [pos1.reshape(n // t, 1, t), pos2.reshape(n // t, 1, t)], axis=-1)

    hs = _dispatch(x.reshape(n, d), mod, norm_g, blocked(td), zero_info, n_rows=nt * tr,
                   seq=s, td=td, tr=tr)
    y_sorted = _experts(hs, te, n_used.reshape(1).astype(jnp.int32), w_gate_up, w_down, tr=tr)
    out = _combine(x.reshape(n, d), meta, mod, final_g, y_sorted, blocked(tc), seq=s, tc=tc)
    return out.reshape(b, s, d)


def kernel(x, c, positions, ada_w, ada_b, mix_norm_g, ffn_norm_g, sgu_w_in, sgu_ln_g, sgu_ln_b,
           sgu_w_s, sgu_b_s, sgu_w_out, ffn_w_gate_up, ffn_w_down, ret_w_in, ret_w_out,
           moe_w_router, moe_b_router, moe_w_gate_up, moe_w_down, final_norm_g):
    b, s, d = x.shape
    depth = ada_w.shape[0]
    assert depth == 2, "layer 0 = SGU + SwiGLU, layer 1 = retention + MoE"
    mod = _adaln(c, ada_w, ada_b).reshape(depth, b, 6, d)
    ne, _, two_ffn = moe_w_gate_up[0].shape
    ffn = moe_w_down[0].shape[1]
    later = [ffn_w_gate_up[0], ffn_w_down[0], ret_w_in[0], ret_w_out[0],
             moe_w_gate_up[0].reshape(ne * d, two_ffn), moe_w_down[0].reshape(ne * ffn, d)]
    x, (ffn_gu, ffn_dn, ret_in, ret_out, moe_gu, moe_dn) = _sgu_layer(
        x, mod[0], mix_norm_g[0], sgu_w_in[0], sgu_ln_g[0], sgu_ln_b[0], sgu_w_s[0],
        sgu_b_s[0], sgu_w_out[0], later)
    x = _ffn_layer(x, mod[0], ffn_norm_g[0], ffn_gu, ffn_dn)
    x = _retention_layer(x, positions, mod[1], mix_norm_g[1], ret_in, ret_out)
    return _moe_layer(x, mod[1], ffn_norm_g[1], final_norm_g, moe_w_router[0], moe_b_router[0],
                      moe_gu.reshape(ne, d, two_ffn), moe_dn.reshape(ne, ffn, d))
```
